```python
import jax
import jax.numpy as jnp
from jax import lax
import numpy as np

D_MODEL = 1024
BATCH = 2
SEQ = 8192
DEPTH = 2
DEC_BATCH = 32
DEC_SEQ = 8
PAST_LEN = 8192
PAGE_SIZE = 128

A_HEADS = 4
A_HEAD_DIM = 128
A_WIDTH = A_HEADS * A_HEAD_DIM
MLSTM_CHUNK = 128
B_HEADS = 8
B_KV_HEADS = 2
B_HEAD_DIM = 64
B_WIDTH = B_HEADS * B_HEAD_DIM
B_KV_WIDTH = B_KV_HEADS * B_HEAD_DIM
IDX_HEADS = 8
IDX_DIM = 64
MAX_KEEP = 256
QUERY_BLOCK = 128
C_WINDOWS = (2, 4, 8, 16)
C_GROUPS = len(C_WINDOWS)
C_GROUP_WIDTH = 128
C_WIDTH = C_GROUPS * C_GROUP_WIDTH
POOL_BUF = max(C_WINDOWS) - 1
N_EXPERTS = 64
TOP_K = 8
N_EXPERT_GROUPS = 8
TOPK_GROUPS = 4
EXPERT_DIM = 256
SHARED_DIM = 256
ROUTED_SCALE = 2.5
EXPERT_BLOCK = 128
DN_ALPHA = (2 * DEPTH) ** 0.25
DN_BETA = (8 * DEPTH) ** -0.25
LN_EPS = 1e-5

IN_WIDTHS = (A_WIDTH, A_WIDTH, A_WIDTH, A_HEADS, A_HEADS, A_WIDTH,
             B_WIDTH, B_KV_WIDTH, B_KV_WIDTH, IDX_HEADS * IDX_DIM, IDX_DIM, IDX_HEADS,
             C_WIDTH, D_MODEL, D_MODEL, D_MODEL)
IN_WIDTH = sum(IN_WIDTHS)

kernel_name = 'hybrid_mlstm_dsa_pool_moe_step'


def layer_norm(x, g, b):
    xf = x.astype(jnp.float32)
    mu = jnp.mean(xf, axis=-1, keepdims=True)
    var = jnp.mean(jnp.square(xf - mu), axis=-1, keepdims=True)
    y = (xf - mu) * lax.rsqrt(var + LN_EPS) * g.astype(jnp.float32) + b.astype(jnp.float32)
    return y.astype(x.dtype)


def split_projection(x, w_in, b_in):
    proj = jnp.einsum('btd,de->bte', x, w_in) + b_in
    points, acc = [], 0
    for w in IN_WIDTHS[:-1]:
        acc += w
        points.append(acc)
    return jnp.split(proj, points, axis=-1)


def mlstm_prepare(a_q, a_k, a_v, a_i, a_f, b_fgate):
    B, T = a_q.shape[:2]
    def heads(a):
        return jnp.moveaxis(a.astype(jnp.float32).reshape(B, T, A_HEADS, A_HEAD_DIM), 2, 1)
    q = heads(a_q)
    k = heads(a_k) * (A_HEAD_DIM ** -0.5)
    v = heads(a_v)
    li = jnp.moveaxis(a_i.astype(jnp.float32), 2, 1)
    lf = jnp.moveaxis(jax.nn.log_sigmoid(a_f.astype(jnp.float32) + b_fgate.astype(jnp.float32)), 2, 1)
    return q, k, v, li, lf


def mlstm_chunk(state, inputs):
    c_prev, n_prev, m_prev = state
    q, k, v, li, lf = inputs
    L = q.shape[2]
    b = jnp.cumsum(lf, axis=-1)
    causal = jnp.tril(jnp.ones((L, L), dtype=bool))
    d_log = jnp.where(causal, b[..., :, None] - b[..., None, :] + li[..., None, :], -jnp.inf)
    inter = b + m_prev[..., None]
    m_t = jnp.maximum(inter, jnp.max(d_log, axis=-1))
    w_inter = jnp.exp(inter - m_t)
    a = jnp.exp(d_log - m_t[..., None]) * jnp.einsum('bhtd,bhsd->bhts', q, k)
    num = w_inter[..., None] * jnp.einsum('bhtd,bhdv->bhtv', q, c_prev) + jnp.einsum('bhts,bhsv->bhtv', a, v)
    den = w_inter * jnp.einsum('bhtd,bhd->bht', q, n_prev) + jnp.sum(a, axis=-1)
    h = num / jnp.maximum(jnp.abs(den), jnp.exp(-m_t))[..., None]
    m_new = m_t[..., -1]
    w_s = jnp.exp(b[..., -1:] - b + li - m_new[..., None])
    decay = jnp.exp(b[..., -1] + m_prev - m_new)
    c_new = decay[..., None, None] * c_prev + jnp.einsum('bhs,bhsk,bhsv->bhkv', w_s, k, v)
    n_new = decay[..., None] * n_prev + jnp.einsum('bhs,bhsk->bhk', w_s, k)
    return (c_new, n_new, m_new), h


def mlstm_scan(q, k, v, li, lf):
    B, H, T, d = q.shape
    nc = T // MLSTM_CHUNK
    def chunks(a):
        return jnp.moveaxis(a.reshape(B, H, nc, MLSTM_CHUNK, *a.shape[3:]), 2, 0)
    init = (jnp.zeros((B, H, d, d), jnp.float32), jnp.zeros((B, H, d), jnp.float32),
            jnp.full((B, H), -jnp.inf, jnp.float32))
    state, h = lax.scan(mlstm_chunk, init, (chunks(q), chunks(k), chunks(v), chunks(li), chunks(lf)))
    return jnp.moveaxis(h, 0, 2).reshape(B, H, T, d), state


def mlstm_output(h, a_o, a_norm_g):
    hb = jnp.moveaxis(h, 1, 2)
    mu = jnp.mean(hb, axis=-1, keepdims=True)
    var = jnp.mean(jnp.square(hb - mu), axis=-1, keepdims=True)
    hn = ((hb - mu) * lax.rsqrt(var + LN_EPS)).reshape(hb.shape[0], hb.shape[1], A_WIDTH)
    out = jax.nn.sigmoid(a_o.astype(jnp.float32)) * hn * a_norm_g.astype(jnp.float32)
    return out.astype(a_o.dtype)


def sparse_attend(q, qi, wi, q_pos, k, v, ki, n_keep):
    L = k.shape[1]
    dots = jnp.einsum('bqhd,bld->bqhl', qi, ki).astype(jnp.float32) * (IDX_DIM ** -0.5)
    score = jnp.einsum('bqhl,bqh->bql', jax.nn.relu(dots), wi.astype(jnp.float32) * (IDX_HEADS ** -0.5))
    visible = jnp.arange(L)[None, :] <= q_pos[:, None]
    score = jnp.where(visible[None], score, -jnp.inf)
    _, sel = lax.top_k(score, n_keep)
    gather = jax.vmap(lambda rows, idx: rows[idx])
    ks = gather(k, sel)
    vs = gather(v, sel)
    Bn, Qn = q.shape[:2]
    qg = q.reshape(Bn, Qn, B_KV_HEADS, B_HEADS // B_KV_HEADS, B_HEAD_DIM)
    logits = jnp.einsum('bqkgd,bqskd->bqkgs', qg, ks).astype(jnp.float32) * (B_HEAD_DIM ** -0.5)
    ok = (sel <= q_pos[None, :, None])[:, :, None, None, :]
    p = jax.nn.softmax(jnp.where(ok, logits, -jnp.inf), axis=-1)
    o = jnp.einsum('bqkgs,bqskd->bqkgd', p.astype(vs.dtype), vs)
    return o.reshape(Bn, Qn, B_WIDTH)


def dsa_prompt(q, qi, wi, k, v, ki):
    B, T = q.shape[:2]
    n_keep = min(MAX_KEEP, T // 4)
    nb = T // QUERY_BLOCK
    def one_block(i):
        t0 = i * QUERY_BLOCK
        def sl(a):
            return lax.dynamic_slice_in_dim(a, t0, QUERY_BLOCK, axis=1)
        return sparse_attend(sl(q), sl(qi), sl(wi), t0 + jnp.arange(QUERY_BLOCK), k, v, ki, n_keep)
    out = lax.map(one_block, jnp.arange(nb))
    return jnp.moveaxis(out, 0, 1).reshape(B, T, B_WIDTH)


def pool_mix(u, pos0, c_map, c_scale):
    B, R, _ = u.shape
    uf = u.astype(jnp.float32)
    wmax = max(C_WINDOWS)
    cs = jnp.concatenate([jnp.zeros((B, wmax, C_WIDTH), jnp.float32), jnp.cumsum(uf, axis=1)], axis=1)
    pos = pos0 + jnp.arange(R)
    means = []
    for g, w in enumerate(C_WINDOWS):
        lo, hi = g * C_GROUP_WIDTH, (g + 1) * C_GROUP_WIDTH
        win = cs[:, wmax:, lo:hi] - cs[:, wmax - w:wmax - w + R, lo:hi]
        cnt = jnp.minimum(w, pos + 1).astype(jnp.float32)
        means.append(win / cnt[None, :, None])
    pooled = jnp.concatenate(means, axis=-1) - uf
    z = jnp.einsum('btgc,gcd->btgd', pooled.reshape(B, R, C_GROUPS, C_GROUP_WIDTH), c_map.astype(jnp.float32))
    return (z.reshape(B, R, C_WIDTH) * c_scale.astype(jnp.float32)).astype(u.dtype)


def moe_ffn(x2, router_w, router_b, exp_gu, exp_down, sh_gu, sh_down):
    n, d = x2.shape
    f32 = jnp.float32
    scores = jax.nn.sigmoid(x2.astype(f32) @ router_w.astype(f32))
    biased = scores + router_b.astype(f32)
    per_group = N_EXPERTS // N_EXPERT_GROUPS
    group_score = jnp.sum(lax.top_k(biased.reshape(n, N_EXPERT_GROUPS, per_group), 2)[0], axis=-1)
    _, top_groups = lax.top_k(group_score, TOPK_GROUPS)
    group_ok = jnp.any(top_groups[:, :, None] == jnp.arange(N_EXPERT_GROUPS), axis=1)
    expert_ok = jnp.repeat(group_ok, per_group, axis=1)
    _, sel = lax.top_k(jnp.where(expert_ok, biased, -jnp.inf), TOP_K)
    s_sel = jnp.take_along_axis(scores, sel, axis=-1)
    gates = ROUTED_SCALE * s_sel / jnp.sum(s_sel, axis=-1, keepdims=True)
    n_assign = n * TOP_K
    n_blocks = -(-(n_assign + N_EXPERTS * (EXPERT_BLOCK - 1)) // EXPERT_BLOCK)
    n_rows = n_blocks * EXPERT_BLOCK
    flat_e = sel.reshape(-1)
    order = jnp.argsort(flat_e)
    e_sorted = flat_e[order]
    counts = jnp.bincount(flat_e, length=N_EXPERTS)
    padded = (counts + EXPERT_BLOCK - 1) // EXPERT_BLOCK * EXPERT_BLOCK
    p_end = jnp.cumsum(padded)
    starts = jnp.cumsum(counts) - counts
    dest = (p_end - padded)[e_sorted] + jnp.arange(n_assign) - starts[e_sorted]
    tok = (jnp.arange(n_assign) // TOP_K)[order].astype(jnp.int32)
    row_tok = jnp.zeros((n_rows,), jnp.int32).at[dest].set(tok)
    row_gate = jnp.zeros((n_rows,), f32).at[dest].set(gates.reshape(-1)[order])
    block_expert = jnp.minimum(jnp.searchsorted(p_end, jnp.arange(n_blocks) * EXPERT_BLOCK, side='right'), N_EXPERTS - 1)

    def expert_block(args):
        tok_b, gate_b, e = args
        g, u = jnp.split(x2[tok_b] @ exp_gu[e], 2, axis=-1)
        return ((jax.nn.silu(g) * u) @ exp_down[e]).astype(f32) * gate_b[:, None]

    rows = lax.map(expert_block, (row_tok.reshape(n_blocks, EXPERT_BLOCK),
                                  row_gate.reshape(n_blocks, EXPERT_BLOCK), block_expert))
    routed = jnp.zeros((n, d), f32).at[row_tok].add(rows.reshape(n_rows, d))
    sg, su = jnp.split(x2 @ sh_gu, 2, axis=-1)
    shared = (jax.nn.silu(sg) * su) @ sh_down
    return (routed + shared.astype(f32)).astype(x2.dtype)


def finish_layer(x, h_a, h_b, h_c, g_a, g_b, g_c, post):
    (p_a, p_b, p_c, w_out, ln1_g, ln1_b, router_w, router_b,
     exp_gu, exp_down, sh_gu, sh_down, ln2_g, ln2_b) = post
    merged = (jax.nn.sigmoid(g_a) * (h_a @ p_a) + jax.nn.sigmoid(g_b) * (h_b @ p_b)
              + jax.nn.sigmoid(g_c) * (h_c @ p_c))
    x = layer_norm(DN_ALPHA * x + merged @ w_out, ln1_g, ln1_b)
    B, T, D = x.shape
    ff = moe_ffn(x.reshape(B * T, D), router_w, router_b, exp_gu, exp_down, sh_gu, sh_down).reshape(B, T, D)
    return layer_norm(DN_ALPHA * x + ff, ln2_g, ln2_b)


def prompt_layer(x, w_in, b_in, b_fgate, a_norm_g, c_map, c_scale, post):
    B, T, _ = x.shape
    (a_q, a_k, a_v, a_i, a_f, a_o, b_q, b_k, b_v, b_iq, b_ik, b_iw,
     c_u, g_a, g_b, g_c) = split_projection(x, w_in, b_in)
    q, k, v, li, lf = mlstm_prepare(a_q, a_k, a_v, a_i, a_f, b_fgate)
    h, (c_fin, n_fin, m_fin) = mlstm_scan(q, k, v, li, lf)
    h_a = mlstm_output(h, a_o, a_norm_g)
    kb = b_k.reshape(B, T, B_KV_HEADS, B_HEAD_DIM)
    vb = b_v.reshape(B, T, B_KV_HEADS, B_HEAD_DIM)
    h_b = dsa_prompt(b_q.reshape(B, T, B_HEADS, B_HEAD_DIM), b_iq.reshape(B, T, IDX_HEADS, IDX_DIM),
                     b_iw, kb, vb, b_ik)
    h_c = pool_mix(c_u, 0, c_map, c_scale)
    y = finish_layer(x, h_a, h_b, h_c, g_a, g_b, g_c, post)
    dt = x.dtype
    return y, (kb, vb, b_ik, c_fin.astype(dt), n_fin.astype(dt), m_fin.astype(dt), c_u[:, T - POOL_BUF:])


def sample_layer(x, cache_k, cache_v, cache_kidx, st_c, st_n, st_m, st_pool, page_table,
                 w_in, b_in, b_fgate, a_norm_g, c_map, c_scale, post):
    Bd, T, _ = x.shape
    f32 = jnp.float32
    (a_q, a_k, a_v, a_i, a_f, a_o, b_q, b_k, b_v, b_iq, b_ik, b_iw,
     c_u, g_a, g_b, g_c) = split_projection(x, w_in, b_in)
    q, k, v, li, lf = mlstm_prepare(a_q, a_k, a_v, a_i, a_f, b_fgate)
    (c_new, n_new, m_new), h = mlstm_chunk((st_c.astype(f32), st_n.astype(f32), st_m.astype(f32)),
                                           (q, k, v, li, lf))
    h_a = mlstm_output(h, a_o, a_norm_g)
    past_len = page_table.shape[1] * cache_k.shape[1]
    def past(cache):
        return cache[page_table].reshape(Bd, past_len, *cache.shape[2:])
    kb = b_k.reshape(Bd, T, B_KV_HEADS, B_HEAD_DIM)
    vb = b_v.reshape(Bd, T, B_KV_HEADS, B_HEAD_DIM)
    k_all = jnp.concatenate([past(cache_k).astype(kb.dtype), kb], axis=1)
    v_all = jnp.concatenate([past(cache_v).astype(vb.dtype), vb], axis=1)
    ki_all = jnp.concatenate([past(cache_kidx).astype(b_ik.dtype), b_ik], axis=1)
    q_pos = past_len + jnp.arange(T)
    h_b = sparse_attend(b_q.reshape(Bd, T, B_HEADS, B_HEAD_DIM), b_iq.reshape(Bd, T, IDX_HEADS, IDX_DIM),
                        b_iw, q_pos, k_all, v_all, ki_all, min(MAX_KEEP, (past_len + T) // 4))
    n_buf = st_pool.shape[1]
    u_full = jnp.concatenate([st_pool.astype(c_u.dtype), c_u], axis=1)
    h_c = pool_mix(u_full, past_len - n_buf, c_map, c_scale)[:, n_buf:]
    y = finish_layer(x, h_a, h_b, h_c, g_a, g_b, g_c, post)
    dt = x.dtype
    return y, (kb, vb, b_ik, c_new.astype(dt), n_new.astype(dt), m_new.astype(dt), u_full[:, -n_buf:])


def setup_inputs(seed: int = 0) -> dict:
    key = jax.random.key(seed)
    keys = iter(jax.random.split(key, 40))
    def normal(shape, scale):
        return scale * jax.random.normal(next(keys), shape, jnp.float32)
    n_pages = PAST_LEN // PAGE_SIZE
    n_pool = (DEC_BATCH * n_pages * 5) // 4
    x_prompt = normal((BATCH, SEQ, D_MODEL), 1.0)
    x_sample = normal((DEC_BATCH, DEC_SEQ, D_MODEL), 1.0)
    cache_k = normal((DEPTH, n_pool, PAGE_SIZE, B_KV_HEADS, B_HEAD_DIM), 1.0)
    cache_v = normal((DEPTH, n_pool, PAGE_SIZE, B_KV_HEADS, B_HEAD_DIM), 1.0)
    cache_kidx = normal((DEPTH, n_pool, PAGE_SIZE, IDX_DIM), 1.0)
    state_C = normal((DEPTH, DEC_BATCH, A_HEADS, A_HEAD_DIM, A_HEAD_DIM), 0.1)
    state_n = normal((DEPTH, DEC_BATCH, A_HEADS, A_HEAD_DIM), 0.3)
    state_m = normal((DEPTH, DEC_BATCH, A_HEADS), 1.0)
    state_pool = normal((DEPTH, DEC_BATCH, POOL_BUF, C_WIDTH), 1.0)
    page_table = jax.random.permutation(next(keys), n_pool)[:DEC_BATCH * n_pages].reshape(
        DEC_BATCH, n_pages).astype(jnp.int32)
    w_in = normal((DEPTH, D_MODEL, IN_WIDTH), D_MODEL ** -0.5)
    b_in = normal((DEPTH, IN_WIDTH), 0.02)
    b_fgate = jnp.linspace(3.0, 6.0, A_HEADS)[None, :] + normal((DEPTH, A_HEADS), 0.1)
    a_norm_g = 1.0 + normal((DEPTH, A_WIDTH), 0.02)
    c_map = normal((DEPTH, C_GROUPS, C_GROUP_WIDTH, C_GROUP_WIDTH), C_GROUP_WIDTH ** -0.5)
    c_scale = 1.0 + normal((DEPTH, C_WIDTH), 0.02)
    p_a = normal((DEPTH, A_WIDTH, D_MODEL), DN_BETA * A_WIDTH ** -0.5)
    p_b = normal((DEPTH, B_WIDTH, D_MODEL), DN_BETA * B_WIDTH ** -0.5)
    p_c = normal((DEPTH, C_WIDTH, D_MODEL), DN_BETA * C_WIDTH ** -0.5)
    w_out = normal((DEPTH, D_MODEL, D_MODEL), DN_BETA * D_MODEL ** -0.5)
    ln1_g = 1.0 + normal((DEPTH, D_MODEL), 0.02)
    ln1_b = normal((DEPTH, D_MODEL), 0.02)
    router_w = normal((DEPTH, D_MODEL, N_EXPERTS), D_MODEL ** -0.5)
    router_b = normal((DEPTH, N_EXPERTS), 0.01)
    exp_gu = normal((DEPTH, N_EXPERTS, D_MODEL, 2 * EXPERT_DIM), D_MODEL ** -0.5)
    exp_down = normal((DEPTH, N_EXPERTS, EXPERT_DIM, D_MODEL), DN_BETA * EXPERT_DIM ** -0.5)
    sh_gu = normal((DEPTH, D_MODEL, 2 * SHARED_DIM), D_MODEL ** -0.5)
    sh_down = normal((DEPTH, SHARED_DIM, D_MODEL), DN_BETA * SHARED_DIM ** -0.5)
    ln2_g = 1.0 + normal((DEPTH, D_MODEL), 0.02)
    ln2_b = normal((DEPTH, D_MODEL), 0.02)
    return {'x_prompt': x_prompt, 'x_sample': x_sample, 'cache_k': cache_k, 'cache_v': cache_v,
            'cache_kidx': cache_kidx, 'state_C': state_C, 'state_n': state_n, 'state_m': state_m,
            'state_pool': state_pool, 'page_table': page_table, 'w_in': w_in, 'b_in': b_in,
            'b_fgate': b_fgate, 'a_norm_g': a_norm_g, 'c_map': c_map, 'c_scale': c_scale,
            'p_a': p_a, 'p_b': p_b, 'p_c': p_c, 'w_out': w_out, 'ln1_g': ln1_g, 'ln1_b': ln1_b,
            'router_w': router_w, 'router_b': router_b, 'exp_gu': exp_gu, 'exp_down': exp_down,
            'sh_gu': sh_gu, 'sh_down': sh_down, 'ln2_g': ln2_g, 'ln2_b': ln2_b}


def reference(x_prompt, x_sample, cache_k, cache_v, cache_kidx, state_C, state_n, state_m, state_pool,
              page_table, w_in, b_in, b_fgate, a_norm_g, c_map, c_scale, p_a, p_b, p_c, w_out,
              ln1_g, ln1_b, router_w, router_b, exp_gu, exp_down, sh_gu, sh_down, ln2_g, ln2_b):
    y_p, y_s = x_prompt, x_sample
    new_p, new_s = [], []
    for l in range(DEPTH):
        post = (p_a[l], p_b[l], p_c[l], w_out[l], ln1_g[l], ln1_b[l], router_w[l], router_b[l],
                exp_gu[l], exp_down[l], sh_gu[l], sh_down[l], ln2_g[l], ln2_b[l])
        y_p, st_p = prompt_layer(y_p, w_in[l], b_in[l], b_fgate[l], a_norm_g[l], c_map[l], c_scale[l], post)
        y_s, st_s = sample_layer(y_s, cache_k[l], cache_v[l], cache_kidx[l], state_C[l], state_n[l],
                                 state_m[l], state_pool[l], page_table, w_in[l], b_in[l], b_fgate[l],
                                 a_norm_g[l], c_map[l], c_scale[l], post)
        new_p.append(st_p)
        new_s.append(st_s)

    def stacked(states, i):
        return jnp.stack([s[i] for s in states])

    return (y_p, y_s,
            stacked(new_p, 0), stacked(new_s, 0),
            stacked(new_p, 1), stacked(new_s, 1),
            stacked(new_p, 2), stacked(new_s, 2),
            stacked(new_p, 3), stacked(new_s, 3),
            stacked(new_p, 4), stacked(new_s, 4),
            stacked(new_p, 5), stacked(new_s, 5),
            stacked(new_p, 6), stacked(new_s, 6))
```

```python
import functools

import jax
import jax.numpy as jnp
from jax import lax
from jax.experimental import pallas as pl
from jax.experimental.pallas import tpu as pltpu

F32 = jnp.float32
BF16 = jnp.bfloat16
I32 = jnp.int32

A_HEADS = 4
A_HEAD_DIM = 128
A_WIDTH = A_HEADS * A_HEAD_DIM
MLSTM_CHUNK = 128
B_HEADS = 8
B_KV_HEADS = 2
B_HEAD_DIM = 64
B_WIDTH = B_HEADS * B_HEAD_DIM
B_KV_WIDTH = B_KV_HEADS * B_HEAD_DIM
B_GROUP = B_HEADS // B_KV_HEADS
IDX_HEADS = 8
IDX_DIM = 64
MAX_KEEP = 256
QUERY_BLOCK = 128
C_WINDOWS = (2, 4, 8, 16)
C_GROUPS = len(C_WINDOWS)
C_GROUP_WIDTH = 128
C_WIDTH = C_GROUPS * C_GROUP_WIDTH
POOL_BUF = max(C_WINDOWS) - 1
N_EXPERTS = 64
TOP_K = 8
N_EXPERT_GROUPS = 8
TOPK_GROUPS = 4
EXPERT_DIM = 256
ROUTED_SCALE = 2.5
LN_EPS = 1e-5

LANES = 128
SUBLANES = 8
VMEM_LIMIT = 56 * 1024 * 1024

INT_MIN = -2 ** 31
NEG_BIG = -1e30

PROJ_TM = 256
MERGE_TM = 256
ROUTER_TM = 256
POOL_HALO = 16


def _cparams(sem):
    return pltpu.CompilerParams(dimension_semantics=sem, vmem_limit_bytes=VMEM_LIMIT)


def _dot(a, b):
    return jnp.dot(a, b, preferred_element_type=F32)


def _dot_nt(a, b):
    return lax.dot_general(a, b, (((1,), (1,)), ((), ())), preferred_element_type=F32)


def _dot_tn(a, b):
    return lax.dot_general(a, b, (((0,), (0,)), ((), ())), preferred_element_type=F32)


def _sigmoid(x):
    return 1.0 / (1.0 + jnp.exp(-x))


def _log_sigmoid(x):
    return jnp.minimum(x, 0.0) - jnp.log1p(jnp.exp(-jnp.abs(x)))


def _layer_norm(x, g, b):
    mu = jnp.mean(x, axis=-1, keepdims=True)
    xc = x - mu
    var = jnp.mean(xc * xc, axis=-1, keepdims=True)
    return xc * lax.rsqrt(var + LN_EPS) * g + b


def _sortable(s):
    bits = lax.bitcast_convert_type(s, I32)
    return jnp.where(bits < 0, bits ^ jnp.int32(0x7FFFFFFF), bits)


_PROJ_PLAIN = ('a_q', 'a_k', 'a_v', 'a_o', 'b_q', 'b_k', 'b_v', 'b_iq', 'b_ik', 'c_u', 'g_a', 'g_b', 'g_c')


def _proj_kernel(*refs):
    n_plain = len(_PROJ_PLAIN)
    x_ref = refs[0]
    w_refs = refs[1:1 + n_plain]
    b_refs = refs[1 + n_plain:1 + 2 * n_plain]
    wt, bt, wc, bc = refs[1 + 2 * n_plain:5 + 2 * n_plain]
    outs = refs[5 + 2 * n_plain:]
    xb = x_ref[...].astype(BF16)
    for w_ref, b_ref, o_ref in zip(w_refs, b_refs, outs[:n_plain]):
        o_ref[...] = _dot(xb, w_ref[...]) + b_ref[...]
    outs[n_plain][...] = _dot_nt(wt[...], xb) + bt[...]
    outs[n_plain + 1][...] = _dot(xb, wc[...]) + bc[...]


def _project(x, w_in, b_in):
    n, d = x.shape
    widths = (A_WIDTH, A_WIDTH, A_WIDTH, A_HEADS, A_HEADS, A_WIDTH,
              B_WIDTH, B_KV_WIDTH, B_KV_WIDTH, IDX_HEADS * IDX_DIM, IDX_DIM, IDX_HEADS,
              C_WIDTH, d, d, d)
    names = ('a_q', 'a_k', 'a_v', 'a_i', 'a_f', 'a_o', 'b_q', 'b_k', 'b_v', 'b_iq', 'b_ik', 'b_iw',
             'c_u', 'g_a', 'g_b', 'g_c')
    off, acc = {}, 0
    for nm, w in zip(names, widths):
        off[nm] = (acc, w)
        acc += w

    def cols(nm):
        s, w = off[nm]
        return w_in[:, s:s + w], b_in[s:s + w]

    ws, bs = [], []
    for nm in _PROJ_PLAIN:
        w, b = cols(nm)
        ws.append(w.astype(BF16))
        bs.append(b.reshape(1, -1))
    wi, bi = cols('a_i')
    wf, bf = cols('a_f')
    ww, bw = cols('b_iw')
    n_rows = 2 * A_HEADS + IDX_HEADS
    w_t = jnp.concatenate([wi, wf, ww], axis=1).T.astype(BF16)
    b_t = jnp.concatenate([bi, bf, bw]).reshape(n_rows, 1)
    w_c = jnp.pad(jnp.concatenate([wi, wf], axis=1), ((0, 0), (0, LANES - 2 * A_HEADS))).astype(BF16)
    b_c = jnp.pad(jnp.concatenate([bi, bf]), (0, LANES - 2 * A_HEADS)).reshape(1, LANES)

    tm = PROJ_TM
    assert n % tm == 0
    const = lambda i: (0, 0)
    in_specs = [pl.BlockSpec((tm, d), lambda i: (i, 0))]
    in_specs += [pl.BlockSpec(w.shape, const) for w in ws]
    in_specs += [pl.BlockSpec(b.shape, const) for b in bs]
    in_specs += [pl.BlockSpec(a.shape, const) for a in (w_t, b_t, w_c, b_c)]
    out_shape = [jax.ShapeDtypeStruct((n, w.shape[1]), F32) for w in ws]
    out_specs = [pl.BlockSpec((tm, w.shape[1]), lambda i: (i, 0)) for w in ws]
    out_shape += [jax.ShapeDtypeStruct((n_rows, n), F32), jax.ShapeDtypeStruct((n, LANES), F32)]
    out_specs += [pl.BlockSpec((n_rows, tm), lambda i: (0, i)), pl.BlockSpec((tm, LANES), lambda i: (i, 0))]
    res = pl.pallas_call(
        _proj_kernel, grid=(n // tm,), in_specs=in_specs, out_specs=out_specs, out_shape=out_shape,
        compiler_params=_cparams(("parallel",)),
    )(x, *ws, *bs, w_t, b_t, w_c, b_c)
    out = dict(zip(_PROJ_PLAIN, res[:len(_PROJ_PLAIN)]))
    out['gate_rows'] = res[-2]
    out['gate_cols'] = res[-1]
    return out


def _mlstm_kernel(q_ref, k_ref, v_ref, gr_ref, gc_ref, ao_ref, br_ref, bc_ref, ng_ref, c0_ref, n0_ref, m0_ref,
                  h_ref, c_out, n_out, m_out, c_s, n_s, m_s, *, L):
    c = pl.program_id(1)
    nc = pl.num_programs(1)

    @pl.when(c == 0)
    def _():
        c_s[...] = c0_ref[0]
        n_s[...] = n0_ref[0]
        m_s[...] = m0_ref[0]

    row = lax.broadcasted_iota(I32, (L, L), 0)
    col = lax.broadcasted_iota(I32, (L, L), 1)
    tril = row >= col
    triu = row <= col
    gr = gr_ref[0]
    gc = gc_ref[...]
    lf_rows = _log_sigmoid(gr + br_ref[...])
    lf_cols = _log_sigmoid(gc + bc_ref[...])
    k_scale = A_HEAD_DIM ** -0.5
    for h in range(A_HEADS):
        hs = slice(h * A_HEAD_DIM, (h + 1) * A_HEAD_DIM)
        q = q_ref[:, hs]
        k = k_ref[:, hs] * k_scale
        v = v_ref[:, hs]
        li_r = gr[h:h + 1, :]
        lf_r = lf_rows[A_HEADS + h:A_HEADS + h + 1, :]
        li_c = gc[:, h:h + 1]
        lf_c = lf_cols[:, A_HEADS + h:A_HEADS + h + 1]
        b_c = jnp.sum(jnp.where(tril, lf_r, 0.0), axis=-1, keepdims=True)
        b_r = jnp.sum(jnp.where(triu, lf_c, 0.0), axis=0, keepdims=True)
        m_prev = m_s[h]
        c_prev = c_s[h]
        n_prev = n_s[h]
        d_log = jnp.where(tril, b_c - b_r + li_r, -jnp.inf)
        inter = b_c + m_prev
        m_t = jnp.maximum(inter, jnp.max(d_log, axis=-1, keepdims=True))
        w_inter = jnp.exp(inter - m_t)
        qb = q.astype(BF16)
        kb = k.astype(BF16)
        vb = v.astype(BF16)
        a = jnp.exp(d_log - m_t) * _dot_nt(qb, kb)
        num = w_inter * _dot(qb, c_prev.astype(BF16)) + _dot(a.astype(BF16), vb)
        den = w_inter * jnp.sum(q * n_prev, axis=-1, keepdims=True) + jnp.sum(a, axis=-1, keepdims=True)
        hh = num / jnp.maximum(jnp.abs(den), jnp.exp(-m_t))
        m_new = m_t[L - 1:L, :]
        b_last = b_c[L - 1:L, :]
        w_c = jnp.exp(b_last - b_c + li_c - m_new)
        decay = jnp.exp(b_last + m_prev - m_new)
        kw = k * w_c
        c_s[h] = decay * c_prev + _dot_tn(kw.astype(BF16), vb)
        n_s[h] = decay * n_prev + jnp.sum(kw, axis=0, keepdims=True)
        m_s[h] = m_new
        mu = jnp.mean(hh, axis=-1, keepdims=True)
        hc = hh - mu
        var = jnp.mean(hc * hc, axis=-1, keepdims=True)
        hn = hc * lax.rsqrt(var + LN_EPS)
        h_ref[:, hs] = _sigmoid(ao_ref[:, hs]) * hn * ng_ref[:, hs]

    @pl.when(c == nc - 1)
    def _():
        c_out[0] = c_s[...]
        n_out[0] = n_s[...]
        m_out[0] = m_s[...]


def _mlstm(a_q, a_k, a_v, a_o, gate_rows, gate_cols, b_fgate, a_norm_g, c0, n0, m0, *, row0, n_seq, n_chunks, L):
    assert row0 % L == 0
    rb0 = row0 // L
    n_rows = n_seq * n_chunks * L
    H, dh = A_HEADS, A_HEAD_DIM
    tok = lambda b, c: (rb0 + b * n_chunks + c, 0)
    st = lambda b, c: (b, 0, 0, 0)
    const = lambda b, c: (0, 0)
    bias_r = jnp.pad(b_fgate, (A_HEADS, 0)).reshape(2 * A_HEADS, 1)
    bias_c = jnp.pad(b_fgate, (A_HEADS, LANES - 2 * A_HEADS)).reshape(1, LANES)
    outs = pl.pallas_call(
        functools.partial(_mlstm_kernel, L=L),
        grid=(n_seq, n_chunks),
        in_specs=[pl.BlockSpec((L, A_WIDTH), tok)] * 3
        + [pl.BlockSpec((1, 2 * A_HEADS, L), lambda b, c: (b * n_chunks + c, 0, 0)),
           pl.BlockSpec((L, LANES), tok),
           pl.BlockSpec((L, A_WIDTH), tok),
           pl.BlockSpec((2 * A_HEADS, 1), const),
           pl.BlockSpec((1, LANES), const),
           pl.BlockSpec((1, A_WIDTH), const),
           pl.BlockSpec((1, H, dh, dh), st),
           pl.BlockSpec((1, H, 1, dh), st),
           pl.BlockSpec((1, H, 1, 1), st)],
        out_specs=[pl.BlockSpec((L, A_WIDTH), lambda b, c: (b * n_chunks + c, 0)),
                   pl.BlockSpec((1, H, dh, dh), st),
                   pl.BlockSpec((1, H, 1, dh), st),
                   pl.BlockSpec((1, H, 1, 1), st)],
        out_shape=[jax.ShapeDtypeStruct((n_rows, A_WIDTH), F32),
                   jax.ShapeDtypeStruct((n_seq, H, dh, dh), F32),
                   jax.ShapeDtypeStruct((n_seq, H, 1, dh), F32),
                   jax.ShapeDtypeStruct((n_seq, H, 1, 1), F32)],
        scratch_shapes=[pltpu.VMEM((H, dh, dh), F32), pltpu.VMEM((H, 1, dh), F32), pltpu.VMEM((H, 1, 1), F32)],
        compiler_params=_cparams(("parallel", "arbitrary")),
    )(a_q, a_k, a_v, gate_rows, gate_cols, a_o, bias_r, bias_c, a_norm_g.reshape(1, A_WIDTH),
      c0, n0.reshape(n_seq, H, 1, dh), m0.reshape(n_seq, H, 1, 1))
    h, c_new, n_new, m_new = outs
    return h, c_new, n_new.reshape(n_seq, H, dh), m_new.reshape(n_seq, H)


def _threshold_search(count_ge, n_keep, shape):
    def body(j, t):
        cand = t ^ (jnp.int32(1) << (31 - j))
        return jnp.where(count_ge(cand) >= n_keep, cand, t)
    return lax.fori_loop(0, 32, body, jnp.full(shape, INT_MIN, I32))


def _dsa_prompt_kernel(qi_ref, w_ref, ki_ref, q_ref, k_ref, vt_ref, o_ref, s_ref, acc_ref, m_ref, l_ref, *, n_keep):
    QB = QUERY_BLOCK
    i = pl.program_id(1)
    n_chunks = i + 1
    key_i = lax.broadcasted_iota(I32, (QB, QB), 0)
    qry_i = lax.broadcasted_iota(I32, (QB, QB), 1)
    qi = qi_ref[0, 0]
    w = w_ref[0, 0] * (IDX_HEADS ** -0.5)

    def score_body(c, carry):
        ki = ki_ref[0, pl.ds(pl.multiple_of(c * QB, QB), QB), :]
        d = _dot(ki, qi)
        s = jnp.zeros((QB, QB), F32)
        for h in range(IDX_HEADS):
            s = s + jnp.maximum(d[:, h * QB:(h + 1) * QB] * (IDX_DIM ** -0.5), 0.0) * w[h:h + 1, :]
        visible = (c * QB + key_i) <= (i * QB + qry_i)
        s_ref[c] = jnp.where(visible, _sortable(s), INT_MIN)
        return carry

    lax.fori_loop(0, n_chunks, score_body, 0)

    def count(pred):
        def body(c, acc):
            hit = jnp.where(pred(s_ref[c]), 1.0, 0.0)
            return acc + jnp.sum(hit.reshape(QB // SUBLANES, SUBLANES, QB), axis=0)
        acc = lax.fori_loop(0, n_chunks, body, jnp.zeros((SUBLANES, QB), F32))
        return jnp.sum(acc, axis=0, keepdims=True)

    thr = _threshold_search(lambda cand: count(lambda blk: blk >= cand), float(n_keep), (1, QB))
    n_ties = float(n_keep) - count(lambda blk: blk > thr)

    m_ref[...] = jnp.full(m_ref.shape, NEG_BIG, F32)
    l_ref[...] = jnp.zeros(l_ref.shape, F32)
    acc_ref[...] = jnp.zeros(acc_ref.shape, F32)
    q = q_ref[0, 0]
    tri = (key_i >= qry_i).astype(BF16)
    GW = B_GROUP * QB

    def att_body(c, run):
        blk = s_ref[c]
        tie = (blk == thr) & (blk != INT_MIN)
        prefix = _dot(tri, jnp.where(tie, 1.0, 0.0).astype(BF16))
        sel = (blk > thr) | (tie & (run + prefix <= n_ties))
        off = pl.multiple_of(c * QB, QB)
        for g in range(B_KV_HEADS):
            kc = k_ref[0, g, pl.ds(off, QB), :]
            lg = _dot(kc, q[:, g * GW:(g + 1) * GW]) * (B_HEAD_DIM ** -0.5)
            lg = jnp.concatenate([jnp.where(sel, lg[:, j * QB:(j + 1) * QB], NEG_BIG) for j in range(B_GROUP)], axis=1)
            m_old = m_ref[g]
            m_new = jnp.maximum(m_old, jnp.max(lg, axis=0, keepdims=True))
            alpha = jnp.exp(m_old - m_new)
            p = jnp.exp(lg - m_new)
            l_ref[g] = alpha * l_ref[g] + jnp.sum(p, axis=0, keepdims=True)
            vt = vt_ref[0, g, :, pl.ds(off, QB)]
            acc_ref[g] = alpha * acc_ref[g] + _dot(vt, p.astype(BF16))
            m_ref[g] = m_new
        return run + prefix[QB - 1:QB, :]

    lax.fori_loop(0, n_chunks, att_body, jnp.zeros((1, QB), F32))
    for g in range(B_KV_HEADS):
        o_ref[0, 0, g] = acc_ref[g] / l_ref[g]


def _dsa_prompt(b_q, b_iq, iw_rows, b_k, b_v, b_ik, *, B, T):
    QB = QUERY_BLOCK
    nb = T // QB
    n_keep = min(MAX_KEEP, T // 4)

    def head_t(a, heads, dim):
        a = a.reshape(B, nb, QB, heads, dim)
        return jnp.transpose(a, (0, 1, 4, 3, 2)).reshape(B, nb, dim, heads * QB).astype(BF16)

    qi_t = head_t(b_iq, IDX_HEADS, IDX_DIM)
    q_t = head_t(b_q, B_HEADS, B_HEAD_DIM)
    w_t = jnp.transpose(iw_rows.reshape(IDX_HEADS, B, nb, QB), (1, 2, 0, 3))
    ki = b_ik.reshape(B, T, IDX_DIM).astype(BF16)
    kh = jnp.transpose(b_k.reshape(B, T, B_KV_HEADS, B_HEAD_DIM), (0, 2, 1, 3)).astype(BF16)
    vt = jnp.transpose(b_v.reshape(B, T, B_KV_HEADS, B_HEAD_DIM), (0, 2, 3, 1)).astype(BF16)
    GW = B_GROUP * QB
    blk = lambda b, i: (b, i, 0, 0)
    out = pl.pallas_call(
        functools.partial(_dsa_prompt_kernel, n_keep=n_keep),
        grid=(B, nb),
        in_specs=[pl.BlockSpec((1, 1, IDX_DIM, IDX_HEADS * QB), blk),
                  pl.BlockSpec((1, 1, IDX_HEADS, QB), blk),
                  pl.BlockSpec((1, T, IDX_DIM), lambda b, i: (b, 0, 0)),
                  pl.BlockSpec((1, 1, B_HEAD_DIM, B_HEADS * QB), blk),
                  pl.BlockSpec((1, B_KV_HEADS, T, B_HEAD_DIM), lambda b, i: (b, 0, 0, 0)),
                  pl.BlockSpec((1, B_KV_HEADS, B_HEAD_DIM, T), lambda b, i: (b, 0, 0, 0))],
        out_specs=pl.BlockSpec((1, 1, B_KV_HEADS, B_HEAD_DIM, GW), lambda b, i: (b, i, 0, 0, 0)),
        out_shape=jax.ShapeDtypeStruct((B, nb, B_KV_HEADS, B_HEAD_DIM, GW), F32),
        scratch_shapes=[pltpu.VMEM((nb, QB, QB), I32),
                        pltpu.VMEM((B_KV_HEADS, B_HEAD_DIM, GW), F32),
                        pltpu.VMEM((B_KV_HEADS, 1, GW), F32),
                        pltpu.VMEM((B_KV_HEADS, 1, GW), F32)],
        compiler_params=_cparams(("parallel", "arbitrary")),
    )(qi_t, w_t, ki, q_t, kh, vt)
    out = out.reshape(B, nb, B_KV_HEADS, B_HEAD_DIM, B_GROUP, QB)
    return jnp.transpose(out, (0, 1, 5, 2, 4, 3)).reshape(B * T, B_WIDTH)


def _dsa_decode_kernel(pt_ref, qi_ref, w_ref, q_ref, ck_ref, cv_ref, cki_ref, nk_ref, nv_ref, nki_ref,
                       o_ref, s_ref, k_s, v_s, *, n_keep, n_q):
    P = LANES
    p = pl.program_id(1)
    n_pages = pl.num_programs(1)
    n_chunks = n_pages + 1
    n_rows = IDX_HEADS * n_q
    qi = qi_ref[0]
    w = w_ref[0] * (IDX_HEADS ** -0.5)

    def chunk_scores(ki):
        d = _dot_nt(qi, ki)
        t = jnp.maximum(d * (IDX_DIM ** -0.5), 0.0) * w
        s = jnp.zeros((n_q, P), F32)
        for h in range(IDX_HEADS):
            s = s + t[h * n_q:(h + 1) * n_q, :]
        return s

    off = pl.multiple_of(p * P, P)
    s_ref[:, pl.ds(off, P)] = _sortable(chunk_scores(cki_ref[0].astype(BF16)))
    k_s[pl.ds(off, P), :] = ck_ref[0].astype(BF16)
    v_s[pl.ds(off, P), :] = cv_ref[0].astype(BF16)

    @pl.when(p == n_pages - 1)
    def _():
        past = n_pages * P
        key_j = lax.broadcasted_iota(I32, (n_q, P), 1)
        qry_t = lax.broadcasted_iota(I32, (n_q, P), 0)
        visible = key_j <= qry_t
        s_ref[:, pl.ds(past, P)] = jnp.where(visible, _sortable(chunk_scores(nki_ref[0].astype(BF16))), INT_MIN)
        k_s[pl.ds(past, P), :] = nk_ref[0].astype(BF16)
        v_s[pl.ds(past, P), :] = nv_ref[0].astype(BF16)

        def count(pred):
            def body(c, acc):
                blk = s_ref[:, pl.ds(pl.multiple_of(c * P, P), P)]
                return acc + jnp.where(pred(blk), 1.0, 0.0)
            acc = lax.fori_loop(0, n_chunks, body, jnp.zeros((n_q, P), F32))
            return jnp.sum(acc, axis=-1, keepdims=True)

        thr = _threshold_search(lambda cand: count(lambda blk: blk >= cand), float(n_keep), (n_q, 1))
        n_ties = float(n_keep) - count(lambda blk: blk > thr)
        tri = (lax.broadcasted_iota(I32, (P, P), 0) <= lax.broadcasted_iota(I32, (P, P), 1)).astype(BF16)

        def sel_body(c, run):
            o = pl.multiple_of(c * P, P)
            blk = s_ref[:, pl.ds(o, P)]
            tie = (blk == thr) & (blk != INT_MIN)
            prefix = _dot(jnp.where(tie, 1.0, 0.0).astype(BF16), tri)
            sel = (blk > thr) | (tie & (run + prefix <= n_ties))
            s_ref[:, pl.ds(o, P)] = jnp.where(sel, 1, 0).astype(I32)
            return run + prefix[:, P - 1:P]

        lax.fori_loop(0, n_chunks, sel_body, jnp.zeros((n_q, 1), F32))
        sel = s_ref[...] > 0
        q = q_ref[0]
        lg = _dot_nt(q, k_s[...]) * (B_HEAD_DIM ** -0.5)
        L = lg.shape[1]
        lg = jnp.where(sel[None], lg.reshape(B_HEADS, n_q, L), NEG_BIG)
        m = jnp.max(lg, axis=-1, keepdims=True)
        pr = jnp.exp(lg - m)
        den = jnp.sum(pr, axis=-1, keepdims=True)
        o = _dot(pr.reshape(n_rows, L).astype(BF16), v_s[...])
        o_ref[0] = o / den.reshape(n_rows, 1)


def _dsa_decode(b_q, b_iq, iw_rows, b_k, b_v, b_ik, cache_k, cache_v, cache_kidx, page_table, *, Bd, Tq):
    n_pool, page = cache_k.shape[:2]
    assert page == LANES
    n_pages = page_table.shape[1]
    past = n_pages * page
    n_keep = min(MAX_KEEP, (past + Tq) // 4)
    n_rows = B_HEADS * Tq
    kvw = B_KV_HEADS * B_HEAD_DIM

    def rows_hq(a, heads, dim):
        return jnp.transpose(a.reshape(Bd, Tq, heads, dim), (0, 2, 1, 3)).reshape(Bd, heads * Tq, dim)

    qi = rows_hq(b_iq, IDX_HEADS, IDX_DIM).astype(BF16)
    w = jnp.transpose(iw_rows.reshape(IDX_HEADS, Bd, Tq), (1, 0, 2)).reshape(Bd, IDX_HEADS * Tq, 1)
    qh = rows_hq(b_q, B_HEADS, B_HEAD_DIM)
    group = (jnp.arange(n_rows) // Tq) // B_GROUP
    lane_group = jnp.arange(kvw) // B_HEAD_DIM
    q_wide = jnp.where(group[:, None] == lane_group[None, :], jnp.tile(qh, (1, 1, B_KV_HEADS)), 0.0).astype(BF16)

    def pad_new(a, width):
        return jnp.pad(a.reshape(Bd, Tq, width), ((0, 0), (0, page - Tq), (0, 0)))

    nk, nv, nki = pad_new(b_k, kvw), pad_new(b_v, kvw), pad_new(b_ik, IDX_DIM)
    ck = cache_k.reshape(n_pool, page, kvw)
    cv = cache_v.reshape(n_pool, page, kvw)
    n_keys = past + page
    seq = lambda b, p, pt: (b, 0, 0)
    pg = lambda b, p, pt: (pt[b, p], 0, 0)
    out = pl.pallas_call(
        functools.partial(_dsa_decode_kernel, n_keep=n_keep, n_q=Tq),
        grid_spec=pltpu.PrefetchScalarGridSpec(
            num_scalar_prefetch=1,
            grid=(Bd, n_pages),
            in_specs=[pl.BlockSpec((1, n_rows, IDX_DIM), seq),
                      pl.BlockSpec((1, n_rows, 1), seq),
                      pl.BlockSpec((1, n_rows, kvw), seq),
                      pl.BlockSpec((1, page, kvw), pg),
                      pl.BlockSpec((1, page, kvw), pg),
                      pl.BlockSpec((1, page, IDX_DIM), pg),
                      pl.BlockSpec((1, page, kvw), seq),
                      pl.BlockSpec((1, page, kvw), seq),
                      pl.BlockSpec((1, page, IDX_DIM), seq)],
            out_specs=pl.BlockSpec((1, n_rows, kvw), seq),
            scratch_shapes=[pltpu.VMEM((Tq, n_keys), I32),
                            pltpu.VMEM((n_keys, kvw), BF16),
                            pltpu.VMEM((n_keys, kvw), BF16)]),
        out_shape=jax.ShapeDtypeStruct((Bd, n_rows, kvw), F32),
        compiler_params=_cparams(("parallel", "arbitrary")),
    )(page_table, qi, w, q_wide, ck, cv, cache_kidx, nk, nv, nki)
    out = out.reshape(Bd, B_HEADS, Tq, B_KV_HEADS, B_HEAD_DIM)
    out = jnp.concatenate([out[:, g * B_GROUP:(g + 1) * B_GROUP, :, g] for g in range(B_KV_HEADS)], axis=1)
    return jnp.transpose(out, (0, 2, 1, 3)).reshape(Bd * Tq, B_WIDTH)


def _pool_kernel(u_ref, prev_ref, halo0_ref, cmap_ref, cscale_ref, o_ref, ext, *, Tb, pos0):
    j = pl.program_id(1)
    H = POOL_HALO
    ext[0:H, :] = jnp.where(j == 0, halo0_ref[0], prev_ref[0])
    ext[H:H + Tb, :] = u_ref[0]
    pos = pos0 + j * Tb + lax.broadcasted_iota(I32, (Tb, 1), 0)
    for g, wdw in enumerate(C_WINDOWS):
        ls = slice(g * C_GROUP_WIDTH, (g + 1) * C_GROUP_WIDTH)
        win = ext[H:H + Tb, ls]
        for d in range(1, wdw):
            win = win + ext[H - d:H - d + Tb, ls]
        cnt = jnp.minimum(wdw, pos + 1).astype(F32)
        pooled = win / cnt - ext[H:H + Tb, ls]
        z = _dot(pooled.astype(BF16), cmap_ref[g])
        o_ref[0, :, ls] = z * cscale_ref[:, ls]


def _pool(u, prev_src, halo0, c_map, c_scale, *, Tb, pos0):
    Bn, R, C = u.shape
    H = POOL_HALO
    assert R % Tb == 0 and (Tb % H == 0 or R == Tb)
    prev_idx = (lambda b, j: (b, jnp.maximum(j * (Tb // H) - 1, 0), 0)) if R > Tb else (lambda b, j: (b, 0, 0))
    return pl.pallas_call(
        functools.partial(_pool_kernel, Tb=Tb, pos0=pos0),
        grid=(Bn, R // Tb),
        in_specs=[pl.BlockSpec((1, Tb, C), lambda b, j: (b, j, 0)),
                  pl.BlockSpec((1, H, C), prev_idx),
                  pl.BlockSpec((1, H, C), lambda b, j: (b, 0, 0)),
                  pl.BlockSpec(c_map.shape, lambda b, j: (0, 0, 0)),
                  pl.BlockSpec((1, C), lambda b, j: (0, 0))],
        out_specs=pl.BlockSpec((1, Tb, C), lambda b, j: (b, j, 0)),
        out_shape=jax.ShapeDtypeStruct((Bn, R, C), F32),
        scratch_shapes=[pltpu.VMEM((H + Tb, C), F32)],
        compiler_params=_cparams(("parallel", "parallel")),
    )(u, prev_src, halo0, c_map.astype(BF16), c_scale.reshape(1, C))


def _merge_kernel(x_ref, ha_ref, hb_ref, hc_ref, ga_ref, gb_ref, gc_ref, pa_ref, pb_ref, pc_ref, wo_ref,
                  g1_ref, b1_ref, o_ref, *, alpha):
    merged = (_sigmoid(ga_ref[...]) * _dot(ha_ref[...].astype(BF16), pa_ref[...])
              + _sigmoid(gb_ref[...]) * _dot(hb_ref[...].astype(BF16), pb_ref[...])
              + _sigmoid(gc_ref[...]) * _dot(hc_ref[...].astype(BF16), pc_ref[...]))
    y = alpha * x_ref[...] + _dot(merged.astype(BF16), wo_ref[...])
    o_ref[...] = _layer_norm(y, g1_ref[...], b1_ref[...])


def _merge(x, h_a, h_b, h_c, g_a, g_b, g_c, p_a, p_b, p_c, w_out, ln_g, ln_b, *, alpha):
    n, d = x.shape
    tm = MERGE_TM
    assert n % tm == 0
    tok = lambda w: pl.BlockSpec((tm, w), lambda i: (i, 0))
    full = lambda a: pl.BlockSpec(a.shape, lambda i: (0, 0))
    ws = [p_a.astype(BF16), p_b.astype(BF16), p_c.astype(BF16), w_out.astype(BF16), ln_g.reshape(1, d), ln_b.reshape(1, d)]
    return pl.pallas_call(
        functools.partial(_merge_kernel, alpha=alpha),
        grid=(n // tm,),
        in_specs=[tok(d), tok(h_a.shape[1]), tok(h_b.shape[1]), tok(h_c.shape[1]), tok(d), tok(d), tok(d)]
        + [full(a) for a in ws],
        out_specs=tok(d),
        out_shape=jax.ShapeDtypeStruct((n, d), F32),
        compiler_params=_cparams(("parallel",)),
    )(x, h_a, h_b, h_c, g_a, g_b, g_c, *ws)


def _router_kernel(x_ref, w_ref, rb_ref, o_ref):
    tm = x_ref.shape[0]
    E = LANES
    per_group = N_EXPERTS // N_EXPERT_GROUPS
    scores = _sigmoid(_dot(x_ref[...].astype(BF16), w_ref[...]))
    biased = scores + rb_ref[...]
    lane_i = lax.broadcasted_iota(I32, (tm, E), 1)
    lane = lane_i.astype(F32)
    grp = lane_i // per_group
    neg = -jnp.inf

    def first_max(a):
        m = jnp.max(a, axis=-1, keepdims=True)
        idx = jnp.min(jnp.where(a == m, lane, float(E)), axis=-1, keepdims=True)
        return m, idx

    gscore = []
    for g in range(N_EXPERT_GROUPS):
        bg = jnp.where(grp == g, biased, neg)
        m1, i1 = first_max(bg)
        m2, _ = first_max(jnp.where(lane == i1, neg, bg))
        gscore.append(m1 + m2)
    expert_ok = jnp.zeros((tm, E), jnp.bool_)
    for g in range(N_EXPERT_GROUPS):
        beaten = jnp.zeros((tm, 1), F32)
        for o in range(N_EXPERT_GROUPS):
            if o == g:
                continue
            wins = (gscore[o] > gscore[g]) | ((gscore[o] == gscore[g]) & (o < g))
            beaten = beaten + jnp.where(wins, 1.0, 0.0)
        expert_ok = expert_ok | ((grp == g) & (beaten < float(TOPK_GROUPS)))
    rem = jnp.where(expert_ok, biased, neg)
    chosen = jnp.zeros((tm, E), jnp.bool_)
    for _ in range(TOP_K):
        _, idx = first_max(rem)
        hit = lane == idx
        chosen = chosen | hit
        rem = jnp.where(hit, neg, rem)
    s_sel = jnp.where(chosen, scores, 0.0)
    gates = ROUTED_SCALE * s_sel / jnp.sum(s_sel, axis=-1, keepdims=True)
    o_ref[...] = jnp.where(lane_i == N_EXPERTS, 1.0, gates)


def _router(x, router_w, router_b):
    n, d = x.shape
    tm = ROUTER_TM
    assert n % tm == 0
    pad = LANES - N_EXPERTS
    w = jnp.pad(router_w, ((0, 0), (0, pad))).astype(BF16)
    full = lambda a: pl.BlockSpec(a.shape, lambda i: (0, 0))
    rb = jnp.pad(router_b, (0, pad), constant_values=-jnp.inf).reshape(1, LANES)
    return pl.pallas_call(
        _router_kernel, grid=(n // tm,),
        in_specs=[pl.BlockSpec((tm, d), lambda i: (i, 0)), full(w), full(rb)],
        out_specs=pl.BlockSpec((tm, LANES), lambda i: (i, 0)),
        out_shape=jax.ShapeDtypeStruct((n, LANES), F32),
        compiler_params=_cparams(("parallel",)),
    )(x, w, rb)


def _moe_kernel(x_ref, g_ref, wgu_ref, wd_ref, g2_ref, b2_ref, o_ref, xb_s, *, alpha):
    e = pl.program_id(1)
    n_e = pl.num_programs(1)

    @pl.when(e == 0)
    def _():
        xb_s[...] = x_ref[...].astype(BF16)
        o_ref[...] = jnp.zeros(o_ref.shape, F32)

    gates = g_ref[...]
    lane = lax.broadcasted_iota(I32, gates.shape, 1)
    gate = jnp.sum(jnp.where(lane == e, gates, 0.0), axis=-1, keepdims=True)
    gu = _dot(xb_s[...], wgu_ref[0])
    gg = gu[:, :EXPERT_DIM]
    act = gg * _sigmoid(gg) * gu[:, EXPERT_DIM:]
    y = _dot(act.astype(BF16), wd_ref[0])
    o_ref[...] += jnp.where(gate != 0.0, y * gate, 0.0)

    @pl.when(e == n_e - 1)
    def _():
        o_ref[...] = _layer_norm(alpha * x_ref[...] + o_ref[...], g2_ref[...], b2_ref[...])


def _moe(x, gates, wgu, wd, ln_g, ln_b, *, alpha, tm):
    n, d = x.shape
    n_e = wgu.shape[0]
    assert n % tm == 0
    return pl.pallas_call(
        functools.partial(_moe_kernel, alpha=alpha),
        grid=(n // tm, n_e),
        in_specs=[pl.BlockSpec((tm, d), lambda i, e: (i, 0)),
                  pl.BlockSpec((tm, LANES), lambda i, e: (i, 0)),
                  pl.BlockSpec((1,) + wgu.shape[1:], lambda i, e: (e, 0, 0)),
                  pl.BlockSpec((1,) + wd.shape[1:], lambda i, e: (e, 0, 0)),
                  pl.BlockSpec((1, d), lambda i, e: (0, 0)),
                  pl.BlockSpec((1, d), lambda i, e: (0, 0))],
        out_specs=pl.BlockSpec((tm, d), lambda i, e: (i, 0)),
        out_shape=jax.ShapeDtypeStruct((n, d), F32),
        scratch_shapes=[pltpu.VMEM((tm, d), BF16)],
        compiler_params=_cparams(("parallel", "arbitrary")),
    )(x, gates, wgu, wd, ln_g.reshape(1, d), ln_b.reshape(1, d))


def _moe_tile(n):
    for cand in (1280, 1024, 512, 256, 128):
        if n % cand == 0:
            return cand
    return n


def kernel(x_prompt, x_sample, cache_k, cache_v, cache_kidx, state_C, state_n, state_m, state_pool, page_table, w_in, b_in, b_fgate, a_norm_g, c_map, c_scale, p_a, p_b, p_c, w_out, ln1_g, ln1_b, router_w, router_b, exp_gu, exp_down, sh_gu, sh_down, ln2_g, ln2_b):
    B, T, D = x_prompt.shape
    Bd, Td, _ = x_sample.shape
    depth = w_in.shape[0]
    alpha = (2 * depth) ** 0.25
    n_p, n_d = B * T, Bd * Td
    n = n_p + n_d
    L = MLSTM_CHUNK
    nc = T // L
    past_len = page_table.shape[1] * cache_k.shape[2]
    H, dh = A_HEADS, A_HEAD_DIM

    x = jnp.concatenate([x_prompt.reshape(n_p, D), x_sample.reshape(n_d, D)], axis=0)
    new_p, new_s = [], []
    for l in range(depth):
        pr = _project(x, w_in[l], b_in[l])
        gate_rows = pr['gate_rows']
        gr_p = jnp.transpose(gate_rows[:2 * H, :n_p].reshape(2 * H, B * nc, L), (1, 0, 2))
        gr_d = jnp.transpose(gate_rows[:2 * H, n_p:].reshape(2 * H, Bd, Td), (1, 0, 2))
        iw_p = gate_rows[2 * H:, :n_p]
        iw_d = gate_rows[2 * H:, n_p:]

        zc = jnp.zeros((B, H, dh, dh), F32)
        zn = jnp.zeros((B, H, dh), F32)
        zm = jnp.full((B, H), -jnp.inf, F32)
        ml = functools.partial(_mlstm, pr['a_q'], pr['a_k'], pr['a_v'], pr['a_o'])
        ha_p, c_p, nn_p, m_p = ml(gr_p, pr['gate_cols'], b_fgate[l], a_norm_g[l], zc, zn, zm,
                                  row0=0, n_seq=B, n_chunks=nc, L=L)
        ha_d, c_d, nn_d, m_d = ml(gr_d, pr['gate_cols'], b_fgate[l], a_norm_g[l], state_C[l], state_n[l], state_m[l],
                                  row0=n_p, n_seq=Bd, n_chunks=1, L=Td)

        bk, bv, bik = pr['b_k'], pr['b_v'], pr['b_ik']
        hb_p = _dsa_prompt(pr['b_q'][:n_p], pr['b_iq'][:n_p], iw_p, bk[:n_p], bv[:n_p], bik[:n_p], B=B, T=T)
        hb_d = _dsa_decode(pr['b_q'][n_p:], pr['b_iq'][n_p:], iw_d, bk[n_p:], bv[n_p:], bik[n_p:],
                           cache_k[l], cache_v[l], cache_kidx[l], page_table, Bd=Bd, Tq=Td)

        cu = pr['c_u']
        cu_p = cu[:n_p].reshape(B, T, C_WIDTH)
        cu_d = cu[n_p:].reshape(Bd, Td, C_WIDTH)
        zero_halo = jnp.zeros((B, POOL_HALO, C_WIDTH), F32)
        hc_p = _pool(cu_p, cu_p, zero_halo, c_map[l], c_scale[l], Tb=min(T, 512), pos0=0)
        halo_d = jnp.pad(state_pool[l], ((0, 0), (POOL_HALO - POOL_BUF, 0), (0, 0)))
        hc_d = _pool(cu_d, halo_d, halo_d, c_map[l], c_scale[l], Tb=Td, pos0=past_len)

        h_a = jnp.concatenate([ha_p, ha_d], axis=0)
        h_b = jnp.concatenate([hb_p, hb_d], axis=0)
        h_c = jnp.concatenate([hc_p.reshape(n_p, C_WIDTH), hc_d.reshape(n_d, C_WIDTH)], axis=0)
        x1 = _merge(x, h_a, h_b, h_c, pr['g_a'], pr['g_b'], pr['g_c'], p_a[l], p_b[l], p_c[l], w_out[l],
                    ln1_g[l], ln1_b[l], alpha=alpha)
        gates = _router(x1, router_w[l], router_b[l])
        wgu = jnp.concatenate([exp_gu[l].astype(BF16), sh_gu[l].astype(BF16)[None]], axis=0)
        wd = jnp.concatenate([exp_down[l].astype(BF16), sh_down[l].astype(BF16)[None]], axis=0)
        x = _moe(x1, gates, wgu, wd, ln2_g[l], ln2_b[l], alpha=alpha, tm=_moe_tile(n))

        kvs = (B_KV_HEADS, B_HEAD_DIM)
        pool_d = jnp.concatenate([state_pool[l], cu_d], axis=1)[:, -POOL_BUF:]
        new_p.append((bk[:n_p].reshape(B, T, *kvs), bv[:n_p].reshape(B, T, *kvs), bik[:n_p].reshape(B, T, IDX_DIM),
                      c_p, nn_p, m_p, cu_p[:, T - POOL_BUF:]))
        new_s.append((bk[n_p:].reshape(Bd, Td, *kvs), bv[n_p:].reshape(Bd, Td, *kvs),
                      bik[n_p:].reshape(Bd, Td, IDX_DIM), c_d, nn_d, m_d, pool_d))

    def stacked(states, i):
        return jnp.stack([s[i] for s in states])

    outs = [x[:n_p].reshape(B, T, D), x[n_p:].reshape(Bd, Td, D)]
    for i in range(7):
        outs += [stacked(new_p, i), stacked(new_s, i)]
    return tuple(outs)
```

```python
import functools
import math

import jax
import jax.numpy as jnp
from jax import lax
from jax.experimental import pallas as pl
from jax.experimental.pallas import tpu as pltpu

F32 = jnp.float32
BF16 = jnp.bfloat16
I32 = jnp.int32

A_HEADS = 4
A_HEAD_DIM = 128
A_WIDTH = A_HEADS * A_HEAD_DIM
MLSTM_CHUNK = 128
B_HEADS = 8
B_KV_HEADS = 2
B_HEAD_DIM = 64
B_WIDTH = B_HEADS * B_HEAD_DIM
B_KV_WIDTH = B_KV_HEADS * B_HEAD_DIM
B_GROUP = B_HEADS // B_KV_HEADS
IDX_HEADS = 8
IDX_DIM = 64
MAX_KEEP = 256
QUERY_BLOCK = 128
C_WINDOWS = (2, 4, 8, 16)
C_GROUPS = len(C_WINDOWS)
C_GROUP_WIDTH = 128
C_WIDTH = C_GROUPS * C_GROUP_WIDTH
POOL_BUF = max(C_WINDOWS) - 1
N_EXPERTS = 64
TOP_K = 8
N_EXPERT_GROUPS = 8
TOPK_GROUPS = 4
EXPERT_DIM = 256
ROUTED_SCALE = 2.5
LN_EPS = 1e-5

LANES = 128
SUBLANES = 8
VMEM_LIMIT = 56 * 1024 * 1024

INT_MIN = -2 ** 31
NEG_BIG = -1e30

DSA_UNROLL = 4
DECODE_PAGES_PER_STEP = 8
PROJ_TM = 256
MERGE_TM = 256
ROUTER_TM = 256
POOL_HALO = 16


def _cparams(sem):
    return pltpu.CompilerParams(dimension_semantics=sem, vmem_limit_bytes=VMEM_LIMIT)


def _dot(a, b):
    return jnp.dot(a, b, preferred_element_type=F32)


def _dot_nt(a, b):
    return lax.dot_general(a, b, (((1,), (1,)), ((), ())), preferred_element_type=F32)


def _dot_tn(a, b):
    return lax.dot_general(a, b, (((0,), (0,)), ((), ())), preferred_element_type=F32)


def _sigmoid(x):
    return 1.0 / (1.0 + jnp.exp(-x))


def _log_sigmoid(x):
    return jnp.minimum(x, 0.0) - jnp.log1p(jnp.exp(-jnp.abs(x)))


def _layer_norm(x, g, b):
    mu = jnp.mean(x, axis=-1, keepdims=True)
    xc = x - mu
    var = jnp.mean(xc * xc, axis=-1, keepdims=True)
    return xc * lax.rsqrt(var + LN_EPS) * g + b


def _bf16_round(x):
    return x.astype(BF16).astype(F32)


def _exact_pow2(scale):
    m = float(scale)
    assert math.frexp(m)[0] == 0.5, m
    return jnp.asarray(m, BF16)


def _sortable(s):
    bits = lax.bitcast_convert_type(s, I32)
    return jnp.where(bits < 0, bits ^ jnp.int32(0x7FFFFFFF), bits)


_PROJ_PLAIN = ('a_q', 'a_k', 'a_v', 'a_o', 'b_q', 'b_k', 'b_v', 'b_iq', 'b_ik', 'c_u', 'g_a', 'g_b', 'g_c')


def _proj_kernel(*refs):
    n_plain = len(_PROJ_PLAIN)
    x_ref = refs[0]
    w_refs = refs[1:1 + n_plain]
    b_refs = refs[1 + n_plain:1 + 2 * n_plain]
    wt, bt, wc, bc = refs[1 + 2 * n_plain:5 + 2 * n_plain]
    outs = refs[5 + 2 * n_plain:]
    xb = x_ref[...].astype(BF16)
    for w_ref, b_ref, o_ref in zip(w_refs, b_refs, outs[:n_plain]):
        o_ref[...] = _dot(xb, w_ref[...]) + b_ref[...]
    outs[n_plain][...] = _dot_nt(wt[...], xb) + bt[...]
    outs[n_plain + 1][...] = _dot(xb, wc[...]) + bc[...]


def _project(x, w_in, b_in):
    n, d = x.shape
    widths = (A_WIDTH, A_WIDTH, A_WIDTH, A_HEADS, A_HEADS, A_WIDTH,
              B_WIDTH, B_KV_WIDTH, B_KV_WIDTH, IDX_HEADS * IDX_DIM, IDX_DIM, IDX_HEADS,
              C_WIDTH, d, d, d)
    names = ('a_q', 'a_k', 'a_v', 'a_i', 'a_f', 'a_o', 'b_q', 'b_k', 'b_v', 'b_iq', 'b_ik', 'b_iw',
             'c_u', 'g_a', 'g_b', 'g_c')
    off, acc = {}, 0
    for nm, w in zip(names, widths):
        off[nm] = (acc, w)
        acc += w

    def cols(nm):
        s, w = off[nm]
        return w_in[:, s:s + w], b_in[s:s + w]

    ws, bs = [], []
    for nm in _PROJ_PLAIN:
        w, b = cols(nm)
        ws.append(w.astype(BF16))
        bs.append(b.reshape(1, -1))
    wi, bi = cols('a_i')
    wf, bf = cols('a_f')
    ww, bw = cols('b_iw')
    n_rows = 2 * A_HEADS + IDX_HEADS
    w_t = jnp.concatenate([wi, wf, ww], axis=1).T.astype(BF16)
    b_t = jnp.concatenate([bi, bf, bw]).reshape(n_rows, 1)
    w_c = jnp.pad(jnp.concatenate([wi, wf], axis=1), ((0, 0), (0, LANES - 2 * A_HEADS))).astype(BF16)
    b_c = jnp.pad(jnp.concatenate([bi, bf]), (0, LANES - 2 * A_HEADS)).reshape(1, LANES)

    tm = PROJ_TM
    assert n % tm == 0
    const = lambda i: (0, 0)
    in_specs = [pl.BlockSpec((tm, d), lambda i: (i, 0))]
    in_specs += [pl.BlockSpec(w.shape, const) for w in ws]
    in_specs += [pl.BlockSpec(b.shape, const) for b in bs]
    in_specs += [pl.BlockSpec(a.shape, const) for a in (w_t, b_t, w_c, b_c)]
    out_shape = [jax.ShapeDtypeStruct((n, w.shape[1]), F32) for w in ws]
    out_specs = [pl.BlockSpec((tm, w.shape[1]), lambda i: (i, 0)) for w in ws]
    out_shape += [jax.ShapeDtypeStruct((n_rows, n), F32), jax.ShapeDtypeStruct((n, LANES), F32)]
    out_specs += [pl.BlockSpec((n_rows, tm), lambda i: (0, i)), pl.BlockSpec((tm, LANES), lambda i: (i, 0))]
    res = pl.pallas_call(
        _proj_kernel, grid=(n // tm,), in_specs=in_specs, out_specs=out_specs, out_shape=out_shape,
        compiler_params=_cparams(("parallel",)),
    )(x, *ws, *bs, w_t, b_t, w_c, b_c)
    out = dict(zip(_PROJ_PLAIN, res[:len(_PROJ_PLAIN)]))
    out['gate_rows'] = res[-2]
    out['gate_cols'] = res[-1]
    return out


def _mlstm_kernel(q_ref, k_ref, v_ref, gr_ref, gc_ref, ao_ref, br_ref, bc_ref, ng_ref, c0_ref, n0_ref, m0_ref,
                  h_ref, c_out, n_out, m_out, c_s, n_s, m_s, *, L):
    c = pl.program_id(1)
    nc = pl.num_programs(1)

    @pl.when(c == 0)
    def _():
        c_s[...] = c0_ref[0]
        n_s[...] = n0_ref[0]
        m_s[...] = m0_ref[0]

    row = lax.broadcasted_iota(I32, (L, L), 0)
    col = lax.broadcasted_iota(I32, (L, L), 1)
    tril = row >= col
    triu = row <= col
    gr = gr_ref[0]
    gc = gc_ref[...]
    lf_rows = _log_sigmoid(gr + br_ref[...])
    lf_cols = _log_sigmoid(gc + bc_ref[...])
    k_scale = A_HEAD_DIM ** -0.5
    for h in range(A_HEADS):
        hs = slice(h * A_HEAD_DIM, (h + 1) * A_HEAD_DIM)
        q = q_ref[:, hs]
        k = k_ref[:, hs] * k_scale
        v = v_ref[:, hs]
        li_r = gr[h:h + 1, :]
        lf_r = lf_rows[A_HEADS + h:A_HEADS + h + 1, :]
        li_c = gc[:, h:h + 1]
        lf_c = lf_cols[:, A_HEADS + h:A_HEADS + h + 1]
        b_c = jnp.sum(jnp.where(tril, lf_r, 0.0), axis=-1, keepdims=True)
        b_r = jnp.sum(jnp.where(triu, lf_c, 0.0), axis=0, keepdims=True)
        m_prev = m_s[h]
        c_prev = c_s[h]
        n_prev = n_s[h]
        d_log = jnp.where(tril, b_c - b_r + li_r, -jnp.inf)
        inter = b_c + m_prev
        m_t = jnp.maximum(inter, jnp.max(d_log, axis=-1, keepdims=True))
        w_inter = jnp.exp(inter - m_t)
        qb = q.astype(BF16)
        kb = k.astype(BF16)
        vb = v.astype(BF16)
        a = jnp.exp(d_log - m_t) * _dot_nt(qb, kb)
        num = w_inter * _dot(qb, c_prev.astype(BF16)) + _dot(a.astype(BF16), vb)
        qn = jnp.sum(qb.astype(F32) * _bf16_round(n_prev), axis=-1, keepdims=True)
        den = w_inter * qn + jnp.sum(a, axis=-1, keepdims=True)
        hh = num / jnp.maximum(jnp.abs(den), jnp.exp(-m_t))
        m_new = m_t[L - 1:L, :]
        b_last = b_c[L - 1:L, :]
        w_c = jnp.exp(b_last - b_c + li_c - m_new)
        decay = jnp.exp(b_last + m_prev - m_new)
        kw = k * w_c
        c_s[h] = decay * c_prev + _dot_tn(kw.astype(BF16), vb)
        n_s[h] = decay * n_prev + jnp.sum(kb.astype(F32) * _bf16_round(w_c), axis=0, keepdims=True)
        m_s[h] = m_new
        mu = jnp.mean(hh, axis=-1, keepdims=True)
        hc = hh - mu
        var = jnp.mean(hc * hc, axis=-1, keepdims=True)
        hn = hc * lax.rsqrt(var + LN_EPS)
        h_ref[:, hs] = _sigmoid(ao_ref[:, hs]) * hn * ng_ref[:, hs]

    @pl.when(c == nc - 1)
    def _():
        c_out[0] = c_s[...]
        n_out[0] = n_s[...]
        m_out[0] = m_s[...]


def _mlstm(a_q, a_k, a_v, a_o, gate_rows, gate_cols, b_fgate, a_norm_g, c0, n0, m0, *, row0, n_seq, n_chunks, L):
    assert row0 % L == 0
    rb0 = row0 // L
    n_rows = n_seq * n_chunks * L
    H, dh = A_HEADS, A_HEAD_DIM
    tok = lambda b, c: (rb0 + b * n_chunks + c, 0)
    st = lambda b, c: (b, 0, 0, 0)
    const = lambda b, c: (0, 0)
    bias_r = jnp.pad(b_fgate, (A_HEADS, 0)).reshape(2 * A_HEADS, 1)
    bias_c = jnp.pad(b_fgate, (A_HEADS, LANES - 2 * A_HEADS)).reshape(1, LANES)
    outs = pl.pallas_call(
        functools.partial(_mlstm_kernel, L=L),
        grid=(n_seq, n_chunks),
        in_specs=[pl.BlockSpec((L, A_WIDTH), tok)] * 3
        + [pl.BlockSpec((1, 2 * A_HEADS, L), lambda b, c: (b * n_chunks + c, 0, 0)),
           pl.BlockSpec((L, LANES), tok),
           pl.BlockSpec((L, A_WIDTH), tok),
           pl.BlockSpec((2 * A_HEADS, 1), const),
           pl.BlockSpec((1, LANES), const),
           pl.BlockSpec((1, A_WIDTH), const),
           pl.BlockSpec((1, H, dh, dh), st),
           pl.BlockSpec((1, H, 1, dh), st),
           pl.BlockSpec((1, H, 1, 1), st)],
        out_specs=[pl.BlockSpec((L, A_WIDTH), lambda b, c: (b * n_chunks + c, 0)),
                   pl.BlockSpec((1, H, dh, dh), st),
                   pl.BlockSpec((1, H, 1, dh), st),
                   pl.BlockSpec((1, H, 1, 1), st)],
        out_shape=[jax.ShapeDtypeStruct((n_rows, A_WIDTH), F32),
                   jax.ShapeDtypeStruct((n_seq, H, dh, dh), F32),
                   jax.ShapeDtypeStruct((n_seq, H, 1, dh), F32),
                   jax.ShapeDtypeStruct((n_seq, H, 1, 1), F32)],
        scratch_shapes=[pltpu.VMEM((H, dh, dh), F32), pltpu.VMEM((H, 1, dh), F32), pltpu.VMEM((H, 1, 1), F32)],
        compiler_params=_cparams(("parallel", "arbitrary")),
    )(a_q, a_k, a_v, gate_rows, gate_cols, a_o, bias_r, bias_c, a_norm_g.reshape(1, A_WIDTH),
      c0, n0.reshape(n_seq, H, 1, dh), m0.reshape(n_seq, H, 1, 1))
    h, c_new, n_new, m_new = outs
    return h, c_new, n_new.reshape(n_seq, H, dh), m_new.reshape(n_seq, H)


def _threshold_search(count_ge, n_keep, shape, n_total):
    def body(j, carry):
        t, cnt_t = carry
        cand = t ^ (jnp.int32(1) << (31 - j))
        cnt = count_ge(cand)
        ok = cnt >= n_keep
        return jnp.where(ok, cand, t), jnp.where(ok, cnt, cnt_t)
    return lax.fori_loop(0, 32, body, (jnp.full(shape, INT_MIN, I32), jnp.full(shape, n_total, F32)))


def _tree_sum(parts):
    parts = list(parts)
    while len(parts) > 1:
        parts = [parts[i] + parts[i + 1] for i in range(0, len(parts) - 1, 2)] + parts[len(parts) & ~1:]
    return parts[0]


def _dsa_prompt_kernel(qi_ref, w_ref, ki_ref, q_ref, k_ref, vt_ref, o_ref, s_ref, acc_ref, m_ref, l_ref,
                       bias_s, lg_s, p_s, *, n_keep):
    QB = QUERY_BLOCK
    U = DSA_UNROLL
    SK = U * QB
    PW = 2 * QB
    i = pl.program_id(1)
    n_trips = (i + U) // U
    key_minus_qry = lax.broadcasted_iota(I32, (QB, QB), 0) - lax.broadcasted_iota(I32, (QB, QB), 1)
    qi = qi_ref[0, 0]
    w = _bf16_round(w_ref[0, 0] * (IDX_HEADS ** -0.5))

    def score_body(t, carry):
        for u in range(U):
            c = t * U + u
            ki = ki_ref[0, pl.ds(pl.multiple_of(c * QB, QB), QB), :]
            s = jnp.zeros((QB, QB), F32)
            for hp in range(IDX_HEADS // 2):
                d = _dot(ki, qi[:, hp * PW:(hp + 1) * PW])
                for h in (2 * hp, 2 * hp + 1):
                    s = s + _bf16_round(jnp.maximum(d[:, (h % 2) * QB:(h % 2 + 1) * QB], 0.0)) * w[h:h + 1, :]
            visible = key_minus_qry <= (i - c) * QB
            s_ref[c] = jnp.where(visible, _sortable(s), INT_MIN)
        return carry

    lax.fori_loop(0, n_trips, score_body, 0)

    def count(pred):
        def body(t, acc):
            parts = []
            for u in range(U):
                hit = jnp.where(pred(s_ref[t * U + u]), 1.0, 0.0)
                parts += [hit[r * SUBLANES:(r + 1) * SUBLANES, :] for r in range(QB // SUBLANES)]
            return acc + _tree_sum(parts)
        acc = lax.fori_loop(0, n_trips, body, jnp.zeros((SUBLANES, QB), F32))
        return jnp.sum(acc, axis=0, keepdims=True)

    keep = float(n_keep)
    n_total = (n_trips * SK).astype(F32)
    thr, cnt_thr = _threshold_search(lambda cand: count(lambda blk: blk >= cand), keep, (1, QB), n_total)
    has_ties = jnp.max(jnp.where((thr != INT_MIN) & (cnt_thr > keep), 1.0, 0.0)) > 0.5

    m_ref[...] = jnp.full(m_ref.shape, NEG_BIG, F32)
    l_ref[...] = jnp.zeros(l_ref.shape, F32)
    acc_ref[...] = jnp.zeros(acc_ref.shape, F32)
    q = q_ref[0, 0]
    pairs_per_group = B_GROUP // 2

    def fold(x, op):
        parts = [x[r * SUBLANES:(r + 1) * SUBLANES, :] for r in range(QB // SUBLANES)]
        while len(parts) > 1:
            parts = [op(parts[a], parts[a + 1]) for a in range(0, len(parts), 2)]
        return parts[0]

    def attend(t):
        off = pl.multiple_of(t * SK, SK)
        n_pairs = B_HEADS // 2
        halves = [slice(e * QB, (e + 1) * QB) for e in range(2)]
        mx = [[None, None] for _ in range(n_pairs)]
        for u in range(U):
            bias = bias_s[u]
            for g in range(B_KV_HEADS):
                kc = k_ref[0, g, pl.ds(off + u * QB, QB), :]
                for hp in range(g * pairs_per_group, (g + 1) * pairs_per_group):
                    lg = _dot(kc, q[:, hp * PW:(hp + 1) * PW])
                    for e, cs in enumerate(halves):
                        piece = lg[:, cs] + bias
                        lg_s[hp, u, :, cs] = piece
                        pm = fold(piece, jnp.maximum)
                        mx[hp][e] = pm if mx[hp][e] is None else jnp.maximum(mx[hp][e], pm)
        m_new, alpha = [], []
        for hp in range(n_pairs):
            m_old = m_ref[hp]
            m_new.append(jnp.maximum(m_old, jnp.concatenate(
                [jnp.max(mx[hp][e], axis=0, keepdims=True) for e in range(2)], axis=1)))
            alpha.append(jnp.exp(m_old - m_new[hp]))
            m_ref[hp] = m_new[hp]
        for hp in range(n_pairs):
            ls = [None, None]
            for u in range(U):
                for e, cs in enumerate(halves):
                    p = jnp.exp(lg_s[hp, u, :, cs] - m_new[hp][:, cs])
                    p_s[hp, u * QB:(u + 1) * QB, cs] = p.astype(BF16)
                    ps = fold(p, jnp.add)
                    ls[e] = ps if ls[e] is None else ls[e] + ps
            l_new = jnp.concatenate([jnp.sum(ls[e], axis=0, keepdims=True) for e in range(2)], axis=1)
            l_ref[hp] = alpha[hp] * l_ref[hp] + l_new
        for g in range(B_KV_HEADS):
            vt = vt_ref[0, g, :, pl.ds(off, SK)]
            for hp in range(g * pairs_per_group, (g + 1) * pairs_per_group):
                acc_ref[hp] = alpha[hp] * acc_ref[hp] + _dot(vt, p_s[hp])

    @pl.when(jnp.logical_not(has_ties))
    def _():
        thr_eff = jnp.maximum(thr, INT_MIN + 1)

        def body(t, carry):
            for u in range(U):
                bias_s[u] = jnp.where(s_ref[t * U + u] >= thr_eff, 0.0, NEG_BIG)
            attend(t)
            return carry

        lax.fori_loop(0, n_trips, body, 0)

    @pl.when(has_ties)
    def _():
        n_ties = keep - count(lambda blk: blk > thr)
        tri = (lax.broadcasted_iota(I32, (SK, SK), 0) >= lax.broadcasted_iota(I32, (SK, SK), 1)).astype(BF16)

        def body(t, run):
            blk = s_ref[pl.ds(t * U, U)].reshape(SK, QB)
            tie = (blk == thr) & (blk != INT_MIN)
            prefix = _dot(tri, jnp.where(tie, 1.0, 0.0).astype(BF16))
            sel = (blk > thr) | (tie & (run + prefix <= n_ties))
            bias_s[...] = jnp.where(sel, 0.0, NEG_BIG).reshape(U, QB, QB)
            attend(t)
            return run + prefix[SK - 1:SK, :]

        lax.fori_loop(0, n_trips, body, jnp.zeros((1, QB), F32))

    for hp in range(B_HEADS // 2):
        g, pr = divmod(hp, pairs_per_group)
        o_ref[0, 0, g, :, pr * PW:(pr + 1) * PW] = acc_ref[hp] / l_ref[hp]


def _dsa_prompt(b_q, b_iq, iw_rows, b_k, b_v, b_ik, *, B, T):
    QB = QUERY_BLOCK
    nb = T // QB
    n_keep = min(MAX_KEEP, T // 4)

    assert nb % DSA_UNROLL == 0

    def head_t(a, heads, dim):
        a = a.reshape(B, nb, QB, heads, dim).astype(BF16) * _exact_pow2(dim ** -0.5)
        return jnp.transpose(a, (0, 1, 4, 3, 2)).reshape(B, nb, dim, heads * QB)

    qi_t = head_t(b_iq, IDX_HEADS, IDX_DIM)
    q_t = head_t(b_q, B_HEADS, B_HEAD_DIM)
    w_t = jnp.transpose(iw_rows.reshape(IDX_HEADS, B, nb, QB), (1, 2, 0, 3))
    ki = b_ik.reshape(B, T, IDX_DIM).astype(BF16)
    kh = jnp.transpose(b_k.reshape(B, T, B_KV_HEADS, B_HEAD_DIM), (0, 2, 1, 3)).astype(BF16)
    vt = jnp.transpose(b_v.reshape(B, T, B_KV_HEADS, B_HEAD_DIM), (0, 2, 3, 1)).astype(BF16)
    GW = B_GROUP * QB
    n_pairs = B_HEADS // 2
    blk = lambda b, i: (b, i, 0, 0)
    out = pl.pallas_call(
        functools.partial(_dsa_prompt_kernel, n_keep=n_keep),
        grid=(B, nb),
        in_specs=[pl.BlockSpec((1, 1, IDX_DIM, IDX_HEADS * QB), blk),
                  pl.BlockSpec((1, 1, IDX_HEADS, QB), blk),
                  pl.BlockSpec((1, T, IDX_DIM), lambda b, i: (b, 0, 0)),
                  pl.BlockSpec((1, 1, B_HEAD_DIM, B_HEADS * QB), blk),
                  pl.BlockSpec((1, B_KV_HEADS, T, B_HEAD_DIM), lambda b, i: (b, 0, 0, 0)),
                  pl.BlockSpec((1, B_KV_HEADS, B_HEAD_DIM, T), lambda b, i: (b, 0, 0, 0))],
        out_specs=pl.BlockSpec((1, 1, B_KV_HEADS, B_HEAD_DIM, GW), lambda b, i: (b, i, 0, 0, 0)),
        out_shape=jax.ShapeDtypeStruct((B, nb, B_KV_HEADS, B_HEAD_DIM, GW), F32),
        scratch_shapes=[pltpu.VMEM((nb, QB, QB), I32),
                        pltpu.VMEM((n_pairs, B_HEAD_DIM, 2 * QB), F32),
                        pltpu.VMEM((n_pairs, 1, 2 * QB), F32),
                        pltpu.VMEM((n_pairs, 1, 2 * QB), F32),
                        pltpu.VMEM((DSA_UNROLL, QB, QB), F32),
                        pltpu.VMEM((n_pairs, DSA_UNROLL, QB, 2 * QB), F32),
                        pltpu.VMEM((n_pairs, DSA_UNROLL * QB, 2 * QB), BF16)],
        compiler_params=_cparams(("parallel", "arbitrary")),
    )(qi_t, w_t, ki, q_t, kh, vt)
    out = out.reshape(B, nb, B_KV_HEADS, B_HEAD_DIM, B_GROUP, QB)
    return jnp.transpose(out, (0, 1, 5, 2, 4, 3)).reshape(B * T, B_WIDTH)


def _dsa_decode_kernel(pt_ref, qi_ref, w_ref, q_ref, *rest, n_keep, n_q, pps):
    ck_refs, cv_refs, cki_refs = rest[0:pps], rest[pps:2 * pps], rest[2 * pps:3 * pps]
    nk_ref, nv_ref, nki_ref, o_ref, s_ref, k_s, v_s = rest[3 * pps:]
    P = LANES
    p = pl.program_id(1)
    n_steps = pl.num_programs(1)
    n_rows = IDX_HEADS * n_q
    qi = qi_ref[0]
    w = _bf16_round(w_ref[0] * (IDX_HEADS ** -0.5))

    def chunk_scores(ki):
        t = _bf16_round(jnp.maximum(_dot_nt(qi, ki), 0.0)) * w
        s = jnp.zeros((n_q, P), F32)
        for h in range(IDX_HEADS):
            s = s + t[h * n_q:(h + 1) * n_q, :]
        return s

    for j in range(pps):
        off = pl.multiple_of((p * pps + j) * P, P)
        s_ref[:, pl.ds(off, P)] = _sortable(chunk_scores(cki_refs[j][0].astype(BF16)))
        k_s[pl.ds(off, P), :] = ck_refs[j][0].astype(BF16)
        v_s[pl.ds(off, P), :] = cv_refs[j][0].astype(BF16)

    @pl.when(p == n_steps - 1)
    def _():
        n_keys = s_ref.shape[1]
        past = n_keys - P
        n_chunks = n_keys // P
        key_j = lax.broadcasted_iota(I32, (n_q, P), 1)
        qry_t = lax.broadcasted_iota(I32, (n_q, P), 0)
        visible = key_j <= qry_t
        s_ref[:, pl.ds(past, P)] = jnp.where(visible, _sortable(chunk_scores(nki_ref[0].astype(BF16))), INT_MIN)
        k_s[pl.ds(past, P), :] = nk_ref[0].astype(BF16)
        v_s[pl.ds(past, P), :] = nv_ref[0].astype(BF16)

        def count(pred):
            hit = jnp.where(pred(s_ref[...]), 1.0, 0.0)
            return jnp.sum(_tree_sum([hit[:, c * P:(c + 1) * P] for c in range(n_chunks)]), axis=-1, keepdims=True)

        keep = float(n_keep)
        thr, cnt_thr = _threshold_search(lambda cand: count(lambda blk: blk >= cand), keep, (n_q, 1), float(n_keys))
        has_ties = jnp.max(jnp.where((thr != INT_MIN) & (cnt_thr > keep), 1.0, 0.0)) > 0.5

        @pl.when(jnp.logical_not(has_ties))
        def _():
            thr_eff = jnp.maximum(thr, INT_MIN + 1)
            s_ref[...] = jnp.where(s_ref[...] >= thr_eff, 1, 0).astype(I32)

        @pl.when(has_ties)
        def _():
            n_ties = keep - count(lambda blk: blk > thr)
            tri = (lax.broadcasted_iota(I32, (P, P), 0) <= lax.broadcasted_iota(I32, (P, P), 1)).astype(BF16)

            def sel_body(c, run):
                o = pl.multiple_of(c * P, P)
                blk = s_ref[:, pl.ds(o, P)]
                tie = (blk == thr) & (blk != INT_MIN)
                prefix = _dot(jnp.where(tie, 1.0, 0.0).astype(BF16), tri)
                sel = (blk > thr) | (tie & (run + prefix <= n_ties))
                s_ref[:, pl.ds(o, P)] = jnp.where(sel, 1, 0).astype(I32)
                return run + prefix[:, P - 1:P]

            lax.fori_loop(0, n_chunks, sel_body, jnp.zeros((n_q, 1), F32))

        sel = s_ref[...] > 0
        q = q_ref[0]
        lg = _dot_nt(q, k_s[...])
        L = lg.shape[1]
        lg = jnp.where(sel[None], lg.reshape(B_HEADS, n_q, L), NEG_BIG)
        m = jnp.max(lg, axis=-1, keepdims=True)
        pr = jnp.exp(lg - m)
        den = jnp.sum(pr, axis=-1, keepdims=True)
        o = _dot(pr.reshape(n_rows, L).astype(BF16), v_s[...])
        o_ref[0] = o / den.reshape(n_rows, 1)


def _dsa_decode(b_q, b_iq, iw_rows, b_k, b_v, b_ik, cache_k, cache_v, cache_kidx, page_table, *, Bd, Tq):
    n_pool, page = cache_k.shape[:2]
    assert page == LANES
    n_pages = page_table.shape[1]
    past = n_pages * page
    n_keep = min(MAX_KEEP, (past + Tq) // 4)
    n_rows = B_HEADS * Tq
    kvw = B_KV_HEADS * B_HEAD_DIM

    def rows_hq(a, heads, dim):
        return jnp.transpose(a.reshape(Bd, Tq, heads, dim), (0, 2, 1, 3)).reshape(Bd, heads * Tq, dim)

    qi = rows_hq(b_iq, IDX_HEADS, IDX_DIM).astype(BF16) * _exact_pow2(IDX_DIM ** -0.5)
    w = jnp.transpose(iw_rows.reshape(IDX_HEADS, Bd, Tq), (1, 0, 2)).reshape(Bd, IDX_HEADS * Tq, 1)
    qh = rows_hq(b_q, B_HEADS, B_HEAD_DIM).astype(BF16) * _exact_pow2(B_HEAD_DIM ** -0.5)
    group = (jnp.arange(n_rows) // Tq) // B_GROUP
    lane_group = jnp.arange(kvw) // B_HEAD_DIM
    q_wide = jnp.where(group[:, None] == lane_group[None, :], jnp.tile(qh, (1, 1, B_KV_HEADS)), 0.0).astype(BF16)
    pps = DECODE_PAGES_PER_STEP if n_pages % DECODE_PAGES_PER_STEP == 0 else 1

    def pad_new(a, width):
        return jnp.pad(a.reshape(Bd, Tq, width), ((0, 0), (0, page - Tq), (0, 0)))

    nk, nv, nki = pad_new(b_k, kvw), pad_new(b_v, kvw), pad_new(b_ik, IDX_DIM)
    ck = cache_k.reshape(n_pool, page, kvw)
    cv = cache_v.reshape(n_pool, page, kvw)
    n_keys = past + page
    seq = lambda b, p, pt: (b, 0, 0)
    pages = lambda width: [pl.BlockSpec((1, page, width), lambda b, p, pt, j=j: (pt[b, p * pps + j], 0, 0))
                           for j in range(pps)]
    out = pl.pallas_call(
        functools.partial(_dsa_decode_kernel, n_keep=n_keep, n_q=Tq, pps=pps),
        grid_spec=pltpu.PrefetchScalarGridSpec(
            num_scalar_prefetch=1,
            grid=(Bd, n_pages // pps),
            in_specs=[pl.BlockSpec((1, n_rows, IDX_DIM), seq),
                      pl.BlockSpec((1, n_rows, 1), seq),
                      pl.BlockSpec((1, n_rows, kvw), seq)]
            + pages(kvw) + pages(kvw) + pages(IDX_DIM)
            + [pl.BlockSpec((1, page, kvw), seq),
               pl.BlockSpec((1, page, kvw), seq),
               pl.BlockSpec((1, page, IDX_DIM), seq)],
            out_specs=pl.BlockSpec((1, n_rows, kvw), seq),
            scratch_shapes=[pltpu.VMEM((Tq, n_keys), I32),
                            pltpu.VMEM((n_keys, kvw), BF16),
                            pltpu.VMEM((n_keys, kvw), BF16)]),
        out_shape=jax.ShapeDtypeStruct((Bd, n_rows, kvw), F32),
        compiler_params=_cparams(("parallel", "arbitrary")),
    )(page_table, qi, w, q_wide, *([ck] * pps), *([cv] * pps), *([cache_kidx] * pps), nk, nv, nki)
    out = out.reshape(Bd, B_HEADS, Tq, B_KV_HEADS, B_HEAD_DIM)
    out = jnp.concatenate([out[:, g * B_GROUP:(g + 1) * B_GROUP, :, g] for g in range(B_KV_HEADS)], axis=1)
    return jnp.transpose(out, (0, 2, 1, 3)).reshape(Bd * Tq, B_WIDTH)


def _pool_kernel(u_ref, prev_ref, halo0_ref, cmap_ref, cscale_ref, o_ref, ext, *, Tb, pos0):
    j = pl.program_id(1)
    H = POOL_HALO
    ext[0:H, :] = jnp.where(j == 0, halo0_ref[0], prev_ref[0])
    ext[H:H + Tb, :] = u_ref[0]
    pos = pos0 + j * Tb + lax.broadcasted_iota(I32, (Tb, 1), 0)
    for g, wdw in enumerate(C_WINDOWS):
        ls = slice(g * C_GROUP_WIDTH, (g + 1) * C_GROUP_WIDTH)
        win = ext[H:H + Tb, ls]
        for d in range(1, wdw):
            win = win + ext[H - d:H - d + Tb, ls]
        cnt = jnp.minimum(wdw, pos + 1).astype(F32)
        pooled = win / cnt - ext[H:H + Tb, ls]
        z = _dot(pooled.astype(BF16), cmap_ref[g])
        o_ref[0, :, ls] = z * cscale_ref[:, ls]


def _pool(u, prev_src, halo0, c_map, c_scale, *, Tb, pos0):
    Bn, R, C = u.shape
    H = POOL_HALO
    assert R % Tb == 0 and (Tb % H == 0 or R == Tb)
    prev_idx = (lambda b, j: (b, jnp.maximum(j * (Tb // H) - 1, 0), 0)) if R > Tb else (lambda b, j: (b, 0, 0))
    return pl.pallas_call(
        functools.partial(_pool_kernel, Tb=Tb, pos0=pos0),
        grid=(Bn, R // Tb),
        in_specs=[pl.BlockSpec((1, Tb, C), lambda b, j: (b, j, 0)),
                  pl.BlockSpec((1, H, C), prev_idx),
                  pl.BlockSpec((1, H, C), lambda b, j: (b, 0, 0)),
                  pl.BlockSpec(c_map.shape, lambda b, j: (0, 0, 0)),
                  pl.BlockSpec((1, C), lambda b, j: (0, 0))],
        out_specs=pl.BlockSpec((1, Tb, C), lambda b, j: (b, j, 0)),
        out_shape=jax.ShapeDtypeStruct((Bn, R, C), F32),
        scratch_shapes=[pltpu.VMEM((H + Tb, C), F32)],
        compiler_params=_cparams(("parallel", "parallel")),
    )(u, prev_src, halo0, c_map.astype(BF16), c_scale.reshape(1, C))


def _merge_kernel(x_ref, ha_ref, hb_ref, hc_ref, ga_ref, gb_ref, gc_ref, pa_ref, pb_ref, pc_ref, wo_ref,
                  g1_ref, b1_ref, o_ref, *, alpha):
    merged = (_sigmoid(ga_ref[...]) * _dot(ha_ref[...].astype(BF16), pa_ref[...])
              + _sigmoid(gb_ref[...]) * _dot(hb_ref[...].astype(BF16), pb_ref[...])
              + _sigmoid(gc_ref[...]) * _dot(hc_ref[...].astype(BF16), pc_ref[...]))
    y = alpha * x_ref[...] + _dot(merged.astype(BF16), wo_ref[...])
    o_ref[...] = _layer_norm(y, g1_ref[...], b1_ref[...])


def _merge(x, h_a, h_b, h_c, g_a, g_b, g_c, p_a, p_b, p_c, w_out, ln_g, ln_b, *, alpha):
    n, d = x.shape
    tm = MERGE_TM
    assert n % tm == 0
    tok = lambda w: pl.BlockSpec((tm, w), lambda i: (i, 0))
    full = lambda a: pl.BlockSpec(a.shape, lambda i: (0, 0))
    ws = [p_a.astype(BF16), p_b.astype(BF16), p_c.astype(BF16), w_out.astype(BF16), ln_g.reshape(1, d), ln_b.reshape(1, d)]
    return pl.pallas_call(
        functools.partial(_merge_kernel, alpha=alpha),
        grid=(n // tm,),
        in_specs=[tok(d), tok(h_a.shape[1]), tok(h_b.shape[1]), tok(h_c.shape[1]), tok(d), tok(d), tok(d)]
        + [full(a) for a in ws],
        out_specs=tok(d),
        out_shape=jax.ShapeDtypeStruct((n, d), F32),
        compiler_params=_cparams(("parallel",)),
    )(x, h_a, h_b, h_c, g_a, g_b, g_c, *ws)


def _router_kernel(x_ref, w_ref, rb_ref, o_ref):
    tm = x_ref.shape[0]
    E = LANES
    per_group = N_EXPERTS // N_EXPERT_GROUPS
    scores = _sigmoid(_dot(x_ref[...].astype(BF16), w_ref[...]))
    biased = scores + rb_ref[...]
    lane_i = lax.broadcasted_iota(I32, (tm, E), 1)
    lane = lane_i.astype(F32)
    grp = lane_i // per_group
    neg = -jnp.inf

    def first_max(a):
        m = jnp.max(a, axis=-1, keepdims=True)
        idx = jnp.min(jnp.where(a == m, lane, float(E)), axis=-1, keepdims=True)
        return m, idx

    gscore = []
    for g in range(N_EXPERT_GROUPS):
        bg = jnp.where(grp == g, biased, neg)
        m1, i1 = first_max(bg)
        m2, _ = first_max(jnp.where(lane == i1, neg, bg))
        gscore.append(m1 + m2)
    expert_ok = jnp.zeros((tm, E), jnp.bool_)
    for g in range(N_EXPERT_GROUPS):
        beaten = jnp.zeros((tm, 1), F32)
        for o in range(N_EXPERT_GROUPS):
            if o == g:
                continue
            wins = (gscore[o] > gscore[g]) | ((gscore[o] == gscore[g]) & (o < g))
            beaten = beaten + jnp.where(wins, 1.0, 0.0)
        expert_ok = expert_ok | ((grp == g) & (beaten < float(TOPK_GROUPS)))
    rem = jnp.where(expert_ok, biased, neg)
    chosen = jnp.zeros((tm, E), jnp.bool_)
    for _ in range(TOP_K):
        _, idx = first_max(rem)
        hit = lane == idx
        chosen = chosen | hit
        rem = jnp.where(hit, neg, rem)
    s_sel = jnp.where(chosen, scores, 0.0)
    gates = ROUTED_SCALE * s_sel / jnp.sum(s_sel, axis=-1, keepdims=True)
    o_ref[...] = jnp.where(lane_i == N_EXPERTS, 1.0, gates)


def _router(x, router_w, router_b):
    n, d = x.shape
    tm = ROUTER_TM
    assert n % tm == 0
    pad = LANES - N_EXPERTS
    w = jnp.pad(router_w, ((0, 0), (0, pad))).astype(BF16)
    full = lambda a: pl.BlockSpec(a.shape, lambda i: (0, 0))
    rb = jnp.pad(router_b, (0, pad), constant_values=-jnp.inf).reshape(1, LANES)
    return pl.pallas_call(
        _router_kernel, grid=(n // tm,),
        in_specs=[pl.BlockSpec((tm, d), lambda i: (i, 0)), full(w), full(rb)],
        out_specs=pl.BlockSpec((tm, LANES), lambda i: (i, 0)),
        out_shape=jax.ShapeDtypeStruct((n, LANES), F32),
        compiler_params=_cparams(("parallel",)),
    )(x, w, rb)


def _moe_kernel(x_ref, g_ref, wgu_ref, wd_ref, g2_ref, b2_ref, o_ref, xb_s, *, alpha):
    e = pl.program_id(1)
    n_e = pl.num_programs(1)

    @pl.when(e == 0)
    def _():
        xb_s[...] = x_ref[...].astype(BF16)
        o_ref[...] = jnp.zeros(o_ref.shape, F32)

    gates = g_ref[...]
    lane = lax.broadcasted_iota(I32, gates.shape, 1)
    gate = jnp.sum(jnp.where(lane == e, gates, 0.0), axis=-1, keepdims=True)
    gu = _dot(xb_s[...], wgu_ref[0])
    gg = gu[:, :EXPERT_DIM]
    act = gg * _sigmoid(gg) * gu[:, EXPERT_DIM:]
    y = _dot(act.astype(BF16), wd_ref[0])
    o_ref[...] += jnp.where(gate != 0.0, y * gate, 0.0)

    @pl.when(e == n_e - 1)
    def _():
        o_ref[...] = _layer_norm(alpha * x_ref[...] + o_ref[...], g2_ref[...], b2_ref[...])


def _moe(x, gates, wgu, wd, ln_g, ln_b, *, alpha, tm):
    n, d = x.shape
    n_e = wgu.shape[0]
    assert n % tm == 0
    return pl.pallas_call(
        functools.partial(_moe_kernel, alpha=alpha),
        grid=(n // tm, n_e),
        in_specs=[pl.BlockSpec((tm, d), lambda i, e: (i, 0)),
                  pl.BlockSpec((tm, LANES), lambda i, e: (i, 0)),
                  pl.BlockSpec((1,) + wgu.shape[1:], lambda i, e: (e, 0, 0)),
                  pl.BlockSpec((1,) + wd.shape[1:], lambda i, e: (e, 0, 0)),
                  pl.BlockSpec((1, d), lambda i, e: (0, 0)),
                  pl.BlockSpec((1, d), lambda i, e: (0, 0))],
        out_specs=pl.BlockSpec((tm, d), lambda i, e: (i, 0)),
        out_shape=jax.ShapeDtypeStruct((n, d), F32),
        scratch_shapes=[pltpu.VMEM((tm, d), BF16)],
        compiler_params=_cparams(("parallel", "arbitrary")),
    )(x, gates, wgu, wd, ln_g.reshape(1, d), ln_b.reshape(1, d))


def _moe_tile(n):
    for cand in (1280, 1024, 512, 256, 128):
        if n % cand == 0:
            return cand
    return n


def kernel(x_prompt, x_sample, cache_k, cache_v, cache_kidx, state_C, state_n, state_m, state_pool, page_table, w_in, b_in, b_fgate, a_norm_g, c_map, c_scale, p_a, p_b, p_c, w_out, ln1_g, ln1_b, router_w, router_b, exp_gu, exp_down, sh_gu, sh_down, ln2_g, ln2_b):
    B, T, D = x_prompt.shape
    Bd, Td, _ = x_sample.shape
    depth = w_in.shape[0]
    alpha = (2 * depth) ** 0.25
    n_p, n_d = B * T, Bd * Td
    n = n_p + n_d
    L = MLSTM_CHUNK
    nc = T // L
    past_len = page_table.shape[1] * cache_k.shape[2]
    H, dh = A_HEADS, A_HEAD_DIM

    x = jnp.concatenate([x_prompt.reshape(n_p, D), x_sample.reshape(n_d, D)], axis=0)
    new_p, new_s = [], []
    for l in range(depth):
        pr = _project(x, w_in[l], b_in[l])
        gate_rows = pr['gate_rows']
        gr_p = jnp.transpose(gate_rows[:2 * H, :n_p].reshape(2 * H, B * nc, L), (1, 0, 2))
        gr_d = jnp.transpose(gate_rows[:2 * H, n_p:].reshape(2 * H, Bd, Td), (1, 0, 2))
        iw_p = gate_rows[2 * H:, :n_p]
        iw_d = gate_rows[2 * H:, n_p:]

        zc = jnp.zeros((B, H, dh, dh), F32)
        zn = jnp.zeros((B, H, dh), F32)
        zm = jnp.full((B, H), -jnp.inf, F32)
        ml = functools.partial(_mlstm, pr['a_q'], pr['a_k'], pr['a_v'], pr['a_o'])
        ha_p, c_p, nn_p, m_p = ml(gr_p, pr['gate_cols'], b_fgate[l], a_norm_g[l], zc, zn, zm,
                                  row0=0, n_seq=B, n_chunks=nc, L=L)
        ha_d, c_d, nn_d, m_d = ml(gr_d, pr['gate_cols'], b_fgate[l], a_norm_g[l], state_C[l], state_n[l], state_m[l],
                                  row0=n_p, n_seq=Bd, n_chunks=1, L=Td)

        bk, bv, bik = pr['b_k'], pr['b_v'], pr['b_ik']
        hb_p = _dsa_prompt(pr['b_q'][:n_p], pr['b_iq'][:n_p], iw_p, bk[:n_p], bv[:n_p], bik[:n_p], B=B, T=T)
        hb_d = _dsa_decode(pr['b_q'][n_p:], pr['b_iq'][n_p:], iw_d, bk[n_p:], bv[n_p:], bik[n_p:],
                           cache_k[l], cache_v[l], cache_kidx[l], page_table, Bd=Bd, Tq=Td)

        cu = pr['c_u']
        cu_p = cu[:n_p].reshape(B, T, C_WIDTH)
        cu_d = cu[n_p:].reshape(Bd, Td, C_WIDTH)
        zero_halo = jnp.zeros((B, POOL_HALO, C_WIDTH), F32)
        hc_p = _pool(cu_p, cu_p, zero_halo, c_map[l], c_scale[l], Tb=min(T, 512), pos0=0)
        halo_d = jnp.pad(state_pool[l], ((0, 0), (POOL_HALO - POOL_BUF, 0), (0, 0)))
        hc_d = _pool(cu_d, halo_d, halo_d, c_map[l], c_scale[l], Tb=Td, pos0=past_len)

        h_a = jnp.concatenate([ha_p, ha_d], axis=0)
        h_b = jnp.concatenate([hb_p, hb_d], axis=0)
        h_c = jnp.concatenate([hc_p.reshape(n_p, C_WIDTH), hc_d.reshape(n_d, C_WIDTH)], axis=0)
        x1 = _merge(x, h_a, h_b, h_c, pr['g_a'], pr['g_b'], pr['g_c'], p_a[l], p_b[l], p_c[l], w_out[l],
                    ln1_g[l], ln1_b[l], alpha=alpha)
        gates = _router(x1, router_w[l], router_b[l])
        wgu = jnp.concatenate([exp_gu[l].astype(BF16), sh_gu[l].astype(BF16)[None]], axis=0)
        wd = jnp.concatenate([exp_down[l].astype(BF16), sh_down[l].astype(BF16)[None]], axis=0)
        x = _moe(x1, gates, wgu, wd, ln2_g[l], ln2_b[l], alpha=alpha, tm=_moe_tile(n))

        kvs = (B_KV_HEADS, B_HEAD_DIM)
        pool_d = jnp.concatenate([state_pool[l], cu_d], axis=1)[:, -POOL_BUF:]
        new_p.append((bk[:n_p].reshape(B, T, *kvs), bv[:n_p].reshape(B, T, *kvs), bik[:n_p].reshape(B, T, IDX_DIM),
                      c_p, nn_p, m_p, cu_p[:, T - POOL_BUF:]))
        new_s.append((bk[n_p:].reshape(Bd, Td, *kvs), bv[n_p:].reshape(Bd, Td, *kvs),
                      bik[n_p:].reshape(Bd, Td, IDX_DIM), c_d, nn_d, m_d, pool_d))

    def stacked(states, i):
        return jnp.stack([s[i] for s in states])

    outs = [x[:n_p].reshape(B, T, D), x[n_p:].reshape(Bd, Td, D)]
    for i in range(7):
        outs += [stacked(new_p, i), stacked(new_s, i)]
    return tuple(outs)
```

```python
import functools
import math

import jax
import jax.numpy as jnp
from jax import lax
from jax.experimental import pallas as pl
from jax.experimental.pallas import tpu as pltpu

F32 = jnp.float32
BF16 = jnp.bfloat16
I32 = jnp.int32

A_HEADS = 4
A_HEAD_DIM = 128
A_WIDTH = A_HEADS * A_HEAD_DIM
MLSTM_CHUNK = 128
B_HEADS = 8
B_KV_HEADS = 2
B_HEAD_DIM = 64
B_WIDTH = B_HEADS * B_HEAD_DIM
B_KV_WIDTH = B_KV_HEADS * B_HEAD_DIM
B_GROUP = B_HEADS // B_KV_HEADS
IDX_HEADS = 8
IDX_DIM = 64
MAX_KEEP = 256
QUERY_BLOCK = 128
C_WINDOWS = (2, 4, 8, 16)
C_GROUPS = len(C_WINDOWS)
C_GROUP_WIDTH = 128
C_WIDTH = C_GROUPS * C_GROUP_WIDTH
POOL_BUF = max(C_WINDOWS) - 1
N_EXPERTS = 64
TOP_K = 8
N_EXPERT_GROUPS = 8
TOPK_GROUPS = 4
EXPERT_DIM = 256
ROUTED_SCALE = 2.5
LN_EPS = 1e-5

LANES = 128
SUBLANES = 8
VMEM_LIMIT = 56 * 1024 * 1024

INT_MIN = -2 ** 31
NEG_BIG = -1e30

DSA_UNROLL = 4
DECODE_PAGES_PER_STEP = 8
PROJ_TM = 256
MERGE_TM = 256
ROUTER_TM = 256
POOL_HALO = 16


def _cparams(sem):
    return pltpu.CompilerParams(dimension_semantics=sem, vmem_limit_bytes=VMEM_LIMIT)


def _dot(a, b):
    return jnp.dot(a, b, preferred_element_type=F32)


def _dot_nt(a, b):
    return lax.dot_general(a, b, (((1,), (1,)), ((), ())), preferred_element_type=F32)


def _dot_tn(a, b):
    return lax.dot_general(a, b, (((0,), (0,)), ((), ())), preferred_element_type=F32)


def _sigmoid(x):
    return 1.0 / (1.0 + jnp.exp(-x))


def _log_sigmoid(x):
    return jnp.minimum(x, 0.0) - jnp.log1p(jnp.exp(-jnp.abs(x)))


def _layer_norm(x, g, b):
    mu = jnp.mean(x, axis=-1, keepdims=True)
    xc = x - mu
    var = jnp.mean(xc * xc, axis=-1, keepdims=True)
    return xc * lax.rsqrt(var + LN_EPS) * g + b


def _bf16_round(x):
    return x.astype(BF16).astype(F32)


def _exact_pow2(scale):
    m = float(scale)
    assert math.frexp(m)[0] == 0.5, m
    return jnp.asarray(m, BF16)


def _sortable(s):
    bits = lax.bitcast_convert_type(s, I32)
    return jnp.where(bits < 0, bits ^ jnp.int32(0x7FFFFFFF), bits)


_PROJ_PLAIN = ('a_q', 'a_k', 'a_v', 'a_o', 'b_q', 'b_k', 'b_v', 'b_iq', 'b_ik', 'c_u', 'g_a', 'g_b', 'g_c')


def _proj_kernel(*refs):
    n_plain = len(_PROJ_PLAIN)
    x_ref = refs[0]
    w_refs = refs[1:1 + n_plain]
    b_refs = refs[1 + n_plain:1 + 2 * n_plain]
    wt, bt, wc, bc = refs[1 + 2 * n_plain:5 + 2 * n_plain]
    outs = refs[5 + 2 * n_plain:]
    xb = x_ref[...].astype(BF16)
    for w_ref, b_ref, o_ref in zip(w_refs, b_refs, outs[:n_plain]):
        o_ref[...] = _dot(xb, w_ref[...]) + b_ref[...]
    outs[n_plain][...] = _dot_nt(wt[...], xb) + bt[...]
    outs[n_plain + 1][...] = _dot(xb, wc[...]) + bc[...]


def _project(x, w_in, b_in):
    n, d = x.shape
    widths = (A_WIDTH, A_WIDTH, A_WIDTH, A_HEADS, A_HEADS, A_WIDTH,
              B_WIDTH, B_KV_WIDTH, B_KV_WIDTH, IDX_HEADS * IDX_DIM, IDX_DIM, IDX_HEADS,
              C_WIDTH, d, d, d)
    names = ('a_q', 'a_k', 'a_v', 'a_i', 'a_f', 'a_o', 'b_q', 'b_k', 'b_v', 'b_iq', 'b_ik', 'b_iw',
             'c_u', 'g_a', 'g_b', 'g_c')
    off, acc = {}, 0
    for nm, w in zip(names, widths):
        off[nm] = (acc, w)
        acc += w

    def cols(nm):
        s, w = off[nm]
        return w_in[:, s:s + w], b_in[s:s + w]

    ws, bs = [], []
    for nm in _PROJ_PLAIN:
        w, b = cols(nm)
        ws.append(w.astype(BF16))
        bs.append(b.reshape(1, -1))
    wi, bi = cols('a_i')
    wf, bf = cols('a_f')
    ww, bw = cols('b_iw')
    n_rows = 2 * A_HEADS + IDX_HEADS
    w_t = jnp.concatenate([wi, wf, ww], axis=1).T.astype(BF16)
    b_t = jnp.concatenate([bi, bf, bw]).reshape(n_rows, 1)
    w_c = jnp.pad(jnp.concatenate([wi, wf], axis=1), ((0, 0), (0, LANES - 2 * A_HEADS))).astype(BF16)
    b_c = jnp.pad(jnp.concatenate([bi, bf]), (0, LANES - 2 * A_HEADS)).reshape(1, LANES)

    tm = PROJ_TM
    assert n % tm == 0
    const = lambda i: (0, 0)
    in_specs = [pl.BlockSpec((tm, d), lambda i: (i, 0))]
    in_specs += [pl.BlockSpec(w.shape, const) for w in ws]
    in_specs += [pl.BlockSpec(b.shape, const) for b in bs]
    in_specs += [pl.BlockSpec(a.shape, const) for a in (w_t, b_t, w_c, b_c)]
    out_shape = [jax.ShapeDtypeStruct((n, w.shape[1]), F32) for w in ws]
    out_specs = [pl.BlockSpec((tm, w.shape[1]), lambda i: (i, 0)) for w in ws]
    out_shape += [jax.ShapeDtypeStruct((n_rows, n), F32), jax.ShapeDtypeStruct((n, LANES), F32)]
    out_specs += [pl.BlockSpec((n_rows, tm), lambda i: (0, i)), pl.BlockSpec((tm, LANES), lambda i: (i, 0))]
    res = pl.pallas_call(
        _proj_kernel, grid=(n // tm,), in_specs=in_specs, out_specs=out_specs, out_shape=out_shape,
        compiler_params=_cparams(("parallel",)),
    )(x, *ws, *bs, w_t, b_t, w_c, b_c)
    out = dict(zip(_PROJ_PLAIN, res[:len(_PROJ_PLAIN)]))
    out['gate_rows'] = res[-2]
    out['gate_cols'] = res[-1]
    return out


def _mlstm_kernel(q_ref, k_ref, v_ref, gr_ref, gc_ref, ao_ref, br_ref, bc_ref, ng_ref, c0_ref, n0_ref, m0_ref,
                  h_ref, c_out, n_out, m_out, c_s, n_s, m_s, *, L):
    c = pl.program_id(1)
    nc = pl.num_programs(1)

    @pl.when(c == 0)
    def _():
        c_s[...] = c0_ref[0]
        n_s[...] = n0_ref[0]
        m_s[...] = m0_ref[0]

    row = lax.broadcasted_iota(I32, (L, L), 0)
    col = lax.broadcasted_iota(I32, (L, L), 1)
    tril = row >= col
    triu = row <= col
    gr = gr_ref[0]
    gc = gc_ref[...]
    lf_rows = _log_sigmoid(gr + br_ref[...])
    lf_cols = _log_sigmoid(gc + bc_ref[...])
    k_scale = A_HEAD_DIM ** -0.5
    for h in range(A_HEADS):
        hs = slice(h * A_HEAD_DIM, (h + 1) * A_HEAD_DIM)
        q = q_ref[:, hs]
        k = k_ref[:, hs] * k_scale
        v = v_ref[:, hs]
        li_r = gr[h:h + 1, :]
        lf_r = lf_rows[A_HEADS + h:A_HEADS + h + 1, :]
        li_c = gc[:, h:h + 1]
        lf_c = lf_cols[:, A_HEADS + h:A_HEADS + h + 1]
        b_c = jnp.sum(jnp.where(tril, lf_r, 0.0), axis=-1, keepdims=True)
        b_r = jnp.sum(jnp.where(triu, lf_c, 0.0), axis=0, keepdims=True)
        m_prev = m_s[h]
        c_prev = c_s[h]
        n_prev = n_s[h]
        d_log = jnp.where(tril, b_c - b_r + li_r, -jnp.inf)
        inter = b_c + m_prev
        m_t = jnp.maximum(inter, jnp.max(d_log, axis=-1, keepdims=True))
        w_inter = jnp.exp(inter - m_t)
        qb = q.astype(BF16)
        kb = k.astype(BF16)
        vb = v.astype(BF16)
        a = jnp.exp(d_log - m_t) * _dot_nt(qb, kb)
        num = w_inter * _dot(qb, c_prev.astype(BF16)) + _dot(a.astype(BF16), vb)
        qn = jnp.sum(qb.astype(F32) * _bf16_round(n_prev), axis=-1, keepdims=True)
        den = w_inter * qn + jnp.sum(a, axis=-1, keepdims=True)
        hh = num / jnp.maximum(jnp.abs(den), jnp.exp(-m_t))
        m_new = m_t[L - 1:L, :]
        b_last = b_c[L - 1:L, :]
        w_c = jnp.exp(b_last - b_c + li_c - m_new)
        decay = jnp.exp(b_last + m_prev - m_new)
        kw = k * w_c
        c_s[h] = decay * c_prev + _dot_tn(kw.astype(BF16), vb)
        n_s[h] = decay * n_prev + jnp.sum(kb.astype(F32) * _bf16_round(w_c), axis=0, keepdims=True)
        m_s[h] = m_new
        mu = jnp.mean(hh, axis=-1, keepdims=True)
        hc = hh - mu
        var = jnp.mean(hc * hc, axis=-1, keepdims=True)
        hn = hc * lax.rsqrt(var + LN_EPS)
        h_ref[:, hs] = _sigmoid(ao_ref[:, hs]) * hn * ng_ref[:, hs]

    @pl.when(c == nc - 1)
    def _():
        c_out[0] = c_s[...]
        n_out[0] = n_s[...]
        m_out[0] = m_s[...]


def _mlstm(a_q, a_k, a_v, a_o, gate_rows, gate_cols, b_fgate, a_norm_g, c0, n0, m0, *, row0, n_seq, n_chunks, L):
    assert row0 % L == 0
    rb0 = row0 // L
    n_rows = n_seq * n_chunks * L
    H, dh = A_HEADS, A_HEAD_DIM
    tok = lambda b, c: (rb0 + b * n_chunks + c, 0)
    st = lambda b, c: (b, 0, 0, 0)
    const = lambda b, c: (0, 0)
    bias_r = jnp.pad(b_fgate, (A_HEADS, 0)).reshape(2 * A_HEADS, 1)
    bias_c = jnp.pad(b_fgate, (A_HEADS, LANES - 2 * A_HEADS)).reshape(1, LANES)
    outs = pl.pallas_call(
        functools.partial(_mlstm_kernel, L=L),
        grid=(n_seq, n_chunks),
        in_specs=[pl.BlockSpec((L, A_WIDTH), tok)] * 3
        + [pl.BlockSpec((1, 2 * A_HEADS, L), lambda b, c: (b * n_chunks + c, 0, 0)),
           pl.BlockSpec((L, LANES), tok),
           pl.BlockSpec((L, A_WIDTH), tok),
           pl.BlockSpec((2 * A_HEADS, 1), const),
           pl.BlockSpec((1, LANES), const),
           pl.BlockSpec((1, A_WIDTH), const),
           pl.BlockSpec((1, H, dh, dh), st),
           pl.BlockSpec((1, H, 1, dh), st),
           pl.BlockSpec((1, H, 1, 1), st)],
        out_specs=[pl.BlockSpec((L, A_WIDTH), lambda b, c: (b * n_chunks + c, 0)),
                   pl.BlockSpec((1, H, dh, dh), st),
                   pl.BlockSpec((1, H, 1, dh), st),
                   pl.BlockSpec((1, H, 1, 1), st)],
        out_shape=[jax.ShapeDtypeStruct((n_rows, A_WIDTH), F32),
                   jax.ShapeDtypeStruct((n_seq, H, dh, dh), F32),
                   jax.ShapeDtypeStruct((n_seq, H, 1, dh), F32),
                   jax.ShapeDtypeStruct((n_seq, H, 1, 1), F32)],
        scratch_shapes=[pltpu.VMEM((H, dh, dh), F32), pltpu.VMEM((H, 1, dh), F32), pltpu.VMEM((H, 1, 1), F32)],
        compiler_params=_cparams(("parallel", "arbitrary")),
    )(a_q, a_k, a_v, gate_rows, gate_cols, a_o, bias_r, bias_c, a_norm_g.reshape(1, A_WIDTH),
      c0, n0.reshape(n_seq, H, 1, dh), m0.reshape(n_seq, H, 1, 1))
    h, c_new, n_new, m_new = outs
    return h, c_new, n_new.reshape(n_seq, H, dh), m_new.reshape(n_seq, H)


def _threshold_search(count_ge, n_keep, shape, n_total):
    def body(j, carry):
        t, cnt_t = carry
        cand = t ^ (jnp.int32(1) << (31 - j))
        cnt = count_ge(cand)
        ok = cnt >= n_keep
        return jnp.where(ok, cand, t), jnp.where(ok, cnt, cnt_t)
    return lax.fori_loop(0, 32, body, (jnp.full(shape, INT_MIN, I32), jnp.full(shape, n_total, F32)))


def _tree_sum(parts):
    parts = list(parts)
    while len(parts) > 1:
        parts = [parts[i] + parts[i + 1] for i in range(0, len(parts) - 1, 2)] + parts[len(parts) & ~1:]
    return parts[0]


def _dsa_prompt_kernel(qi_ref, w_ref, ki_ref, q_ref, k_ref, vt_ref, o_ref, s_ref, acc_ref, m_ref, l_ref,
                       bias_s, lg_s, p_s, *, n_keep):
    QB = QUERY_BLOCK
    U = DSA_UNROLL
    SK = U * QB
    PW = 2 * QB
    i = pl.program_id(1)
    n_trips = (i + U) // U
    key_minus_qry = lax.broadcasted_iota(I32, (QB, QB), 0) - lax.broadcasted_iota(I32, (QB, QB), 1)
    qi = qi_ref[0, 0]
    w = _bf16_round(w_ref[0, 0] * (IDX_HEADS ** -0.5))

    def score_body(t, carry):
        for u in range(U):
            c = t * U + u
            ki = ki_ref[0, pl.ds(pl.multiple_of(c * QB, QB), QB), :]
            s = jnp.zeros((QB, QB), F32)
            for hp in range(IDX_HEADS // 2):
                d = _dot(ki, qi[:, hp * PW:(hp + 1) * PW])
                for h in (2 * hp, 2 * hp + 1):
                    s = s + _bf16_round(jnp.maximum(d[:, (h % 2) * QB:(h % 2 + 1) * QB], 0.0)) * w[h:h + 1, :]
            visible = key_minus_qry <= (i - c) * QB
            s_ref[c] = jnp.where(visible, _sortable(s), INT_MIN)
        return carry

    lax.fori_loop(0, n_trips, score_body, 0)

    def count(pred):
        def body(t, acc):
            parts = []
            for u in range(U):
                hit = jnp.where(pred(s_ref[t * U + u]), 1.0, 0.0)
                parts += [hit[r * SUBLANES:(r + 1) * SUBLANES, :] for r in range(QB // SUBLANES)]
            return acc + _tree_sum(parts)
        acc = lax.fori_loop(0, n_trips, body, jnp.zeros((SUBLANES, QB), F32))
        return jnp.sum(acc, axis=0, keepdims=True)

    keep = float(n_keep)
    n_total = (n_trips * SK).astype(F32)
    thr, cnt_thr = _threshold_search(lambda cand: count(lambda blk: blk >= cand), keep, (1, QB), n_total)
    has_ties = jnp.max(jnp.where((thr != INT_MIN) & (cnt_thr > keep), 1.0, 0.0)) > 0.5

    m_ref[...] = jnp.full(m_ref.shape, NEG_BIG, F32)
    l_ref[...] = jnp.zeros(l_ref.shape, F32)
    acc_ref[...] = jnp.zeros(acc_ref.shape, F32)
    q = q_ref[0, 0]
    pairs_per_group = B_GROUP // 2

    def fold(x, op):
        parts = [x[r * SUBLANES:(r + 1) * SUBLANES, :] for r in range(QB // SUBLANES)]
        while len(parts) > 1:
            parts = [op(parts[a], parts[a + 1]) for a in range(0, len(parts), 2)]
        return parts[0]

    def attend(t):
        off = pl.multiple_of(t * SK, SK)
        n_pairs = B_HEADS // 2
        halves = [slice(e * QB, (e + 1) * QB) for e in range(2)]
        mx = [[None, None] for _ in range(n_pairs)]
        for u in range(U):
            bias = bias_s[u]
            for g in range(B_KV_HEADS):
                kc = k_ref[0, g, pl.ds(off + u * QB, QB), :]
                for hp in range(g * pairs_per_group, (g + 1) * pairs_per_group):
                    lg = _dot(kc, q[:, hp * PW:(hp + 1) * PW])
                    for e, cs in enumerate(halves):
                        piece = lg[:, cs] + bias
                        lg_s[hp, u, :, cs] = piece
                        pm = fold(piece, jnp.maximum)
                        mx[hp][e] = pm if mx[hp][e] is None else jnp.maximum(mx[hp][e], pm)
        m_new, alpha = [], []
        for hp in range(n_pairs):
            m_old = m_ref[hp]
            m_new.append(jnp.maximum(m_old, jnp.concatenate(
                [jnp.max(mx[hp][e], axis=0, keepdims=True) for e in range(2)], axis=1)))
            alpha.append(jnp.exp(m_old - m_new[hp]))
            m_ref[hp] = m_new[hp]
        for hp in range(n_pairs):
            ls = [None, None]
            for u in range(U):
                for e, cs in enumerate(halves):
                    p = jnp.exp(lg_s[hp, u, :, cs] - m_new[hp][:, cs])
                    p_s[hp, u * QB:(u + 1) * QB, cs] = p.astype(BF16)
                    ps = fold(p, jnp.add)
                    ls[e] = ps if ls[e] is None else ls[e] + ps
            l_new = jnp.concatenate([jnp.sum(ls[e], axis=0, keepdims=True) for e in range(2)], axis=1)
            l_ref[hp] = alpha[hp] * l_ref[hp] + l_new
        for g in range(B_KV_HEADS):
            vt = vt_ref[0, g, :, pl.ds(off, SK)]
            for hp in range(g * pairs_per_group, (g + 1) * pairs_per_group):
                acc_ref[hp] = alpha[hp] * acc_ref[hp] + _dot(vt, p_s[hp])

    @pl.when(jnp.logical_not(has_ties))
    def _():
        thr_eff = jnp.maximum(thr, INT_MIN + 1)

        def body(t, carry):
            for u in range(U):
                bias_s[u] = jnp.where(s_ref[t * U + u] >= thr_eff, 0.0, NEG_BIG)
            attend(t)
            return carry

        lax.fori_loop(0, n_trips, body, 0)

    @pl.when(has_ties)
    def _():
        n_ties = keep - count(lambda blk: blk > thr)
        tri = (lax.broadcasted_iota(I32, (SK, SK), 0) >= lax.broadcasted_iota(I32, (SK, SK), 1)).astype(BF16)

        def body(t, run):
            blk = s_ref[pl.ds(t * U, U)].reshape(SK, QB)
            tie = (blk == thr) & (blk != INT_MIN)
            prefix = _dot(tri, jnp.where(tie, 1.0, 0.0).astype(BF16))
            sel = (blk > thr) | (tie & (run + prefix <= n_ties))
            bias_s[...] = jnp.where(sel, 0.0, NEG_BIG).reshape(U, QB, QB)
            attend(t)
            return run + prefix[SK - 1:SK, :]

        lax.fori_loop(0, n_trips, body, jnp.zeros((1, QB), F32))

    for hp in range(B_HEADS // 2):
        g, pr = divmod(hp, pairs_per_group)
        o_ref[0, 0, g, :, pr * PW:(pr + 1) * PW] = acc_ref[hp] / l_ref[hp]


def _dsa_prompt(b_q, b_iq, iw_rows, b_k, b_v, b_ik, *, B, T):
    QB = QUERY_BLOCK
    nb = T // QB
    n_keep = min(MAX_KEEP, T // 4)

    assert nb % DSA_UNROLL == 0

    def head_t(a, heads, dim):
        a = a.reshape(B, nb, QB, heads, dim).astype(BF16) * _exact_pow2(dim ** -0.5)
        return jnp.transpose(a, (0, 1, 4, 3, 2)).reshape(B, nb, dim, heads * QB)

    qi_t = head_t(b_iq, IDX_HEADS, IDX_DIM)
    q_t = head_t(b_q, B_HEADS, B_HEAD_DIM)
    w_t = jnp.transpose(iw_rows.reshape(IDX_HEADS, B, nb, QB), (1, 2, 0, 3))
    ki = b_ik.reshape(B, T, IDX_DIM).astype(BF16)
    kh = jnp.transpose(b_k.reshape(B, T, B_KV_HEADS, B_HEAD_DIM), (0, 2, 1, 3)).astype(BF16)
    vt = jnp.transpose(b_v.reshape(B, T, B_KV_HEADS, B_HEAD_DIM), (0, 2, 3, 1)).astype(BF16)
    GW = B_GROUP * QB
    n_pairs = B_HEADS // 2
    blk = lambda b, i: (b, i, 0, 0)
    out = pl.pallas_call(
        functools.partial(_dsa_prompt_kernel, n_keep=n_keep),
        grid=(B, nb),
        in_specs=[pl.BlockSpec((1, 1, IDX_DIM, IDX_HEADS * QB), blk),
                  pl.BlockSpec((1, 1, IDX_HEADS, QB), blk),
                  pl.BlockSpec((1, T, IDX_DIM), lambda b, i: (b, 0, 0)),
                  pl.BlockSpec((1, 1, B_HEAD_DIM, B_HEADS * QB), blk),
                  pl.BlockSpec((1, B_KV_HEADS, T, B_HEAD_DIM), lambda b, i: (b, 0, 0, 0)),
                  pl.BlockSpec((1, B_KV_HEADS, B_HEAD_DIM, T), lambda b, i: (b, 0, 0, 0))],
        out_specs=pl.BlockSpec((1, 1, B_KV_HEADS, B_HEAD_DIM, GW), lambda b, i: (b, i, 0, 0, 0)),
        out_shape=jax.ShapeDtypeStruct((B, nb, B_KV_HEADS, B_HEAD_DIM, GW), F32),
        scratch_shapes=[pltpu.VMEM((nb, QB, QB), I32),
                        pltpu.VMEM((n_pairs, B_HEAD_DIM, 2 * QB), F32),
                        pltpu.VMEM((n_pairs, 1, 2 * QB), F32),
                        pltpu.VMEM((n_pairs, 1, 2 * QB), F32),
                        pltpu.VMEM((DSA_UNROLL, QB, QB), F32),
                        pltpu.VMEM((n_pairs, DSA_UNROLL, QB, 2 * QB), F32),
                        pltpu.VMEM((n_pairs, DSA_UNROLL * QB, 2 * QB), BF16)],
        compiler_params=_cparams(("parallel", "arbitrary")),
    )(qi_t, w_t, ki, q_t, kh, vt)
    out = out.reshape(B, nb, B_KV_HEADS, B_HEAD_DIM, B_GROUP, QB)
    return jnp.transpose(out, (0, 1, 5, 2, 4, 3)).reshape(B * T, B_WIDTH)


def _dsa_decode_kernel(pt_ref, qi_ref, w_ref, q_ref, *rest, n_keep, n_q, pps):
    ck_refs, cv_refs, cki_refs = rest[0:pps], rest[pps:2 * pps], rest[2 * pps:3 * pps]
    nk_ref, nv_ref, nki_ref, o_ref, s_ref, k_s, v_s = rest[3 * pps:]
    P = LANES
    p = pl.program_id(1)
    n_steps = pl.num_programs(1)
    n_rows = IDX_HEADS * n_q
    qi = qi_ref[0]
    w = _bf16_round(w_ref[0] * (IDX_HEADS ** -0.5))

    def chunk_scores(ki_t):
        t = _bf16_round(jnp.maximum(_dot(qi, ki_t), 0.0)) * w
        s = jnp.zeros((n_q, P), F32)
        for h in range(IDX_HEADS):
            s = s + t[h * n_q:(h + 1) * n_q, :]
        return s

    for j in range(pps):
        off = pl.multiple_of((p * pps + j) * P, P)
        s_ref[:, pl.ds(off, P)] = _sortable(chunk_scores(cki_refs[j][0, 0].astype(BF16)))
        k_s[:, pl.ds(off, P)] = ck_refs[j][0, 0].astype(BF16)
        v_s[:, pl.ds(off, P)] = cv_refs[j][0, 0].astype(BF16)

    @pl.when(p == n_steps - 1)
    def _():
        n_keys = s_ref.shape[1]
        past = n_keys - P
        n_chunks = n_keys // P
        key_j = lax.broadcasted_iota(I32, (n_q, P), 1)
        qry_t = lax.broadcasted_iota(I32, (n_q, P), 0)
        visible = key_j <= qry_t
        s_ref[:, pl.ds(past, P)] = jnp.where(visible, _sortable(chunk_scores(nki_ref[0].astype(BF16))), INT_MIN)
        k_s[:, pl.ds(past, P)] = nk_ref[0].astype(BF16)
        v_s[:, pl.ds(past, P)] = nv_ref[0].astype(BF16)

        def count(pred):
            hit = jnp.where(pred(s_ref[...]), 1.0, 0.0)
            return jnp.sum(_tree_sum([hit[:, c * P:(c + 1) * P] for c in range(n_chunks)]), axis=-1, keepdims=True)

        keep = float(n_keep)
        thr, cnt_thr = _threshold_search(lambda cand: count(lambda blk: blk >= cand), keep, (n_q, 1), float(n_keys))
        has_ties = jnp.max(jnp.where((thr != INT_MIN) & (cnt_thr > keep), 1.0, 0.0)) > 0.5

        @pl.when(jnp.logical_not(has_ties))
        def _():
            thr_eff = jnp.maximum(thr, INT_MIN + 1)
            s_ref[...] = jnp.where(s_ref[...] >= thr_eff, 1, 0).astype(I32)

        @pl.when(has_ties)
        def _():
            n_ties = keep - count(lambda blk: blk > thr)
            tri = (lax.broadcasted_iota(I32, (P, P), 0) <= lax.broadcasted_iota(I32, (P, P), 1)).astype(BF16)

            def sel_body(c, run):
                o = pl.multiple_of(c * P, P)
                blk = s_ref[:, pl.ds(o, P)]
                tie = (blk == thr) & (blk != INT_MIN)
                prefix = _dot(jnp.where(tie, 1.0, 0.0).astype(BF16), tri)
                sel = (blk > thr) | (tie & (run + prefix <= n_ties))
                s_ref[:, pl.ds(o, P)] = jnp.where(sel, 1, 0).astype(I32)
                return run + prefix[:, P - 1:P]

            lax.fori_loop(0, n_chunks, sel_body, jnp.zeros((n_q, 1), F32))

        sel = s_ref[...] > 0
        q = q_ref[0]
        lg = _dot(q, k_s[...])
        L = lg.shape[1]
        lg = jnp.where(sel[None], lg.reshape(B_HEADS, n_q, L), NEG_BIG)
        m = jnp.max(lg, axis=-1, keepdims=True)
        pr = jnp.exp(lg - m)
        den = jnp.sum(pr, axis=-1, keepdims=True)
        o = _dot_nt(pr.reshape(n_rows, L).astype(BF16), v_s[...])
        o_ref[0] = o / den.reshape(n_rows, 1)


def _dsa_decode(b_q, b_iq, iw_rows, b_k, b_v, b_ik, cache_kt, cache_vt, cache_kit, page_table, *, layer, Bd, Tq):
    page = cache_kt.shape[-1]
    assert page == LANES
    n_pages = page_table.shape[1]
    past = n_pages * page
    n_keep = min(MAX_KEEP, (past + Tq) // 4)
    n_rows = B_HEADS * Tq
    kvw = B_KV_HEADS * B_HEAD_DIM

    def rows_hq(a, heads, dim):
        return jnp.transpose(a.reshape(Bd, Tq, heads, dim), (0, 2, 1, 3)).reshape(Bd, heads * Tq, dim)

    qi = rows_hq(b_iq, IDX_HEADS, IDX_DIM).astype(BF16) * _exact_pow2(IDX_DIM ** -0.5)
    w = jnp.transpose(iw_rows.reshape(IDX_HEADS, Bd, Tq), (1, 0, 2)).reshape(Bd, IDX_HEADS * Tq, 1)
    qh = rows_hq(b_q, B_HEADS, B_HEAD_DIM).astype(BF16) * _exact_pow2(B_HEAD_DIM ** -0.5)
    group = (jnp.arange(n_rows) // Tq) // B_GROUP
    lane_group = jnp.arange(kvw) // B_HEAD_DIM
    q_wide = jnp.where(group[:, None] == lane_group[None, :], jnp.tile(qh, (1, 1, B_KV_HEADS)), 0.0).astype(BF16)
    pps = DECODE_PAGES_PER_STEP if n_pages % DECODE_PAGES_PER_STEP == 0 else 1

    def pad_new(a, width):
        return jnp.pad(jnp.transpose(a.reshape(Bd, Tq, width), (0, 2, 1)), ((0, 0), (0, 0), (0, page - Tq)))

    nk, nv, nki = pad_new(b_k, kvw), pad_new(b_v, kvw), pad_new(b_ik, IDX_DIM)
    n_keys = past + page
    seq = lambda b, p, pt: (b, 0, 0)
    pages = lambda width: [pl.BlockSpec((1, 1, width, page), lambda b, p, pt, j=j: (layer, pt[b, p * pps + j], 0, 0))
                           for j in range(pps)]
    out = pl.pallas_call(
        functools.partial(_dsa_decode_kernel, n_keep=n_keep, n_q=Tq, pps=pps),
        grid_spec=pltpu.PrefetchScalarGridSpec(
            num_scalar_prefetch=1,
            grid=(Bd, n_pages // pps),
            in_specs=[pl.BlockSpec((1, n_rows, IDX_DIM), seq),
                      pl.BlockSpec((1, n_rows, 1), seq),
                      pl.BlockSpec((1, n_rows, kvw), seq)]
            + pages(kvw) + pages(kvw) + pages(IDX_DIM)
            + [pl.BlockSpec((1, kvw, page), seq),
               pl.BlockSpec((1, kvw, page), seq),
               pl.BlockSpec((1, IDX_DIM, page), seq)],
            out_specs=pl.BlockSpec((1, n_rows, kvw), seq),
            scratch_shapes=[pltpu.VMEM((Tq, n_keys), I32),
                            pltpu.VMEM((kvw, n_keys), BF16),
                            pltpu.VMEM((kvw, n_keys), BF16)]),
        out_shape=jax.ShapeDtypeStruct((Bd, n_rows, kvw), F32),
        compiler_params=_cparams(("parallel", "arbitrary")),
    )(page_table, qi, w, q_wide, *([cache_kt] * pps), *([cache_vt] * pps), *([cache_kit] * pps), nk, nv, nki)
    out = out.reshape(Bd, B_HEADS, Tq, B_KV_HEADS, B_HEAD_DIM)
    out = jnp.concatenate([out[:, g * B_GROUP:(g + 1) * B_GROUP, :, g] for g in range(B_KV_HEADS)], axis=1)
    return jnp.transpose(out, (0, 2, 1, 3)).reshape(Bd * Tq, B_WIDTH)


def _pool_kernel(u_ref, prev_ref, halo0_ref, cmap_ref, cscale_ref, o_ref, ext, *, Tb, pos0):
    j = pl.program_id(1)
    H = POOL_HALO
    ext[0:H, :] = jnp.where(j == 0, halo0_ref[0], prev_ref[0])
    ext[H:H + Tb, :] = u_ref[0]
    pos = pos0 + j * Tb + lax.broadcasted_iota(I32, (Tb, 1), 0)
    for g, wdw in enumerate(C_WINDOWS):
        ls = slice(g * C_GROUP_WIDTH, (g + 1) * C_GROUP_WIDTH)
        win = ext[H:H + Tb, ls]
        for d in range(1, wdw):
            win = win + ext[H - d:H - d + Tb, ls]
        cnt = jnp.minimum(wdw, pos + 1).astype(F32)
        pooled = win / cnt - ext[H:H + Tb, ls]
        z = _dot(pooled.astype(BF16), cmap_ref[g])
        o_ref[0, :, ls] = z * cscale_ref[:, ls]


def _pool(u, prev_src, halo0, c_map, c_scale, *, Tb, pos0):
    Bn, R, C = u.shape
    H = POOL_HALO
    assert R % Tb == 0 and (Tb % H == 0 or R == Tb)
    prev_idx = (lambda b, j: (b, jnp.maximum(j * (Tb // H) - 1, 0), 0)) if R > Tb else (lambda b, j: (b, 0, 0))
    return pl.pallas_call(
        functools.partial(_pool_kernel, Tb=Tb, pos0=pos0),
        grid=(Bn, R // Tb),
        in_specs=[pl.BlockSpec((1, Tb, C), lambda b, j: (b, j, 0)),
                  pl.BlockSpec((1, H, C), prev_idx),
                  pl.BlockSpec((1, H, C), lambda b, j: (b, 0, 0)),
                  pl.BlockSpec(c_map.shape, lambda b, j: (0, 0, 0)),
                  pl.BlockSpec((1, C), lambda b, j: (0, 0))],
        out_specs=pl.BlockSpec((1, Tb, C), lambda b, j: (b, j, 0)),
        out_shape=jax.ShapeDtypeStruct((Bn, R, C), F32),
        scratch_shapes=[pltpu.VMEM((H + Tb, C), F32)],
        compiler_params=_cparams(("parallel", "parallel")),
    )(u, prev_src, halo0, c_map.astype(BF16), c_scale.reshape(1, C))


def _merge_kernel(x_ref, ha_ref, hb_ref, hc_ref, ga_ref, gb_ref, gc_ref, pa_ref, pb_ref, pc_ref, wo_ref,
                  g1_ref, b1_ref, o_ref, *, alpha):
    merged = (_sigmoid(ga_ref[...]) * _dot(ha_ref[...].astype(BF16), pa_ref[...])
              + _sigmoid(gb_ref[...]) * _dot(hb_ref[...].astype(BF16), pb_ref[...])
              + _sigmoid(gc_ref[...]) * _dot(hc_ref[...].astype(BF16), pc_ref[...]))
    y = alpha * x_ref[...] + _dot(merged.astype(BF16), wo_ref[...])
    o_ref[...] = _layer_norm(y, g1_ref[...], b1_ref[...])


def _merge(x, h_a, h_b, h_c, g_a, g_b, g_c, p_a, p_b, p_c, w_out, ln_g, ln_b, *, alpha):
    n, d = x.shape
    tm = MERGE_TM
    assert n % tm == 0
    tok = lambda w: pl.BlockSpec((tm, w), lambda i: (i, 0))
    full = lambda a: pl.BlockSpec(a.shape, lambda i: (0, 0))
    ws = [p_a.astype(BF16), p_b.astype(BF16), p_c.astype(BF16), w_out.astype(BF16), ln_g.reshape(1, d), ln_b.reshape(1, d)]
    return pl.pallas_call(
        functools.partial(_merge_kernel, alpha=alpha),
        grid=(n // tm,),
        in_specs=[tok(d), tok(h_a.shape[1]), tok(h_b.shape[1]), tok(h_c.shape[1]), tok(d), tok(d), tok(d)]
        + [full(a) for a in ws],
        out_specs=tok(d),
        out_shape=jax.ShapeDtypeStruct((n, d), F32),
        compiler_params=_cparams(("parallel",)),
    )(x, h_a, h_b, h_c, g_a, g_b, g_c, *ws)


def _router_kernel(x_ref, w_ref, rb_ref, o_ref):
    tm = x_ref.shape[0]
    E = LANES
    per_group = N_EXPERTS // N_EXPERT_GROUPS
    scores = _sigmoid(_dot(x_ref[...].astype(BF16), w_ref[...]))
    biased = scores + rb_ref[...]
    lane_i = lax.broadcasted_iota(I32, (tm, E), 1)
    lane = lane_i.astype(F32)
    grp = lane_i // per_group
    neg = -jnp.inf

    def first_max(a):
        m = jnp.max(a, axis=-1, keepdims=True)
        idx = jnp.min(jnp.where(a == m, lane, float(E)), axis=-1, keepdims=True)
        return m, idx

    gscore = []
    for g in range(N_EXPERT_GROUPS):
        bg = jnp.where(grp == g, biased, neg)
        m1, i1 = first_max(bg)
        m2, _ = first_max(jnp.where(lane == i1, neg, bg))
        gscore.append(m1 + m2)
    expert_ok = jnp.zeros((tm, E), jnp.bool_)
    for g in range(N_EXPERT_GROUPS):
        beaten = jnp.zeros((tm, 1), F32)
        for o in range(N_EXPERT_GROUPS):
            if o == g:
                continue
            wins = (gscore[o] > gscore[g]) | ((gscore[o] == gscore[g]) & (o < g))
            beaten = beaten + jnp.where(wins, 1.0, 0.0)
        expert_ok = expert_ok | ((grp == g) & (beaten < float(TOPK_GROUPS)))
    rem = jnp.where(expert_ok, biased, neg)
    chosen = jnp.zeros((tm, E), jnp.bool_)
    for _ in range(TOP_K):
        _, idx = first_max(rem)
        hit = lane == idx
        chosen = chosen | hit
        rem = jnp.where(hit, neg, rem)
    s_sel = jnp.where(chosen, scores, 0.0)
    gates = ROUTED_SCALE * s_sel / jnp.sum(s_sel, axis=-1, keepdims=True)
    o_ref[...] = gates


def _router(x, router_w, router_b):
    n, d = x.shape
    tm = ROUTER_TM
    assert n % tm == 0
    pad = LANES - N_EXPERTS
    w = jnp.pad(router_w, ((0, 0), (0, pad))).astype(BF16)
    full = lambda a: pl.BlockSpec(a.shape, lambda i: (0, 0))
    rb = jnp.pad(router_b, (0, pad), constant_values=-jnp.inf).reshape(1, LANES)
    return pl.pallas_call(
        _router_kernel, grid=(n // tm,),
        in_specs=[pl.BlockSpec((tm, d), lambda i: (i, 0)), full(w), full(rb)],
        out_specs=pl.BlockSpec((tm, LANES), lambda i: (i, 0)),
        out_shape=jax.ShapeDtypeStruct((n, LANES), F32),
        compiler_params=_cparams(("parallel",)),
    )(x, w, rb)


def _moe_kernel(x_ref, g_ref, wgu_ref, wd_ref, sgu_ref, sd_ref, g2_ref, b2_ref, o_ref, xb_s, *, alpha):
    e = pl.program_id(1)
    n_e = pl.num_programs(1)

    def swiglu(w_gu, w_down):
        gu = _dot(xb_s[...], w_gu)
        gg = gu[:, :EXPERT_DIM]
        act = gg * _sigmoid(gg) * gu[:, EXPERT_DIM:]
        return _dot(act.astype(BF16), w_down)

    @pl.when(e == 0)
    def _():
        xb_s[...] = x_ref[...].astype(BF16)
        o_ref[...] = swiglu(sgu_ref[...], sd_ref[...])

    gates = g_ref[...]
    lane = lax.broadcasted_iota(I32, gates.shape, 1)
    gate = jnp.sum(jnp.where(lane == e, gates, 0.0), axis=-1, keepdims=True)
    y = swiglu(wgu_ref[0, 0], wd_ref[0, 0])
    o_ref[...] += jnp.where(gate != 0.0, y * gate, 0.0)

    @pl.when(e == n_e - 1)
    def _():
        o_ref[...] = _layer_norm(alpha * x_ref[...] + o_ref[...], g2_ref[...], b2_ref[...])


def _moe(x, gates, wgu, wd, sgu, sd, ln_g, ln_b, *, layer, alpha, tm):
    n, d = x.shape
    n_e = wgu.shape[1]
    assert n % tm == 0
    const = lambda i, e: (0, 0)
    return pl.pallas_call(
        functools.partial(_moe_kernel, alpha=alpha),
        grid=(n // tm, n_e),
        in_specs=[pl.BlockSpec((tm, d), lambda i, e: (i, 0)),
                  pl.BlockSpec((tm, LANES), lambda i, e: (i, 0)),
                  pl.BlockSpec((1, 1) + wgu.shape[2:], lambda i, e: (layer, e, 0, 0)),
                  pl.BlockSpec((1, 1) + wd.shape[2:], lambda i, e: (layer, e, 0, 0)),
                  pl.BlockSpec(sgu.shape, const),
                  pl.BlockSpec(sd.shape, const),
                  pl.BlockSpec((1, d), const),
                  pl.BlockSpec((1, d), const)],
        out_specs=pl.BlockSpec((tm, d), lambda i, e: (i, 0)),
        out_shape=jax.ShapeDtypeStruct((n, d), F32),
        scratch_shapes=[pltpu.VMEM((tm, d), BF16)],
        compiler_params=_cparams(("parallel", "arbitrary")),
    )(x, gates, wgu, wd, sgu, sd, ln_g.reshape(1, d), ln_b.reshape(1, d))


def _moe_tile(n):
    for cand in (1280, 1024, 512, 256, 128):
        if n % cand == 0:
            return cand
    return n


def kernel(x_prompt, x_sample, cache_k, cache_v, cache_kidx, state_C, state_n, state_m, state_pool, page_table, w_in, b_in, b_fgate, a_norm_g, c_map, c_scale, p_a, p_b, p_c, w_out, ln1_g, ln1_b, router_w, router_b, exp_gu, exp_down, sh_gu, sh_down, ln2_g, ln2_b):
    B, T, D = x_prompt.shape
    Bd, Td, _ = x_sample.shape
    depth = w_in.shape[0]
    alpha = (2 * depth) ** 0.25
    n_p, n_d = B * T, Bd * Td
    n = n_p + n_d
    L = MLSTM_CHUNK
    nc = T // L
    past_len = page_table.shape[1] * cache_k.shape[2]
    H, dh = A_HEADS, A_HEAD_DIM

    n_pool, page = cache_k.shape[1:3]
    cache_kt = jnp.transpose(cache_k, (0, 1, 3, 4, 2)).reshape(depth, n_pool, B_KV_WIDTH, page)
    cache_vt = jnp.transpose(cache_v, (0, 1, 3, 4, 2)).reshape(depth, n_pool, B_KV_WIDTH, page)
    cache_kit = jnp.transpose(cache_kidx, (0, 1, 3, 2))
    exp_gu_b = exp_gu.astype(BF16)
    exp_down_b = exp_down.astype(BF16)

    x = jnp.concatenate([x_prompt.reshape(n_p, D), x_sample.reshape(n_d, D)], axis=0)
    new_p, new_s = [], []
    for l in range(depth):
        pr = _project(x, w_in[l], b_in[l])
        gate_rows = pr['gate_rows']
        gr_p = jnp.transpose(gate_rows[:2 * H, :n_p].reshape(2 * H, B * nc, L), (1, 0, 2))
        gr_d = jnp.transpose(gate_rows[:2 * H, n_p:].reshape(2 * H, Bd, Td), (1, 0, 2))
        iw_p = gate_rows[2 * H:, :n_p]
        iw_d = gate_rows[2 * H:, n_p:]

        zc = jnp.zeros((B, H, dh, dh), F32)
        zn = jnp.zeros((B, H, dh), F32)
        zm = jnp.full((B, H), -jnp.inf, F32)
        ml = functools.partial(_mlstm, pr['a_q'], pr['a_k'], pr['a_v'], pr['a_o'])
        ha_p, c_p, nn_p, m_p = ml(gr_p, pr['gate_cols'], b_fgate[l], a_norm_g[l], zc, zn, zm,
                                  row0=0, n_seq=B, n_chunks=nc, L=L)
        ha_d, c_d, nn_d, m_d = ml(gr_d, pr['gate_cols'], b_fgate[l], a_norm_g[l], state_C[l], state_n[l], state_m[l],
                                  row0=n_p, n_seq=Bd, n_chunks=1, L=Td)

        bk, bv, bik = pr['b_k'], pr['b_v'], pr['b_ik']
        hb_p = _dsa_prompt(pr['b_q'][:n_p], pr['b_iq'][:n_p], iw_p, bk[:n_p], bv[:n_p], bik[:n_p], B=B, T=T)
        hb_d = _dsa_decode(pr['b_q'][n_p:], pr['b_iq'][n_p:], iw_d, bk[n_p:], bv[n_p:], bik[n_p:],
                           cache_kt, cache_vt, cache_kit, page_table, layer=l, Bd=Bd, Tq=Td)

        cu = pr['c_u']
        cu_p = cu[:n_p].reshape(B, T, C_WIDTH)
        cu_d = cu[n_p:].reshape(Bd, Td, C_WIDTH)
        zero_halo = jnp.zeros((B, POOL_HALO, C_WIDTH), F32)
        hc_p = _pool(cu_p, cu_p, zero_halo, c_map[l], c_scale[l], Tb=min(T, 512), pos0=0)
        halo_d = jnp.pad(state_pool[l], ((0, 0), (POOL_HALO - POOL_BUF, 0), (0, 0)))
        hc_d = _pool(cu_d, halo_d, halo_d, c_map[l], c_scale[l], Tb=Td, pos0=past_len)

        h_a = jnp.concatenate([ha_p, ha_d], axis=0)
        h_b = jnp.concatenate([hb_p, hb_d], axis=0)
        h_c = jnp.concatenate([hc_p.reshape(n_p, C_WIDTH), hc_d.reshape(n_d, C_WIDTH)], axis=0)
        x1 = _merge(x, h_a, h_b, h_c, pr['g_a'], pr['g_b'], pr['g_c'], p_a[l], p_b[l], p_c[l], w_out[l],
                    ln1_g[l], ln1_b[l], alpha=alpha)
        gates = _router(x1, router_w[l], router_b[l])
        x = _moe(x1, gates, exp_gu_b, exp_down_b, sh_gu[l].astype(BF16), sh_down[l].astype(BF16),
                 ln2_g[l], ln2_b[l], layer=l, alpha=alpha, tm=_moe_tile(n))

        kvs = (B_KV_HEADS, B_HEAD_DIM)
        pool_d = jnp.concatenate([state_pool[l], cu_d], axis=1)[:, -POOL_BUF:]
        new_p.append((bk[:n_p].reshape(B, T, *kvs), bv[:n_p].reshape(B, T, *kvs), bik[:n_p].reshape(B, T, IDX_DIM),
                      c_p, nn_p, m_p, cu_p[:, T - POOL_BUF:]))
        new_s.append((bk[n_p:].reshape(Bd, Td, *kvs), bv[n_p:].reshape(Bd, Td, *kvs),
                      bik[n_p:].reshape(Bd, Td, IDX_DIM), c_d, nn_d, m_d, pool_d))

    def stacked(states, i):
        return jnp.stack([s[i] for s in states])

    outs = [x[:n_p].reshape(B, T, D), x[n_p:].reshape(Bd, Td, D)]
    for i in range(7):
        outs += [stacked(new_p, i), stacked(new_s, i)]
    return tuple(outs)
```

```python
import functools
import math

import jax
import jax.numpy as jnp
from jax import lax
from jax.experimental import pallas as pl
from jax.experimental.pallas import tpu as pltpu

F32 = jnp.float32
BF16 = jnp.bfloat16
I32 = jnp.int32

A_HEADS = 4
A_HEAD_DIM = 128
A_WIDTH = A_HEADS * A_HEAD_DIM
MLSTM_CHUNK = 128
B_HEADS = 8
B_KV_HEADS = 2
B_HEAD_DIM = 64
B_WIDTH = B_HEADS * B_HEAD_DIM
B_KV_WIDTH = B_KV_HEADS * B_HEAD_DIM
B_GROUP = B_HEADS // B_KV_HEADS
IDX_HEADS = 8
IDX_DIM = 64
MAX_KEEP = 256
QUERY_BLOCK = 128
C_WINDOWS = (2, 4, 8, 16)
C_GROUPS = len(C_WINDOWS)
C_GROUP_WIDTH = 128
C_WIDTH = C_GROUPS * C_GROUP_WIDTH
POOL_BUF = max(C_WINDOWS) - 1
N_EXPERTS = 64
TOP_K = 8
N_EXPERT_GROUPS = 8
TOPK_GROUPS = 4
EXPERT_DIM = 256
ROUTED_SCALE = 2.5
LN_EPS = 1e-5

LANES = 128
SUBLANES = 8
VMEM_LIMIT = 56 * 1024 * 1024

INT_MIN = -2 ** 31
NEG_BIG = -1e30

DSA_UNROLL = 4
DECODE_PAGES_PER_STEP = 8
PROJ_TM = 256
MERGE_TM = 256
ROUTER_TM = 256
POOL_HALO = 16


def _cparams(sem):
    return pltpu.CompilerParams(dimension_semantics=sem, vmem_limit_bytes=VMEM_LIMIT)


def _dot(a, b):
    return jnp.dot(a, b, preferred_element_type=F32)


def _dot_nt(a, b):
    return lax.dot_general(a, b, (((1,), (1,)), ((), ())), preferred_element_type=F32)


def _dot_tn(a, b):
    return lax.dot_general(a, b, (((0,), (0,)), ((), ())), preferred_element_type=F32)


def _sigmoid(x):
    return 1.0 / (1.0 + jnp.exp(-x))


def _log_sigmoid(x):
    return jnp.minimum(x, 0.0) - jnp.log1p(jnp.exp(-jnp.abs(x)))


def _layer_norm(x, g, b):
    mu = jnp.mean(x, axis=-1, keepdims=True)
    xc = x - mu
    var = jnp.mean(xc * xc, axis=-1, keepdims=True)
    return xc * lax.rsqrt(var + LN_EPS) * g + b


def _bf16_round(x):
    return x.astype(BF16).astype(F32)


def _exact_pow2(scale):
    m = float(scale)
    assert math.frexp(m)[0] == 0.5, m
    return jnp.asarray(m, BF16)


def _sortable(s):
    bits = lax.bitcast_convert_type(s, I32)
    return jnp.where(bits < 0, bits ^ jnp.int32(0x7FFFFFFF), bits)


_PROJ_PLAIN = ('a_q', 'a_k', 'a_v', 'a_o', 'b_q', 'b_k', 'b_v', 'b_iq', 'b_ik', 'c_u', 'g_a', 'g_b', 'g_c')


def _proj_kernel(*refs):
    n_plain = len(_PROJ_PLAIN)
    x_ref = refs[0]
    w_refs = refs[1:1 + n_plain]
    b_refs = refs[1 + n_plain:1 + 2 * n_plain]
    wt, bt, wc, bc = refs[1 + 2 * n_plain:5 + 2 * n_plain]
    outs = refs[5 + 2 * n_plain:]
    xb = x_ref[...].astype(BF16)
    for w_ref, b_ref, o_ref in zip(w_refs, b_refs, outs[:n_plain]):
        o_ref[...] = _dot(xb, w_ref[...]) + b_ref[...]
    outs[n_plain][...] = _dot_nt(wt[...], xb) + bt[...]
    outs[n_plain + 1][...] = _dot(xb, wc[...]) + bc[...]


def _project(x, w_in, b_in):
    n, d = x.shape
    widths = (A_WIDTH, A_WIDTH, A_WIDTH, A_HEADS, A_HEADS, A_WIDTH,
              B_WIDTH, B_KV_WIDTH, B_KV_WIDTH, IDX_HEADS * IDX_DIM, IDX_DIM, IDX_HEADS,
              C_WIDTH, d, d, d)
    names = ('a_q', 'a_k', 'a_v', 'a_i', 'a_f', 'a_o', 'b_q', 'b_k', 'b_v', 'b_iq', 'b_ik', 'b_iw',
             'c_u', 'g_a', 'g_b', 'g_c')
    off, acc = {}, 0
    for nm, w in zip(names, widths):
        off[nm] = (acc, w)
        acc += w

    def cols(nm):
        s, w = off[nm]
        return w_in[:, s:s + w], b_in[s:s + w]

    ws, bs = [], []
    for nm in _PROJ_PLAIN:
        w, b = cols(nm)
        ws.append(w.astype(BF16))
        bs.append(b.reshape(1, -1))
    wi, bi = cols('a_i')
    wf, bf = cols('a_f')
    ww, bw = cols('b_iw')
    n_rows = 2 * A_HEADS + IDX_HEADS
    w_t = jnp.concatenate([wi, wf, ww], axis=1).T.astype(BF16)
    b_t = jnp.concatenate([bi, bf, bw]).reshape(n_rows, 1)
    w_c = jnp.pad(jnp.concatenate([wi, wf], axis=1), ((0, 0), (0, LANES - 2 * A_HEADS))).astype(BF16)
    b_c = jnp.pad(jnp.concatenate([bi, bf]), (0, LANES - 2 * A_HEADS)).reshape(1, LANES)

    tm = PROJ_TM
    assert n % tm == 0
    const = lambda i: (0, 0)
    in_specs = [pl.BlockSpec((tm, d), lambda i: (i, 0))]
    in_specs += [pl.BlockSpec(w.shape, const) for w in ws]
    in_specs += [pl.BlockSpec(b.shape, const) for b in bs]
    in_specs += [pl.BlockSpec(a.shape, const) for a in (w_t, b_t, w_c, b_c)]
    out_shape = [jax.ShapeDtypeStruct((n, w.shape[1]), F32) for w in ws]
    out_specs = [pl.BlockSpec((tm, w.shape[1]), lambda i: (i, 0)) for w in ws]
    out_shape += [jax.ShapeDtypeStruct((n_rows, n), F32), jax.ShapeDtypeStruct((n, LANES), F32)]
    out_specs += [pl.BlockSpec((n_rows, tm), lambda i: (0, i)), pl.BlockSpec((tm, LANES), lambda i: (i, 0))]
    res = pl.pallas_call(
        _proj_kernel, grid=(n // tm,), in_specs=in_specs, out_specs=out_specs, out_shape=out_shape,
        compiler_params=_cparams(("parallel",)),
    )(x, *ws, *bs, w_t, b_t, w_c, b_c)
    out = dict(zip(_PROJ_PLAIN, res[:len(_PROJ_PLAIN)]))
    out['gate_rows'] = res[-2]
    out['gate_cols'] = res[-1]
    return out


def _mlstm_kernel(q_ref, k_ref, v_ref, gr_ref, gc_ref, ao_ref, br_ref, bc_ref, ng_ref, c0_ref, n0_ref, m0_ref,
                  h_ref, c_out, n_out, m_out, c_s, n_s, m_s, *, L):
    c = pl.program_id(1)
    nc = pl.num_programs(1)

    @pl.when(c == 0)
    def _():
        c_s[...] = c0_ref[0]
        n_s[...] = n0_ref[0]
        m_s[...] = m0_ref[0]

    row = lax.broadcasted_iota(I32, (L, L), 0)
    col = lax.broadcasted_iota(I32, (L, L), 1)
    tril = row >= col
    triu = row <= col
    gr = gr_ref[0]
    gc = gc_ref[...]
    lf_rows = _log_sigmoid(gr + br_ref[...])
    lf_cols = _log_sigmoid(gc + bc_ref[...])
    k_scale = A_HEAD_DIM ** -0.5
    for h in range(A_HEADS):
        hs = slice(h * A_HEAD_DIM, (h + 1) * A_HEAD_DIM)
        q = q_ref[:, hs]
        k = k_ref[:, hs] * k_scale
        v = v_ref[:, hs]
        li_r = gr[h:h + 1, :]
        lf_r = lf_rows[A_HEADS + h:A_HEADS + h + 1, :]
        li_c = gc[:, h:h + 1]
        lf_c = lf_cols[:, A_HEADS + h:A_HEADS + h + 1]
        b_c = jnp.sum(jnp.where(tril, lf_r, 0.0), axis=-1, keepdims=True)
        b_r = jnp.sum(jnp.where(triu, lf_c, 0.0), axis=0, keepdims=True)
        m_prev = m_s[h]
        c_prev = c_s[h]
        n_prev = n_s[h]
        d_log = jnp.where(tril, b_c - b_r + li_r, -jnp.inf)
        inter = b_c + m_prev
        m_t = jnp.maximum(inter, jnp.max(d_log, axis=-1, keepdims=True))
        w_inter = jnp.exp(inter - m_t)
        qb = q.astype(BF16)
        kb = k.astype(BF16)
        vb = v.astype(BF16)
        a = jnp.exp(d_log - m_t) * _dot_nt(qb, kb)
        num = w_inter * _dot(qb, c_prev.astype(BF16)) + _dot(a.astype(BF16), vb)
        qn = jnp.sum(qb.astype(F32) * _bf16_round(n_prev), axis=-1, keepdims=True)
        den = w_inter * qn + jnp.sum(a, axis=-1, keepdims=True)
        hh = num / jnp.maximum(jnp.abs(den), jnp.exp(-m_t))
        m_new = m_t[L - 1:L, :]
        b_last = b_c[L - 1:L, :]
        w_c = jnp.exp(b_last - b_c + li_c - m_new)
        decay = jnp.exp(b_last + m_prev - m_new)
        kw = k * w_c
        c_s[h] = decay * c_prev + _dot_tn(kw.astype(BF16), vb)
        n_s[h] = decay * n_prev + jnp.sum(kb.astype(F32) * _bf16_round(w_c), axis=0, keepdims=True)
        m_s[h] = m_new
        mu = jnp.mean(hh, axis=-1, keepdims=True)
        hc = hh - mu
        var = jnp.mean(hc * hc, axis=-1, keepdims=True)
        hn = hc * lax.rsqrt(var + LN_EPS)
        h_ref[:, hs] = _sigmoid(ao_ref[:, hs]) * hn * ng_ref[:, hs]

    @pl.when(c == nc - 1)
    def _():
        c_out[0] = c_s[...]
        n_out[0] = n_s[...]
        m_out[0] = m_s[...]


def _mlstm(a_q, a_k, a_v, a_o, gate_rows, gate_cols, b_fgate, a_norm_g, c0, n0, m0, *, row0, n_seq, n_chunks, L):
    assert row0 % L == 0
    rb0 = row0 // L
    n_rows = n_seq * n_chunks * L
    H, dh = A_HEADS, A_HEAD_DIM
    tok = lambda b, c: (rb0 + b * n_chunks + c, 0)
    st = lambda b, c: (b, 0, 0, 0)
    const = lambda b, c: (0, 0)
    bias_r = jnp.pad(b_fgate, (A_HEADS, 0)).reshape(2 * A_HEADS, 1)
    bias_c = jnp.pad(b_fgate, (A_HEADS, LANES - 2 * A_HEADS)).reshape(1, LANES)
    outs = pl.pallas_call(
        functools.partial(_mlstm_kernel, L=L),
        grid=(n_seq, n_chunks),
        in_specs=[pl.BlockSpec((L, A_WIDTH), tok)] * 3
        + [pl.BlockSpec((1, 2 * A_HEADS, L), lambda b, c: (b * n_chunks + c, 0, 0)),
           pl.BlockSpec((L, LANES), tok),
           pl.BlockSpec((L, A_WIDTH), tok),
           pl.BlockSpec((2 * A_HEADS, 1), const),
           pl.BlockSpec((1, LANES), const),
           pl.BlockSpec((1, A_WIDTH), const),
           pl.BlockSpec((1, H, dh, dh), st),
           pl.BlockSpec((1, H, 1, dh), st),
           pl.BlockSpec((1, H, 1, 1), st)],
        out_specs=[pl.BlockSpec((L, A_WIDTH), lambda b, c: (b * n_chunks + c, 0)),
                   pl.BlockSpec((1, H, dh, dh), st),
                   pl.BlockSpec((1, H, 1, dh), st),
                   pl.BlockSpec((1, H, 1, 1), st)],
        out_shape=[jax.ShapeDtypeStruct((n_rows, A_WIDTH), F32),
                   jax.ShapeDtypeStruct((n_seq, H, dh, dh), F32),
                   jax.ShapeDtypeStruct((n_seq, H, 1, dh), F32),
                   jax.ShapeDtypeStruct((n_seq, H, 1, 1), F32)],
        scratch_shapes=[pltpu.VMEM((H, dh, dh), F32), pltpu.VMEM((H, 1, dh), F32), pltpu.VMEM((H, 1, 1), F32)],
        compiler_params=_cparams(("parallel", "arbitrary")),
    )(a_q, a_k, a_v, gate_rows, gate_cols, a_o, bias_r, bias_c, a_norm_g.reshape(1, A_WIDTH),
      c0, n0.reshape(n_seq, H, 1, dh), m0.reshape(n_seq, H, 1, 1))
    h, c_new, n_new, m_new = outs
    return h, c_new, n_new.reshape(n_seq, H, dh), m_new.reshape(n_seq, H)


def _threshold_search(count_ge, n_keep, shape, n_total):
    def body(j, carry):
        t, cnt_t = carry
        cand = t ^ (jnp.int32(1) << (31 - j))
        cnt = count_ge(cand)
        ok = cnt >= n_keep
        return jnp.where(ok, cand, t), jnp.where(ok, cnt, cnt_t)
    return lax.fori_loop(0, 32, body, (jnp.full(shape, INT_MIN, I32), jnp.full(shape, n_total, F32)))


def _tree_sum(parts):
    parts = list(parts)
    while len(parts) > 1:
        parts = [parts[i] + parts[i + 1] for i in range(0, len(parts) - 1, 2)] + parts[len(parts) & ~1:]
    return parts[0]


def _dsa_prompt_kernel(qi_ref, w_ref, ki_ref, q_ref, k_ref, vt_ref, o_ref, s_ref, acc_ref, m_ref, l_ref,
                       bias_s, lg_s, p_s, *, n_keep):
    QB = QUERY_BLOCK
    U = DSA_UNROLL
    SK = U * QB
    PW = 2 * QB
    i = pl.program_id(1)
    n_trips = (i + U) // U
    key_minus_qry = lax.broadcasted_iota(I32, (QB, QB), 0) - lax.broadcasted_iota(I32, (QB, QB), 1)
    qi = qi_ref[0, 0]
    w = _bf16_round(w_ref[0, 0] * (IDX_HEADS ** -0.5))

    def score_body(t, carry):
        for u in range(U):
            c = t * U + u
            ki = ki_ref[0, pl.ds(pl.multiple_of(c * QB, QB), QB), :]
            s = jnp.zeros((QB, QB), F32)
            for hp in range(IDX_HEADS // 2):
                d = _dot(ki, qi[:, hp * PW:(hp + 1) * PW])
                r = jnp.maximum(d.astype(BF16), 0).astype(F32)
                for h in (2 * hp, 2 * hp + 1):
                    s = s + r[:, (h % 2) * QB:(h % 2 + 1) * QB] * w[h:h + 1, :]
            visible = key_minus_qry <= (i - c) * QB
            s_ref[c] = jnp.where(visible, _sortable(s), INT_MIN)
        return carry

    lax.fori_loop(0, n_trips, score_body, 0)

    def count(pred):
        def body(t, acc):
            parts = []
            for u in range(U):
                hit = jnp.where(pred(s_ref[t * U + u]), 1.0, 0.0)
                parts += [hit[r * SUBLANES:(r + 1) * SUBLANES, :] for r in range(QB // SUBLANES)]
            return acc + _tree_sum(parts)
        acc = lax.fori_loop(0, n_trips, body, jnp.zeros((SUBLANES, QB), F32))
        return jnp.sum(acc, axis=0, keepdims=True)

    keep = float(n_keep)
    n_total = (n_trips * SK).astype(F32)
    thr, cnt_thr = _threshold_search(lambda cand: count(lambda blk: blk >= cand), keep, (1, QB), n_total)
    has_ties = jnp.max(jnp.where((thr != INT_MIN) & (cnt_thr > keep), 1.0, 0.0)) > 0.5

    m_ref[...] = jnp.full(m_ref.shape, NEG_BIG, F32)
    l_ref[...] = jnp.zeros(l_ref.shape, F32)
    acc_ref[...] = jnp.zeros(acc_ref.shape, F32)
    q = q_ref[0, 0]
    pairs_per_group = B_GROUP // 2

    def fold(x, op):
        parts = [x[r * SUBLANES:(r + 1) * SUBLANES, :] for r in range(QB // SUBLANES)]
        while len(parts) > 1:
            parts = [op(parts[a], parts[a + 1]) for a in range(0, len(parts), 2)]
        return parts[0]

    n_pairs = B_HEADS // 2
    halves = [slice(e * QB, (e + 1) * QB) for e in range(2)]

    def logits_phase(t, slot):
        off = pl.multiple_of(t * SK, SK)
        mx = [[None, None] for _ in range(n_pairs)]
        for u in range(U):
            bias = bias_s[u]
            for g in range(B_KV_HEADS):
                kc = k_ref[0, g, pl.ds(off + u * QB, QB), :]
                for hp in range(g * pairs_per_group, (g + 1) * pairs_per_group):
                    lg = _dot(kc, q[:, hp * PW:(hp + 1) * PW])
                    for e, cs in enumerate(halves):
                        piece = lg[:, cs] + bias
                        lg_s[slot, hp, u, :, cs] = piece
                        pm = fold(piece, jnp.maximum)
                        mx[hp][e] = pm if mx[hp][e] is None else jnp.maximum(mx[hp][e], pm)
        return tuple(jnp.concatenate([jnp.max(mx[hp][e], axis=0, keepdims=True) for e in range(2)], axis=1)
                     for hp in range(n_pairs))

    def values_phase(t, slot, mx):
        off = pl.multiple_of(t * SK, SK)
        alpha = []
        for hp in range(n_pairs):
            m_old = m_ref[hp]
            m_new = jnp.maximum(m_old, mx[hp])
            alpha.append(jnp.exp(m_old - m_new))
            m_ref[hp] = m_new
            ls = [None, None]
            for u in range(U):
                for e, cs in enumerate(halves):
                    p = jnp.exp(lg_s[slot, hp, u, :, cs] - m_new[:, cs])
                    p_s[hp, u * QB:(u + 1) * QB, cs] = p.astype(BF16)
                    ps = fold(p, jnp.add)
                    ls[e] = ps if ls[e] is None else ls[e] + ps
            l_new = jnp.concatenate([jnp.sum(ls[e], axis=0, keepdims=True) for e in range(2)], axis=1)
            l_ref[hp] = alpha[hp] * l_ref[hp] + l_new
        for g in range(B_KV_HEADS):
            vt = vt_ref[0, g, :, pl.ds(off, SK)]
            for hp in range(g * pairs_per_group, (g + 1) * pairs_per_group):
                acc_ref[hp] = alpha[hp] * acc_ref[hp] + _dot(vt, p_s[hp])

    @pl.when(jnp.logical_not(has_ties))
    def _():
        thr_eff = jnp.maximum(thr, INT_MIN + 1)

        def select(t):
            for u in range(U):
                bias_s[u] = jnp.where(s_ref[t * U + u] >= thr_eff, 0.0, NEG_BIG)

        select(0)
        mx0 = logits_phase(0, 0)

        def body(t, mx):
            values_phase(t, 0, mx)
            nxt = jnp.minimum(t + 1, n_trips - 1)
            select(nxt)
            return logits_phase(nxt, 0)

        lax.fori_loop(0, n_trips, body, mx0)

    @pl.when(has_ties)
    def _():
        n_ties = keep - count(lambda blk: blk > thr)
        tri = (lax.broadcasted_iota(I32, (SK, SK), 0) >= lax.broadcasted_iota(I32, (SK, SK), 1)).astype(BF16)

        def body(t, run):
            blk = s_ref[pl.ds(t * U, U)].reshape(SK, QB)
            tie = (blk == thr) & (blk != INT_MIN)
            prefix = _dot(tri, jnp.where(tie, 1.0, 0.0).astype(BF16))
            sel = (blk > thr) | (tie & (run + prefix <= n_ties))
            bias_s[...] = jnp.where(sel, 0.0, NEG_BIG).reshape(U, QB, QB)
            values_phase(t, 0, logits_phase(t, 0))
            return run + prefix[SK - 1:SK, :]

        lax.fori_loop(0, n_trips, body, jnp.zeros((1, QB), F32))

    for hp in range(B_HEADS // 2):
        g, pr = divmod(hp, pairs_per_group)
        o_ref[0, 0, g, :, pr * PW:(pr + 1) * PW] = acc_ref[hp] / l_ref[hp]


def _dsa_prompt(b_q, b_iq, iw_rows, b_k, b_v, b_ik, *, B, T):
    QB = QUERY_BLOCK
    nb = T // QB
    n_keep = min(MAX_KEEP, T // 4)

    assert nb % DSA_UNROLL == 0

    def head_t(a, heads, dim):
        a = a.reshape(B, nb, QB, heads, dim).astype(BF16) * _exact_pow2(dim ** -0.5)
        return jnp.transpose(a, (0, 1, 4, 3, 2)).reshape(B, nb, dim, heads * QB)

    qi_t = head_t(b_iq, IDX_HEADS, IDX_DIM)
    q_t = head_t(b_q, B_HEADS, B_HEAD_DIM)
    w_t = jnp.transpose(iw_rows.reshape(IDX_HEADS, B, nb, QB), (1, 2, 0, 3))
    ki = b_ik.reshape(B, T, IDX_DIM).astype(BF16)
    kh = jnp.transpose(b_k.reshape(B, T, B_KV_HEADS, B_HEAD_DIM), (0, 2, 1, 3)).astype(BF16)
    vt = jnp.transpose(b_v.reshape(B, T, B_KV_HEADS, B_HEAD_DIM), (0, 2, 3, 1)).astype(BF16)
    GW = B_GROUP * QB
    n_pairs = B_HEADS // 2
    blk = lambda b, i: (b, i, 0, 0)
    out = pl.pallas_call(
        functools.partial(_dsa_prompt_kernel, n_keep=n_keep),
        grid=(B, nb),
        in_specs=[pl.BlockSpec((1, 1, IDX_DIM, IDX_HEADS * QB), blk),
                  pl.BlockSpec((1, 1, IDX_HEADS, QB), blk),
                  pl.BlockSpec((1, T, IDX_DIM), lambda b, i: (b, 0, 0)),
                  pl.BlockSpec((1, 1, B_HEAD_DIM, B_HEADS * QB), blk),
                  pl.BlockSpec((1, B_KV_HEADS, T, B_HEAD_DIM), lambda b, i: (b, 0, 0, 0)),
                  pl.BlockSpec((1, B_KV_HEADS, B_HEAD_DIM, T), lambda b, i: (b, 0, 0, 0))],
        out_specs=pl.BlockSpec((1, 1, B_KV_HEADS, B_HEAD_DIM, GW), lambda b, i: (b, i, 0, 0, 0)),
        out_shape=jax.ShapeDtypeStruct((B, nb, B_KV_HEADS, B_HEAD_DIM, GW), F32),
        scratch_shapes=[pltpu.VMEM((nb, QB, QB), I32),
                        pltpu.VMEM((n_pairs, B_HEAD_DIM, 2 * QB), F32),
                        pltpu.VMEM((n_pairs, 1, 2 * QB), F32),
                        pltpu.VMEM((n_pairs, 1, 2 * QB), F32),
                        pltpu.VMEM((DSA_UNROLL, QB, QB), F32),
                        pltpu.VMEM((1, n_pairs, DSA_UNROLL, QB, 2 * QB), F32),
                        pltpu.VMEM((n_pairs, DSA_UNROLL * QB, 2 * QB), BF16)],
        compiler_params=_cparams(("parallel", "arbitrary")),
    )(qi_t, w_t, ki, q_t, kh, vt)
    out = out.reshape(B, nb, B_KV_HEADS, B_HEAD_DIM, B_GROUP, QB)
    return jnp.transpose(out, (0, 1, 5, 2, 4, 3)).reshape(B * T, B_WIDTH)


def _dsa_decode_kernel(pt_ref, qi_ref, w_ref, q_ref, *rest, n_keep, n_q, pps):
    ck_refs, cv_refs, cki_refs = rest[0:pps], rest[pps:2 * pps], rest[2 * pps:3 * pps]
    nk_ref, nv_ref, nki_ref, o_ref, s_ref, k_s, v_s = rest[3 * pps:]
    P = LANES
    p = pl.program_id(1)
    n_steps = pl.num_programs(1)
    n_rows = IDX_HEADS * n_q
    qi = qi_ref[0]
    w = _bf16_round(w_ref[0] * (IDX_HEADS ** -0.5))

    def chunk_scores(ki_t):
        t = _bf16_round(jnp.maximum(_dot(qi, ki_t), 0.0)) * w
        s = jnp.zeros((n_q, P), F32)
        for h in range(IDX_HEADS):
            s = s + t[h * n_q:(h + 1) * n_q, :]
        return s

    for j in range(pps):
        off = pl.multiple_of((p * pps + j) * P, P)
        s_ref[:, pl.ds(off, P)] = _sortable(chunk_scores(cki_refs[j][0, 0].astype(BF16)))
        k_s[:, pl.ds(off, P)] = ck_refs[j][0, 0].astype(BF16)
        v_s[:, pl.ds(off, P)] = cv_refs[j][0, 0].astype(BF16)

    @pl.when(p == n_steps - 1)
    def _():
        n_keys = s_ref.shape[1]
        past = n_keys - P
        n_chunks = n_keys // P
        key_j = lax.broadcasted_iota(I32, (n_q, P), 1)
        qry_t = lax.broadcasted_iota(I32, (n_q, P), 0)
        visible = key_j <= qry_t
        s_ref[:, pl.ds(past, P)] = jnp.where(visible, _sortable(chunk_scores(nki_ref[0].astype(BF16))), INT_MIN)
        k_s[:, pl.ds(past, P)] = nk_ref[0].astype(BF16)
        v_s[:, pl.ds(past, P)] = nv_ref[0].astype(BF16)

        def count(pred):
            hit = jnp.where(pred(s_ref[...]), 1.0, 0.0)
            return jnp.sum(_tree_sum([hit[:, c * P:(c + 1) * P] for c in range(n_chunks)]), axis=-1, keepdims=True)

        keep = float(n_keep)
        thr, cnt_thr = _threshold_search(lambda cand: count(lambda blk: blk >= cand), keep, (n_q, 1), float(n_keys))
        has_ties = jnp.max(jnp.where((thr != INT_MIN) & (cnt_thr > keep), 1.0, 0.0)) > 0.5

        @pl.when(jnp.logical_not(has_ties))
        def _():
            thr_eff = jnp.maximum(thr, INT_MIN + 1)
            s_ref[...] = jnp.where(s_ref[...] >= thr_eff, 1, 0).astype(I32)

        @pl.when(has_ties)
        def _():
            n_ties = keep - count(lambda blk: blk > thr)
            tri = (lax.broadcasted_iota(I32, (P, P), 0) <= lax.broadcasted_iota(I32, (P, P), 1)).astype(BF16)

            def sel_body(c, run):
                o = pl.multiple_of(c * P, P)
                blk = s_ref[:, pl.ds(o, P)]
                tie = (blk == thr) & (blk != INT_MIN)
                prefix = _dot(jnp.where(tie, 1.0, 0.0).astype(BF16), tri)
                sel = (blk > thr) | (tie & (run + prefix <= n_ties))
                s_ref[:, pl.ds(o, P)] = jnp.where(sel, 1, 0).astype(I32)
                return run + prefix[:, P - 1:P]

            lax.fori_loop(0, n_chunks, sel_body, jnp.zeros((n_q, 1), F32))

        sel = s_ref[...] > 0
        q = q_ref[0]
        lg = _dot(q, k_s[...])
        L = lg.shape[1]
        lg = jnp.where(sel[None], lg.reshape(B_HEADS, n_q, L), NEG_BIG)
        m = jnp.max(lg, axis=-1, keepdims=True)
        pr = jnp.exp(lg - m)
        pr = pr / jnp.sum(pr, axis=-1, keepdims=True)
        o_ref[0] = _dot_nt(pr.reshape(n_rows, L).astype(BF16), v_s[...])


def _dsa_decode(b_q, b_iq, iw_rows, b_k, b_v, b_ik, cache_kt, cache_vt, cache_kit, page_table, *, layer, Bd, Tq):
    page = cache_kt.shape[-1]
    assert page == LANES
    n_pages = page_table.shape[1]
    past = n_pages * page
    n_keep = min(MAX_KEEP, (past + Tq) // 4)
    n_rows = B_HEADS * Tq
    kvw = B_KV_HEADS * B_HEAD_DIM

    def rows_hq(a, heads, dim):
        return jnp.transpose(a.reshape(Bd, Tq, heads, dim), (0, 2, 1, 3)).reshape(Bd, heads * Tq, dim)

    qi = rows_hq(b_iq, IDX_HEADS, IDX_DIM).astype(BF16) * _exact_pow2(IDX_DIM ** -0.5)
    w = jnp.transpose(iw_rows.reshape(IDX_HEADS, Bd, Tq), (1, 0, 2)).reshape(Bd, IDX_HEADS * Tq, 1)
    qh = rows_hq(b_q, B_HEADS, B_HEAD_DIM).astype(BF16) * _exact_pow2(B_HEAD_DIM ** -0.5)
    group = (jnp.arange(n_rows) // Tq) // B_GROUP
    lane_group = jnp.arange(kvw) // B_HEAD_DIM
    q_wide = jnp.where(group[:, None] == lane_group[None, :], jnp.tile(qh, (1, 1, B_KV_HEADS)), 0.0).astype(BF16)
    pps = DECODE_PAGES_PER_STEP if n_pages % DECODE_PAGES_PER_STEP == 0 else 1

    def pad_new(a, width):
        return jnp.pad(jnp.transpose(a.reshape(Bd, Tq, width), (0, 2, 1)), ((0, 0), (0, 0), (0, page - Tq)))

    nk, nv, nki = pad_new(b_k, kvw), pad_new(b_v, kvw), pad_new(b_ik, IDX_DIM)
    n_keys = past + page
    seq = lambda b, p, pt: (b, 0, 0)
    pages = lambda width: [pl.BlockSpec((1, 1, width, page), lambda b, p, pt, j=j: (layer, pt[b, p * pps + j], 0, 0))
                           for j in range(pps)]
    out = pl.pallas_call(
        functools.partial(_dsa_decode_kernel, n_keep=n_keep, n_q=Tq, pps=pps),
        grid_spec=pltpu.PrefetchScalarGridSpec(
            num_scalar_prefetch=1,
            grid=(Bd, n_pages // pps),
            in_specs=[pl.BlockSpec((1, n_rows, IDX_DIM), seq),
                      pl.BlockSpec((1, n_rows, 1), seq),
                      pl.BlockSpec((1, n_rows, kvw), seq)]
            + pages(kvw) + pages(kvw) + pages(IDX_DIM)
            + [pl.BlockSpec((1, kvw, page), seq),
               pl.BlockSpec((1, kvw, page), seq),
               pl.BlockSpec((1, IDX_DIM, page), seq)],
            out_specs=pl.BlockSpec((1, n_rows, kvw), seq),
            scratch_shapes=[pltpu.VMEM((Tq, n_keys), I32),
                            pltpu.VMEM((kvw, n_keys), BF16),
                            pltpu.VMEM((kvw, n_keys), BF16)]),
        out_shape=jax.ShapeDtypeStruct((Bd, n_rows, kvw), F32),
        compiler_params=_cparams(("parallel", "arbitrary")),
    )(page_table, qi, w, q_wide, *([cache_kt] * pps), *([cache_vt] * pps), *([cache_kit] * pps), nk, nv, nki)
    out = out.reshape(Bd, B_HEADS, Tq, B_KV_HEADS, B_HEAD_DIM)
    out = jnp.concatenate([out[:, g * B_GROUP:(g + 1) * B_GROUP, :, g] for g in range(B_KV_HEADS)], axis=1)
    return jnp.transpose(out, (0, 2, 1, 3)).reshape(Bd * Tq, B_WIDTH)


def _pool_kernel(u_ref, prev_ref, halo0_ref, cmap_ref, cscale_ref, o_ref, ext, *, Tb, pos0):
    j = pl.program_id(1)
    H = POOL_HALO
    ext[0:H, :] = jnp.where(j == 0, halo0_ref[0], prev_ref[0])
    ext[H:H + Tb, :] = u_ref[0]
    pos = pos0 + j * Tb + lax.broadcasted_iota(I32, (Tb, 1), 0)
    for g, wdw in enumerate(C_WINDOWS):
        ls = slice(g * C_GROUP_WIDTH, (g + 1) * C_GROUP_WIDTH)
        win = ext[H:H + Tb, ls]
        for d in range(1, wdw):
            win = win + ext[H - d:H - d + Tb, ls]
        cnt = jnp.minimum(wdw, pos + 1).astype(F32)
        pooled = win / cnt - ext[H:H + Tb, ls]
        z = _dot(pooled.astype(BF16), cmap_ref[g])
        o_ref[0, :, ls] = z * cscale_ref[:, ls]


def _pool(u, prev_src, halo0, c_map, c_scale, *, Tb, pos0):
    Bn, R, C = u.shape
    H = POOL_HALO
    assert R % Tb == 0 and (Tb % H == 0 or R == Tb)
    prev_idx = (lambda b, j: (b, jnp.maximum(j * (Tb // H) - 1, 0), 0)) if R > Tb else (lambda b, j: (b, 0, 0))
    return pl.pallas_call(
        functools.partial(_pool_kernel, Tb=Tb, pos0=pos0),
        grid=(Bn, R // Tb),
        in_specs=[pl.BlockSpec((1, Tb, C), lambda b, j: (b, j, 0)),
                  pl.BlockSpec((1, H, C), prev_idx),
                  pl.BlockSpec((1, H, C), lambda b, j: (b, 0, 0)),
                  pl.BlockSpec(c_map.shape, lambda b, j: (0, 0, 0)),
                  pl.BlockSpec((1, C), lambda b, j: (0, 0))],
        out_specs=pl.BlockSpec((1, Tb, C), lambda b, j: (b, j, 0)),
        out_shape=jax.ShapeDtypeStruct((Bn, R, C), F32),
        scratch_shapes=[pltpu.VMEM((H + Tb, C), F32)],
        compiler_params=_cparams(("parallel", "parallel")),
    )(u, prev_src, halo0, c_map.astype(BF16), c_scale.reshape(1, C))


def _merge_kernel(x_ref, ha_ref, hb_ref, hc_ref, ga_ref, gb_ref, gc_ref, pa_ref, pb_ref, pc_ref, wo_ref,
                  g1_ref, b1_ref, o_ref, *, alpha):
    merged = (_sigmoid(ga_ref[...]) * _dot(ha_ref[...].astype(BF16), pa_ref[...])
              + _sigmoid(gb_ref[...]) * _dot(hb_ref[...].astype(BF16), pb_ref[...])
              + _sigmoid(gc_ref[...]) * _dot(hc_ref[...].astype(BF16), pc_ref[...]))
    y = alpha * x_ref[...] + _dot(merged.astype(BF16), wo_ref[...])
    o_ref[...] = _layer_norm(y, g1_ref[...], b1_ref[...])


def _merge(x, h_a, h_b, h_c, g_a, g_b, g_c, p_a, p_b, p_c, w_out, ln_g, ln_b, *, alpha):
    n, d = x.shape
    tm = MERGE_TM
    assert n % tm == 0
    tok = lambda w: pl.BlockSpec((tm, w), lambda i: (i, 0))
    full = lambda a: pl.BlockSpec(a.shape, lambda i: (0, 0))
    ws = [p_a.astype(BF16), p_b.astype(BF16), p_c.astype(BF16), w_out.astype(BF16), ln_g.reshape(1, d), ln_b.reshape(1, d)]
    return pl.pallas_call(
        functools.partial(_merge_kernel, alpha=alpha),
        grid=(n // tm,),
        in_specs=[tok(d), tok(h_a.shape[1]), tok(h_b.shape[1]), tok(h_c.shape[1]), tok(d), tok(d), tok(d)]
        + [full(a) for a in ws],
        out_specs=tok(d),
        out_shape=jax.ShapeDtypeStruct((n, d), F32),
        compiler_params=_cparams(("parallel",)),
    )(x, h_a, h_b, h_c, g_a, g_b, g_c, *ws)


def _router_kernel(x_ref, w_ref, rb_ref, o_ref):
    tm = x_ref.shape[0]
    E = LANES
    per_group = N_EXPERTS // N_EXPERT_GROUPS
    scores = _sigmoid(_dot(x_ref[...].astype(BF16), w_ref[...]))
    biased = scores + rb_ref[...]
    lane_i = lax.broadcasted_iota(I32, (tm, E), 1)
    lane = lane_i.astype(F32)
    grp = lane_i // per_group
    neg = -jnp.inf

    def first_max(a):
        m = jnp.max(a, axis=-1, keepdims=True)
        idx = jnp.min(jnp.where(a == m, lane, float(E)), axis=-1, keepdims=True)
        return m, idx

    gscore = []
    for g in range(N_EXPERT_GROUPS):
        bg = jnp.where(grp == g, biased, neg)
        m1, i1 = first_max(bg)
        m2, _ = first_max(jnp.where(lane == i1, neg, bg))
        gscore.append(m1 + m2)
    expert_ok = jnp.zeros((tm, E), jnp.bool_)
    for g in range(N_EXPERT_GROUPS):
        beaten = jnp.zeros((tm, 1), F32)
        for o in range(N_EXPERT_GROUPS):
            if o == g:
                continue
            wins = (gscore[o] > gscore[g]) | ((gscore[o] == gscore[g]) & (o < g))
            beaten = beaten + jnp.where(wins, 1.0, 0.0)
        expert_ok = expert_ok | ((grp == g) & (beaten < float(TOPK_GROUPS)))
    rem = jnp.where(expert_ok, biased, neg)
    chosen = jnp.zeros((tm, E), jnp.bool_)
    for _ in range(TOP_K):
        _, idx = first_max(rem)
        hit = lane == idx
        chosen = chosen | hit
        rem = jnp.where(hit, neg, rem)
    s_sel = jnp.where(chosen, scores, 0.0)
    gates = ROUTED_SCALE * s_sel / jnp.sum(s_sel, axis=-1, keepdims=True)
    o_ref[...] = gates


def _router(x, router_w, router_b):
    n, d = x.shape
    tm = ROUTER_TM
    assert n % tm == 0
    pad = LANES - N_EXPERTS
    w = jnp.pad(router_w, ((0, 0), (0, pad))).astype(BF16)
    full = lambda a: pl.BlockSpec(a.shape, lambda i: (0, 0))
    rb = jnp.pad(router_b, (0, pad), constant_values=-jnp.inf).reshape(1, LANES)
    return pl.pallas_call(
        _router_kernel, grid=(n // tm,),
        in_specs=[pl.BlockSpec((tm, d), lambda i: (i, 0)), full(w), full(rb)],
        out_specs=pl.BlockSpec((tm, LANES), lambda i: (i, 0)),
        out_shape=jax.ShapeDtypeStruct((n, LANES), F32),
        compiler_params=_cparams(("parallel",)),
    )(x, w, rb)


def _moe_kernel(x_ref, g_ref, wgu_ref, wd_ref, sgu_ref, sd_ref, g2_ref, b2_ref, o_ref, xb_s, *, alpha):
    e = pl.program_id(1)
    n_e = pl.num_programs(1)

    def swiglu(w_gu, w_down):
        gu = _dot(xb_s[...], w_gu)
        gg = gu[:, :EXPERT_DIM]
        act = gg * _sigmoid(gg) * gu[:, EXPERT_DIM:]
        return _dot(act.astype(BF16), w_down)

    @pl.when(e == 0)
    def _():
        xb_s[...] = x_ref[...].astype(BF16)
        o_ref[...] = swiglu(sgu_ref[...], sd_ref[...])

    gates = g_ref[...]
    lane = lax.broadcasted_iota(I32, gates.shape, 1)
    gate = jnp.sum(jnp.where(lane == e, gates, 0.0), axis=-1, keepdims=True)
    y = swiglu(wgu_ref[0, 0], wd_ref[0, 0])
    o_ref[...] += jnp.where(gate != 0.0, y * gate, 0.0)

    @pl.when(e == n_e - 1)
    def _():
        o_ref[...] = _layer_norm(alpha * x_ref[...] + o_ref[...], g2_ref[...], b2_ref[...])


def _moe(x, gates, wgu, wd, sgu, sd, ln_g, ln_b, *, layer, alpha, tm):
    n, d = x.shape
    n_e = wgu.shape[1]
    assert n % tm == 0
    const = lambda i, e: (0, 0)
    return pl.pallas_call(
        functools.partial(_moe_kernel, alpha=alpha),
        grid=(n // tm, n_e),
        in_specs=[pl.BlockSpec((tm, d), lambda i, e: (i, 0)),
                  pl.BlockSpec((tm, LANES), lambda i, e: (i, 0)),
                  pl.BlockSpec((1, 1) + wgu.shape[2:], lambda i, e: (layer, e, 0, 0)),
                  pl.BlockSpec((1, 1) + wd.shape[2:], lambda i, e: (layer, e, 0, 0)),
                  pl.BlockSpec(sgu.shape, const),
                  pl.BlockSpec(sd.shape, const),
                  pl.BlockSpec((1, d), const),
                  pl.BlockSpec((1, d), const)],
        out_specs=pl.BlockSpec((tm, d), lambda i, e: (i, 0)),
        out_shape=jax.ShapeDtypeStruct((n, d), F32),
        scratch_shapes=[pltpu.VMEM((tm, d), BF16)],
        compiler_params=_cparams(("parallel", "arbitrary")),
    )(x, gates, wgu, wd, sgu, sd, ln_g.reshape(1, d), ln_b.reshape(1, d))


def _moe_tile(n):
    for cand in (1280, 1024, 512, 256, 128):
        if n % cand == 0:
            return cand
    return n


def kernel(x_prompt, x_sample, cache_k, cache_v, cache_kidx, state_C, state_n, state_m, state_pool, page_table, w_in, b_in, b_fgate, a_norm_g, c_map, c_scale, p_a, p_b, p_c, w_out, ln1_g, ln1_b, router_w, router_b, exp_gu, exp_down, sh_gu, sh_down, ln2_g, ln2_b):
    B, T, D = x_prompt.shape
    Bd, Td, _ = x_sample.shape
    depth = w_in.shape[0]
    alpha = (2 * depth) ** 0.25
    n_p, n_d = B * T, Bd * Td
    n = n_p + n_d
    L = MLSTM_CHUNK
    nc = T // L
    past_len = page_table.shape[1] * cache_k.shape[2]
    H, dh = A_HEADS, A_HEAD_DIM

    n_pool, page = cache_k.shape[1:3]
    cache_kt = jnp.transpose(cache_k, (0, 1, 3, 4, 2)).reshape(depth, n_pool, B_KV_WIDTH, page)
    cache_vt = jnp.transpose(cache_v, (0, 1, 3, 4, 2)).reshape(depth, n_pool, B_KV_WIDTH, page)
    cache_kit = jnp.transpose(cache_kidx, (0, 1, 3, 2))
    exp_gu_b = exp_gu.astype(BF16)
    exp_down_b = exp_down.astype(BF16)

    x = jnp.concatenate([x_prompt.reshape(n_p, D), x_sample.reshape(n_d, D)], axis=0)
    new_p, new_s = [], []
    for l in range(depth):
        pr = _project(x, w_in[l], b_in[l])
        gate_rows = pr['gate_rows']
        gr_p = jnp.transpose(gate_rows[:2 * H, :n_p].reshape(2 * H, B * nc, L), (1, 0, 2))
        gr_d = jnp.transpose(gate_rows[:2 * H, n_p:].reshape(2 * H, Bd, Td), (1, 0, 2))
        iw_p = gate_rows[2 * H:, :n_p]
        iw_d = gate_rows[2 * H:, n_p:]

        zc = jnp.zeros((B, H, dh, dh), F32)
        zn = jnp.zeros((B, H, dh), F32)
        zm = jnp.full((B, H), -jnp.inf, F32)
        ml = functools.partial(_mlstm, pr['a_q'], pr['a_k'], pr['a_v'], pr['a_o'])
        ha_p, c_p, nn_p, m_p = ml(gr_p, pr['gate_cols'], b_fgate[l], a_norm_g[l], zc, zn, zm,
                                  row0=0, n_seq=B, n_chunks=nc, L=L)
        ha_d, c_d, nn_d, m_d = ml(gr_d, pr['gate_cols'], b_fgate[l], a_norm_g[l], state_C[l], state_n[l], state_m[l],
                                  row0=n_p, n_seq=Bd, n_chunks=1, L=Td)

        bk, bv, bik = pr['b_k'], pr['b_v'], pr['b_ik']
        hb_p = _dsa_prompt(pr['b_q'][:n_p], pr['b_iq'][:n_p], iw_p, bk[:n_p], bv[:n_p], bik[:n_p], B=B, T=T)
        hb_d = _dsa_decode(pr['b_q'][n_p:], pr['b_iq'][n_p:], iw_d, bk[n_p:], bv[n_p:], bik[n_p:],
                           cache_kt, cache_vt, cache_kit, page_table, layer=l, Bd=Bd, Tq=Td)

        cu = pr['c_u']
        cu_p = cu[:n_p].reshape(B, T, C_WIDTH)
        cu_d = cu[n_p:].reshape(Bd, Td, C_WIDTH)
        zero_halo = jnp.zeros((B, POOL_HALO, C_WIDTH), F32)
        hc_p = _pool(cu_p, cu_p, zero_halo, c_map[l], c_scale[l], Tb=min(T, 512), pos0=0)
        halo_d = jnp.pad(state_pool[l], ((0, 0), (POOL_HALO - POOL_BUF, 0), (0, 0)))
        hc_d = _pool(cu_d, halo_d, halo_d, c_map[l], c_scale[l], Tb=Td, pos0=past_len)

        h_a = jnp.concatenate([ha_p, ha_d], axis=0)
        h_b = jnp.concatenate([hb_p, hb_d], axis=0)
        h_c = jnp.concatenate([hc_p.reshape(n_p, C_WIDTH), hc_d.reshape(n_d, C_WIDTH)], axis=0)
        x1 = _merge(x, h_a, h_b, h_c, pr['g_a'], pr['g_b'], pr['g_c'], p_a[l], p_b[l], p_c[l], w_out[l],
                    ln1_g[l], ln1_b[l], alpha=alpha)
        gates = _router(x1, router_w[l], router_b[l])
        x = _moe(x1, gates, exp_gu_b, exp_down_b, sh_gu[l].astype(BF16), sh_down[l].astype(BF16),
                 ln2_g[l], ln2_b[l], layer=l, alpha=alpha, tm=_moe_tile(n))

        kvs = (B_KV_HEADS, B_HEAD_DIM)
        pool_d = jnp.concatenate([state_pool[l], cu_d], axis=1)[:, -POOL_BUF:]
        new_p.append((bk[:n_p].reshape(B, T, *kvs), bv[:n_p].reshape(B, T, *kvs), bik[:n_p].reshape(B, T, IDX_DIM),
                      c_p, nn_p, m_p, cu_p[:, T - POOL_BUF:]))
        new_s.append((bk[n_p:].reshape(Bd, Td, *kvs), bv[n_p:].reshape(Bd, Td, *kvs),
                      bik[n_p:].reshape(Bd, Td, IDX_DIM), c_d, nn_d, m_d, pool_d))

    def stacked(states, i):
        return jnp.stack([s[i] for s in states])

    outs = [x[:n_p].reshape(B, T, D), x[n_p:].reshape(Bd, Td, D)]
    for i in range(7):
        outs += [stacked(new_p, i), stacked(new_s, i)]
    return tuple(outs)
```

```python
import functools
import math

import jax
import jax.numpy as jnp
from jax import lax
from jax.experimental import pallas as pl
from jax.experimental.pallas import tpu as pltpu

F32 = jnp.float32
BF16 = jnp.bfloat16
I32 = jnp.int32

A_HEADS = 4
A_HEAD_DIM = 128
A_WIDTH = A_HEADS * A_HEAD_DIM
MLSTM_CHUNK = 128
B_HEADS = 8
B_KV_HEADS = 2
B_HEAD_DIM = 64
B_WIDTH = B_HEADS * B_HEAD_DIM
B_KV_WIDTH = B_KV_HEADS * B_HEAD_DIM
B_GROUP = B_HEADS // B_KV_HEADS
IDX_HEADS = 8
IDX_DIM = 64
MAX_KEEP = 256
QUERY_BLOCK = 128
C_WINDOWS = (2, 4, 8, 16)
C_GROUPS = len(C_WINDOWS)
C_GROUP_WIDTH = 128
C_WIDTH = C_GROUPS * C_GROUP_WIDTH
POOL_BUF = max(C_WINDOWS) - 1
N_EXPERTS = 64
TOP_K = 8
N_EXPERT_GROUPS = 8
TOPK_GROUPS = 4
EXPERT_DIM = 256
ROUTED_SCALE = 2.5
LN_EPS = 1e-5

LANES = 128
SUBLANES = 8
VMEM_LIMIT = 56 * 1024 * 1024

INT_MIN = -2 ** 31
NEG_BIG = -1e30

DSA_UNROLL = 4
DECODE_PAGES_PER_STEP = 8
PROJ_TM = 256
MERGE_TM = 256
ROUTER_TM = 256
POOL_HALO = 16


def _cparams(sem):
    return pltpu.CompilerParams(dimension_semantics=sem, vmem_limit_bytes=VMEM_LIMIT)


def _dot(a, b):
    return jnp.dot(a, b, preferred_element_type=F32)


def _dot_nt(a, b):
    return lax.dot_general(a, b, (((1,), (1,)), ((), ())), preferred_element_type=F32)


def _dot_tn(a, b):
    return lax.dot_general(a, b, (((0,), (0,)), ((), ())), preferred_element_type=F32)


def _sigmoid(x):
    return 1.0 / (1.0 + jnp.exp(-x))


def _log_sigmoid(x):
    return jnp.minimum(x, 0.0) - jnp.log1p(jnp.exp(-jnp.abs(x)))


def _layer_norm(x, g, b):
    mu = jnp.mean(x, axis=-1, keepdims=True)
    xc = x - mu
    var = jnp.mean(xc * xc, axis=-1, keepdims=True)
    return xc * lax.rsqrt(var + LN_EPS) * g + b


def _bf16_round(x):
    return x.astype(BF16).astype(F32)


def _exact_pow2(scale):
    m = float(scale)
    assert math.frexp(m)[0] == 0.5, m
    return jnp.asarray(m, BF16)


def _sortable(s):
    bits = lax.bitcast_convert_type(s, I32)
    return jnp.where(bits < 0, bits ^ jnp.int32(0x7FFFFFFF), bits)


_PROJ_PLAIN = ('a_q', 'a_k', 'a_v', 'a_o', 'b_q', 'b_k', 'b_v', 'b_iq', 'b_ik', 'c_u', 'g_a', 'g_b', 'g_c')


def _proj_kernel(*refs):
    n_plain = len(_PROJ_PLAIN)
    x_ref = refs[0]
    w_refs = refs[1:1 + n_plain]
    b_refs = refs[1 + n_plain:1 + 2 * n_plain]
    wt, bt, wc, bc = refs[1 + 2 * n_plain:5 + 2 * n_plain]
    outs = refs[5 + 2 * n_plain:]
    xb = x_ref[...].astype(BF16)
    for w_ref, b_ref, o_ref in zip(w_refs, b_refs, outs[:n_plain]):
        o_ref[...] = _dot(xb, w_ref[...]) + b_ref[...]
    outs[n_plain][...] = _dot_nt(wt[...], xb) + bt[...]
    outs[n_plain + 1][...] = _dot(xb, wc[...]) + bc[...]


def _project(x, w_in, b_in):
    n, d = x.shape
    widths = (A_WIDTH, A_WIDTH, A_WIDTH, A_HEADS, A_HEADS, A_WIDTH,
              B_WIDTH, B_KV_WIDTH, B_KV_WIDTH, IDX_HEADS * IDX_DIM, IDX_DIM, IDX_HEADS,
              C_WIDTH, d, d, d)
    names = ('a_q', 'a_k', 'a_v', 'a_i', 'a_f', 'a_o', 'b_q', 'b_k', 'b_v', 'b_iq', 'b_ik', 'b_iw',
             'c_u', 'g_a', 'g_b', 'g_c')
    off, acc = {}, 0
    for nm, w in zip(names, widths):
        off[nm] = (acc, w)
        acc += w

    def cols(nm):
        s, w = off[nm]
        return w_in[:, s:s + w], b_in[s:s + w]

    ws, bs = [], []
    for nm in _PROJ_PLAIN:
        w, b = cols(nm)
        ws.append(w.astype(BF16))
        bs.append(b.reshape(1, -1))
    wi, bi = cols('a_i')
    wf, bf = cols('a_f')
    ww, bw = cols('b_iw')
    n_rows = 2 * A_HEADS + IDX_HEADS
    w_t = jnp.concatenate([wi, wf, ww], axis=1).T.astype(BF16)
    b_t = jnp.concatenate([bi, bf, bw]).reshape(n_rows, 1)
    w_c = jnp.pad(jnp.concatenate([wi, wf], axis=1), ((0, 0), (0, LANES - 2 * A_HEADS))).astype(BF16)
    b_c = jnp.pad(jnp.concatenate([bi, bf]), (0, LANES - 2 * A_HEADS)).reshape(1, LANES)

    tm = PROJ_TM
    assert n % tm == 0
    const = lambda i: (0, 0)
    in_specs = [pl.BlockSpec((tm, d), lambda i: (i, 0))]
    in_specs += [pl.BlockSpec(w.shape, const) for w in ws]
    in_specs += [pl.BlockSpec(b.shape, const) for b in bs]
    in_specs += [pl.BlockSpec(a.shape, const) for a in (w_t, b_t, w_c, b_c)]
    out_shape = [jax.ShapeDtypeStruct((n, w.shape[1]), F32) for w in ws]
    out_specs = [pl.BlockSpec((tm, w.shape[1]), lambda i: (i, 0)) for w in ws]
    out_shape += [jax.ShapeDtypeStruct((n_rows, n), F32), jax.ShapeDtypeStruct((n, LANES), F32)]
    out_specs += [pl.BlockSpec((n_rows, tm), lambda i: (0, i)), pl.BlockSpec((tm, LANES), lambda i: (i, 0))]
    res = pl.pallas_call(
        _proj_kernel, grid=(n // tm,), in_specs=in_specs, out_specs=out_specs, out_shape=out_shape,
        compiler_params=_cparams(("parallel",)),
    )(x, *ws, *bs, w_t, b_t, w_c, b_c)
    out = dict(zip(_PROJ_PLAIN, res[:len(_PROJ_PLAIN)]))
    out['gate_rows'] = res[-2]
    out['gate_cols'] = res[-1]
    return out


def _mlstm_kernel(q_ref, k_ref, v_ref, gr_ref, gc_ref, ao_ref, br_ref, bc_ref, ng_ref, c0_ref, n0_ref, m0_ref,
                  h_ref, c_out, n_out, m_out, c_s, n_s, m_s, *, L):
    c = pl.program_id(1)
    nc = pl.num_programs(1)

    @pl.when(c == 0)
    def _():
        c_s[...] = c0_ref[0]
        n_s[...] = n0_ref[0]
        m_s[...] = m0_ref[0]

    row = lax.broadcasted_iota(I32, (L, L), 0)
    col = lax.broadcasted_iota(I32, (L, L), 1)
    tril = row >= col
    triu = row <= col
    gr = gr_ref[0]
    gc = gc_ref[...]
    lf_rows = _log_sigmoid(gr + br_ref[...])
    lf_cols = _log_sigmoid(gc + bc_ref[...])
    k_scale = A_HEAD_DIM ** -0.5
    for h in range(A_HEADS):
        hs = slice(h * A_HEAD_DIM, (h + 1) * A_HEAD_DIM)
        q = q_ref[:, hs]
        k = k_ref[:, hs] * k_scale
        v = v_ref[:, hs]
        li_r = gr[h:h + 1, :]
        lf_r = lf_rows[A_HEADS + h:A_HEADS + h + 1, :]
        li_c = gc[:, h:h + 1]
        lf_c = lf_cols[:, A_HEADS + h:A_HEADS + h + 1]
        b_c = jnp.sum(jnp.where(tril, lf_r, 0.0), axis=-1, keepdims=True)
        b_r = jnp.sum(jnp.where(triu, lf_c, 0.0), axis=0, keepdims=True)
        m_prev = m_s[h]
        c_prev = c_s[h]
        n_prev = n_s[h]
        d_log = jnp.where(tril, b_c - b_r + li_r, -jnp.inf)
        inter = b_c + m_prev
        m_t = jnp.maximum(inter, jnp.max(d_log, axis=-1, keepdims=True))
        w_inter = jnp.exp(inter - m_t)
        qb = q.astype(BF16)
        kb = k.astype(BF16)
        vb = v.astype(BF16)
        a = jnp.exp(d_log - m_t) * _dot_nt(qb, kb)
        num = w_inter * _dot(qb, c_prev.astype(BF16)) + _dot(a.astype(BF16), vb)
        qn = jnp.sum(qb.astype(F32) * _bf16_round(n_prev), axis=-1, keepdims=True)
        den = w_inter * qn + jnp.sum(a, axis=-1, keepdims=True)
        hh = num / jnp.maximum(jnp.abs(den), jnp.exp(-m_t))
        m_new = m_t[L - 1:L, :]
        b_last = b_c[L - 1:L, :]
        w_c = jnp.exp(b_last - b_c + li_c - m_new)
        decay = jnp.exp(b_last + m_prev - m_new)
        kw = k * w_c
        c_s[h] = decay * c_prev + _dot_tn(kw.astype(BF16), vb)
        n_s[h] = decay * n_prev + jnp.sum(kb.astype(F32) * _bf16_round(w_c), axis=0, keepdims=True)
        m_s[h] = m_new
        mu = jnp.mean(hh, axis=-1, keepdims=True)
        hc = hh - mu
        var = jnp.mean(hc * hc, axis=-1, keepdims=True)
        hn = hc * lax.rsqrt(var + LN_EPS)
        h_ref[:, hs] = _sigmoid(ao_ref[:, hs]) * hn * ng_ref[:, hs]

    @pl.when(c == nc - 1)
    def _():
        c_out[0] = c_s[...]
        n_out[0] = n_s[...]
        m_out[0] = m_s[...]


def _mlstm(a_q, a_k, a_v, a_o, gate_rows, gate_cols, b_fgate, a_norm_g, c0, n0, m0, *, row0, n_seq, n_chunks, L):
    assert row0 % L == 0
    rb0 = row0 // L
    n_rows = n_seq * n_chunks * L
    H, dh = A_HEADS, A_HEAD_DIM
    tok = lambda b, c: (rb0 + b * n_chunks + c, 0)
    st = lambda b, c: (b, 0, 0, 0)
    const = lambda b, c: (0, 0)
    bias_r = jnp.pad(b_fgate, (A_HEADS, 0)).reshape(2 * A_HEADS, 1)
    bias_c = jnp.pad(b_fgate, (A_HEADS, LANES - 2 * A_HEADS)).reshape(1, LANES)
    outs = pl.pallas_call(
        functools.partial(_mlstm_kernel, L=L),
        grid=(n_seq, n_chunks),
        in_specs=[pl.BlockSpec((L, A_WIDTH), tok)] * 3
        + [pl.BlockSpec((1, 2 * A_HEADS, L), lambda b, c: (b * n_chunks + c, 0, 0)),
           pl.BlockSpec((L, LANES), tok),
           pl.BlockSpec((L, A_WIDTH), tok),
           pl.BlockSpec((2 * A_HEADS, 1), const),
           pl.BlockSpec((1, LANES), const),
           pl.BlockSpec((1, A_WIDTH), const),
           pl.BlockSpec((1, H, dh, dh), st),
           pl.BlockSpec((1, H, 1, dh), st),
           pl.BlockSpec((1, H, 1, 1), st)],
        out_specs=[pl.BlockSpec((L, A_WIDTH), lambda b, c: (b * n_chunks + c, 0)),
                   pl.BlockSpec((1, H, dh, dh), st),
                   pl.BlockSpec((1, H, 1, dh), st),
                   pl.BlockSpec((1, H, 1, 1), st)],
        out_shape=[jax.ShapeDtypeStruct((n_rows, A_WIDTH), F32),
                   jax.ShapeDtypeStruct((n_seq, H, dh, dh), F32),
                   jax.ShapeDtypeStruct((n_seq, H, 1, dh), F32),
                   jax.ShapeDtypeStruct((n_seq, H, 1, 1), F32)],
        scratch_shapes=[pltpu.VMEM((H, dh, dh), F32), pltpu.VMEM((H, 1, dh), F32), pltpu.VMEM((H, 1, 1), F32)],
        compiler_params=_cparams(("parallel", "arbitrary")),
    )(a_q, a_k, a_v, gate_rows, gate_cols, a_o, bias_r, bias_c, a_norm_g.reshape(1, A_WIDTH),
      c0, n0.reshape(n_seq, H, 1, dh), m0.reshape(n_seq, H, 1, 1))
    h, c_new, n_new, m_new = outs
    return h, c_new, n_new.reshape(n_seq, H, dh), m_new.reshape(n_seq, H)


def _threshold_search(count_ge, n_keep, shape, n_total):
    def body(j, carry):
        t, cnt_t = carry
        cand = t ^ (jnp.int32(1) << (31 - j))
        cnt = count_ge(cand)
        ok = cnt >= n_keep
        return jnp.where(ok, cand, t), jnp.where(ok, cnt, cnt_t)
    return lax.fori_loop(0, 32, body, (jnp.full(shape, INT_MIN, I32), jnp.full(shape, n_total, F32)))


def _tree_sum(parts):
    parts = list(parts)
    while len(parts) > 1:
        parts = [parts[i] + parts[i + 1] for i in range(0, len(parts) - 1, 2)] + parts[len(parts) & ~1:]
    return parts[0]


def _dsa_prompt_kernel(qi_ref, w_ref, ki_ref, q_ref, k_ref, vt_ref, o_ref, s_ref, acc_ref, m_ref, l_ref,
                       bias_s, lg_s, p_s, hi_s, lo_s, *, n_keep):
    QB = QUERY_BLOCK
    U = DSA_UNROLL
    SK = U * QB
    PW = 2 * QB
    i = pl.program_id(1)
    n_trips = (i + U) // U
    key_minus_qry = lax.broadcasted_iota(I32, (QB, QB), 0) - lax.broadcasted_iota(I32, (QB, QB), 1)
    qi = qi_ref[0, 0]
    w = _bf16_round(w_ref[0, 0] * (IDX_HEADS ** -0.5))

    def score_body(t, carry):
        for u in range(U):
            c = t * U + u
            ki = ki_ref[0, pl.ds(pl.multiple_of(c * QB, QB), QB), :]
            s = jnp.zeros((QB, QB), F32)
            for hp in range(IDX_HEADS // 2):
                d = _dot(ki, qi[:, hp * PW:(hp + 1) * PW])
                r = jnp.maximum(d.astype(BF16), 0).astype(F32)
                for h in (2 * hp, 2 * hp + 1):
                    s = s + r[:, (h % 2) * QB:(h % 2 + 1) * QB] * w[h:h + 1, :]
            visible = key_minus_qry <= (i - c) * QB
            s_ref[c] = jnp.where(visible, _sortable(s), INT_MIN)
            top = lax.bitcast_convert_type(lax.bitcast_convert_type(s, I32) & jnp.int32(-65536), F32)
            hi_s[c] = jnp.where(visible, top, jnp.nan).astype(BF16)
        return carry

    lax.fori_loop(0, n_trips, score_body, 0)

    def count(pred):
        def body(t, acc):
            parts = []
            for u in range(U):
                hit = jnp.where(pred(s_ref[t * U + u]), 1.0, 0.0)
                parts += [hit[r * SUBLANES:(r + 1) * SUBLANES, :] for r in range(QB // SUBLANES)]
            return acc + _tree_sum(parts)
        acc = lax.fori_loop(0, n_trips, body, jnp.zeros((SUBLANES, QB), F32))
        return jnp.sum(acc, axis=0, keepdims=True)

    def count16(ref, cand):
        one, zero = jnp.ones((), BF16), jnp.zeros((), BF16)
        rows = 2 * SUBLANES

        def body(t, acc):
            parts = []
            for u in range(U):
                hit = jnp.where(ref[t * U + u] >= cand, one, zero)
                parts += [hit[r * rows:(r + 1) * rows, :] for r in range(QB // rows)]
            part = _tree_sum(parts).astype(F32)
            return acc + part[:SUBLANES] + part[SUBLANES:]
        acc = lax.fori_loop(0, n_trips, body, jnp.zeros((SUBLANES, QB), F32))
        return jnp.sum(acc, axis=0, keepdims=True)

    def digit_search(ref, n_bits, start, cnt_start, to_plane):
        def body(j, carry):
            d, cnt_d = carry
            cand = d + (jnp.int32(1) << (n_bits - 1 - j))
            cnt = count16(ref, to_plane(cand))
            ok = cnt >= keep
            return jnp.where(ok, cand, d), jnp.where(ok, cnt, cnt_d)
        return lax.fori_loop(0, n_bits, body, (start, cnt_start))

    def prefix_to_plane(p):
        p = jnp.where((p > 0) & (p < 0x80), 0x80, p)
        bits = jnp.where(p < 0, p ^ jnp.int32(0x7FFF), p) & jnp.int32(0xFFFF)
        return lax.bitcast_convert_type(bits << 16, F32).astype(BF16)

    def byte_to_plane(b):
        return b.astype(F32).astype(BF16)

    def byte_plane(hi_digit, mid_digit, shift):
        def body(t, carry):
            for u in range(U):
                key = s_ref[t * U + u]
                hi = key >> 16
                above, match = hi > hi_digit, hi == hi_digit
                if mid_digit is not None:
                    mid = (key >> 8) & 0xFF
                    above = above | (match & (mid > mid_digit))
                    match = match & (mid == mid_digit)
                byte = ((key >> shift) & 0xFF).astype(F32)
                lo_s[t * U + u] = jnp.where(above, 256.0, jnp.where(match, byte, -1.0)).astype(BF16)
            return carry
        lax.fori_loop(0, n_trips, body, 0)

    keep = float(n_keep)
    n_total = (n_trips * SK).astype(F32)
    row = lambda v, dt: jnp.full((1, QB), v, dt)
    d_hi, cnt_thr = digit_search(hi_s, 16, row(-2 ** 15, I32), jnp.full((1, QB), n_total, F32), prefix_to_plane)
    byte_plane(d_hi, None, 8)
    d_mid, cnt_thr = digit_search(lo_s, 8, row(0, I32), cnt_thr, byte_to_plane)
    byte_plane(d_hi, d_mid, 0)
    d_lo, cnt_thr = digit_search(lo_s, 8, row(0, I32), cnt_thr, byte_to_plane)
    thr = d_hi * 65536 + d_mid * 256 + d_lo
    has_ties = jnp.max(jnp.where((thr != INT_MIN) & (cnt_thr > keep), 1.0, 0.0)) > 0.5

    m_ref[...] = jnp.full(m_ref.shape, NEG_BIG, F32)
    l_ref[...] = jnp.zeros(l_ref.shape, F32)
    acc_ref[...] = jnp.zeros(acc_ref.shape, F32)
    q = q_ref[0, 0]
    pairs_per_group = B_GROUP // 2

    def fold(x, op):
        parts = [x[r * SUBLANES:(r + 1) * SUBLANES, :] for r in range(QB // SUBLANES)]
        while len(parts) > 1:
            parts = [op(parts[a], parts[a + 1]) for a in range(0, len(parts), 2)]
        return parts[0]

    n_pairs = B_HEADS // 2
    halves = [slice(e * QB, (e + 1) * QB) for e in range(2)]

    def logits_phase(t, slot):
        off = pl.multiple_of(t * SK, SK)
        mx = [[None, None] for _ in range(n_pairs)]
        for u in range(U):
            bias = bias_s[u]
            for g in range(B_KV_HEADS):
                kc = k_ref[0, g, pl.ds(off + u * QB, QB), :]
                for hp in range(g * pairs_per_group, (g + 1) * pairs_per_group):
                    lg = _dot(kc, q[:, hp * PW:(hp + 1) * PW])
                    for e, cs in enumerate(halves):
                        piece = lg[:, cs] + bias
                        lg_s[slot, hp, u, :, cs] = piece
                        pm = fold(piece, jnp.maximum)
                        mx[hp][e] = pm if mx[hp][e] is None else jnp.maximum(mx[hp][e], pm)
        return tuple(jnp.concatenate([jnp.max(mx[hp][e], axis=0, keepdims=True) for e in range(2)], axis=1)
                     for hp in range(n_pairs))

    def values_phase(t, slot, mx):
        off = pl.multiple_of(t * SK, SK)
        alpha = []
        for hp in range(n_pairs):
            m_old = m_ref[hp]
            m_new = jnp.maximum(m_old, mx[hp])
            alpha.append(jnp.exp(m_old - m_new))
            m_ref[hp] = m_new
            ls = [None, None]
            for u in range(U):
                for e, cs in enumerate(halves):
                    p = jnp.exp(lg_s[slot, hp, u, :, cs] - m_new[:, cs])
                    p_s[hp, u * QB:(u + 1) * QB, cs] = p.astype(BF16)
                    ps = fold(p, jnp.add)
                    ls[e] = ps if ls[e] is None else ls[e] + ps
            l_new = jnp.concatenate([jnp.sum(ls[e], axis=0, keepdims=True) for e in range(2)], axis=1)
            l_ref[hp] = alpha[hp] * l_ref[hp] + l_new
        for g in range(B_KV_HEADS):
            vt = vt_ref[0, g, :, pl.ds(off, SK)]
            for hp in range(g * pairs_per_group, (g + 1) * pairs_per_group):
                acc_ref[hp] = alpha[hp] * acc_ref[hp] + _dot(vt, p_s[hp])

    @pl.when(jnp.logical_not(has_ties))
    def _():
        thr_eff = jnp.maximum(thr, INT_MIN + 1)

        def select(t):
            for u in range(U):
                bias_s[u] = jnp.where(s_ref[t * U + u] >= thr_eff, 0.0, NEG_BIG)

        select(0)
        mx0 = logits_phase(0, 0)

        def body(t, mx):
            values_phase(t, 0, mx)
            nxt = jnp.minimum(t + 1, n_trips - 1)
            select(nxt)
            return logits_phase(nxt, 0)

        lax.fori_loop(0, n_trips, body, mx0)

    @pl.when(has_ties)
    def _():
        n_ties = keep - count(lambda blk: blk > thr)
        tri = (lax.broadcasted_iota(I32, (SK, SK), 0) >= lax.broadcasted_iota(I32, (SK, SK), 1)).astype(BF16)

        def body(t, run):
            blk = s_ref[pl.ds(t * U, U)].reshape(SK, QB)
            tie = (blk == thr) & (blk != INT_MIN)
            prefix = _dot(tri, jnp.where(tie, 1.0, 0.0).astype(BF16))
            sel = (blk > thr) | (tie & (run + prefix <= n_ties))
            bias_s[...] = jnp.where(sel, 0.0, NEG_BIG).reshape(U, QB, QB)
            values_phase(t, 0, logits_phase(t, 0))
            return run + prefix[SK - 1:SK, :]

        lax.fori_loop(0, n_trips, body, jnp.zeros((1, QB), F32))

    for hp in range(B_HEADS // 2):
        g, pr = divmod(hp, pairs_per_group)
        o_ref[0, 0, g, :, pr * PW:(pr + 1) * PW] = acc_ref[hp] / l_ref[hp]


def _dsa_prompt(b_q, b_iq, iw_rows, b_k, b_v, b_ik, *, B, T):
    QB = QUERY_BLOCK
    nb = T // QB
    n_keep = min(MAX_KEEP, T // 4)

    assert nb % DSA_UNROLL == 0

    def head_t(a, heads, dim):
        a = a.reshape(B, nb, QB, heads, dim).astype(BF16) * _exact_pow2(dim ** -0.5)
        return jnp.transpose(a, (0, 1, 4, 3, 2)).reshape(B, nb, dim, heads * QB)

    qi_t = head_t(b_iq, IDX_HEADS, IDX_DIM)
    q_t = head_t(b_q, B_HEADS, B_HEAD_DIM)
    w_t = jnp.transpose(iw_rows.reshape(IDX_HEADS, B, nb, QB), (1, 2, 0, 3))
    ki = b_ik.reshape(B, T, IDX_DIM).astype(BF16)
    kh = jnp.transpose(b_k.reshape(B, T, B_KV_HEADS, B_HEAD_DIM), (0, 2, 1, 3)).astype(BF16)
    vt = jnp.transpose(b_v.reshape(B, T, B_KV_HEADS, B_HEAD_DIM), (0, 2, 3, 1)).astype(BF16)
    GW = B_GROUP * QB
    n_pairs = B_HEADS // 2
    blk = lambda b, i: (b, i, 0, 0)
    out = pl.pallas_call(
        functools.partial(_dsa_prompt_kernel, n_keep=n_keep),
        grid=(B, nb),
        in_specs=[pl.BlockSpec((1, 1, IDX_DIM, IDX_HEADS * QB), blk),
                  pl.BlockSpec((1, 1, IDX_HEADS, QB), blk),
                  pl.BlockSpec((1, T, IDX_DIM), lambda b, i: (b, 0, 0)),
                  pl.BlockSpec((1, 1, B_HEAD_DIM, B_HEADS * QB), blk),
                  pl.BlockSpec((1, B_KV_HEADS, T, B_HEAD_DIM), lambda b, i: (b, 0, 0, 0)),
                  pl.BlockSpec((1, B_KV_HEADS, B_HEAD_DIM, T), lambda b, i: (b, 0, 0, 0))],
        out_specs=pl.BlockSpec((1, 1, B_KV_HEADS, B_HEAD_DIM, GW), lambda b, i: (b, i, 0, 0, 0)),
        out_shape=jax.ShapeDtypeStruct((B, nb, B_KV_HEADS, B_HEAD_DIM, GW), F32),
        scratch_shapes=[pltpu.VMEM((nb, QB, QB), I32),
                        pltpu.VMEM((n_pairs, B_HEAD_DIM, 2 * QB), F32),
                        pltpu.VMEM((n_pairs, 1, 2 * QB), F32),
                        pltpu.VMEM((n_pairs, 1, 2 * QB), F32),
                        pltpu.VMEM((DSA_UNROLL, QB, QB), F32),
                        pltpu.VMEM((1, n_pairs, DSA_UNROLL, QB, 2 * QB), F32),
                        pltpu.VMEM((n_pairs, DSA_UNROLL * QB, 2 * QB), BF16),
                        pltpu.VMEM((nb, QB, QB), BF16),
                        pltpu.VMEM((nb, QB, QB), BF16)],
        compiler_params=_cparams(("parallel", "arbitrary")),
    )(qi_t, w_t, ki, q_t, kh, vt)
    out = out.reshape(B, nb, B_KV_HEADS, B_HEAD_DIM, B_GROUP, QB)
    return jnp.transpose(out, (0, 1, 5, 2, 4, 3)).reshape(B * T, B_WIDTH)


def _dsa_decode_kernel(pt_ref, qi_ref, w_ref, q_ref, *rest, n_keep, n_q, pps):
    ck_refs, cv_refs, cki_refs = rest[0:pps], rest[pps:2 * pps], rest[2 * pps:3 * pps]
    nk_ref, nv_ref, nki_ref, o_ref, s_ref, k_s, v_s = rest[3 * pps:]
    P = LANES
    p = pl.program_id(1)
    n_steps = pl.num_programs(1)
    n_rows = IDX_HEADS * n_q
    qi = qi_ref[0]
    w = _bf16_round(w_ref[0] * (IDX_HEADS ** -0.5))

    def chunk_scores(ki_t):
        t = _bf16_round(jnp.maximum(_dot(qi, ki_t), 0.0)) * w
        s = jnp.zeros((n_q, P), F32)
        for h in range(IDX_HEADS):
            s = s + t[h * n_q:(h + 1) * n_q, :]
        return s

    for j in range(pps):
        off = pl.multiple_of((p * pps + j) * P, P)
        s_ref[:, pl.ds(off, P)] = _sortable(chunk_scores(cki_refs[j][0, 0].astype(BF16)))
        k_s[:, pl.ds(off, P)] = ck_refs[j][0, 0].astype(BF16)
        v_s[:, pl.ds(off, P)] = cv_refs[j][0, 0].astype(BF16)

    @pl.when(p == n_steps - 1)
    def _():
        n_keys = s_ref.shape[1]
        past = n_keys - P
        n_chunks = n_keys // P
        key_j = lax.broadcasted_iota(I32, (n_q, P), 1)
        qry_t = lax.broadcasted_iota(I32, (n_q, P), 0)
        visible = key_j <= qry_t
        s_ref[:, pl.ds(past, P)] = jnp.where(visible, _sortable(chunk_scores(nki_ref[0].astype(BF16))), INT_MIN)
        k_s[:, pl.ds(past, P)] = nk_ref[0].astype(BF16)
        v_s[:, pl.ds(past, P)] = nv_ref[0].astype(BF16)

        def count(pred):
            hit = jnp.where(pred(s_ref[...]), 1.0, 0.0)
            return jnp.sum(_tree_sum([hit[:, c * P:(c + 1) * P] for c in range(n_chunks)]), axis=-1, keepdims=True)

        keep = float(n_keep)
        thr, cnt_thr = _threshold_search(lambda cand: count(lambda blk: blk >= cand), keep, (n_q, 1), float(n_keys))
        has_ties = jnp.max(jnp.where((thr != INT_MIN) & (cnt_thr > keep), 1.0, 0.0)) > 0.5

        @pl.when(jnp.logical_not(has_ties))
        def _():
            thr_eff = jnp.maximum(thr, INT_MIN + 1)
            s_ref[...] = jnp.where(s_ref[...] >= thr_eff, 1, 0).astype(I32)

        @pl.when(has_ties)
        def _():
            n_ties = keep - count(lambda blk: blk > thr)
            tri = (lax.broadcasted_iota(I32, (P, P), 0) <= lax.broadcasted_iota(I32, (P, P), 1)).astype(BF16)

            def sel_body(c, run):
                o = pl.multiple_of(c * P, P)
                blk = s_ref[:, pl.ds(o, P)]
                tie = (blk == thr) & (blk != INT_MIN)
                prefix = _dot(jnp.where(tie, 1.0, 0.0).astype(BF16), tri)
                sel = (blk > thr) | (tie & (run + prefix <= n_ties))
                s_ref[:, pl.ds(o, P)] = jnp.where(sel, 1, 0).astype(I32)
                return run + prefix[:, P - 1:P]

            lax.fori_loop(0, n_chunks, sel_body, jnp.zeros((n_q, 1), F32))

        sel = s_ref[...] > 0
        q = q_ref[0]
        lg = _dot(q, k_s[...])
        L = lg.shape[1]
        lg = jnp.where(sel[None], lg.reshape(B_HEADS, n_q, L), NEG_BIG)
        m = jnp.max(lg, axis=-1, keepdims=True)
        pr = jnp.exp(lg - m)
        pr = pr / jnp.sum(pr, axis=-1, keepdims=True)
        o_ref[0] = _dot_nt(pr.reshape(n_rows, L).astype(BF16), v_s[...])


def _dsa_decode(b_q, b_iq, iw_rows, b_k, b_v, b_ik, cache_kt, cache_vt, cache_kit, page_table, *, layer, Bd, Tq):
    page = cache_kt.shape[-1]
    assert page == LANES
    n_pages = page_table.shape[1]
    past = n_pages * page
    n_keep = min(MAX_KEEP, (past + Tq) // 4)
    n_rows = B_HEADS * Tq
    kvw = B_KV_HEADS * B_HEAD_DIM

    def rows_hq(a, heads, dim):
        return jnp.transpose(a.reshape(Bd, Tq, heads, dim), (0, 2, 1, 3)).reshape(Bd, heads * Tq, dim)

    qi = rows_hq(b_iq, IDX_HEADS, IDX_DIM).astype(BF16) * _exact_pow2(IDX_DIM ** -0.5)
    w = jnp.transpose(iw_rows.reshape(IDX_HEADS, Bd, Tq), (1, 0, 2)).reshape(Bd, IDX_HEADS * Tq, 1)
    qh = rows_hq(b_q, B_HEADS, B_HEAD_DIM).astype(BF16) * _exact_pow2(B_HEAD_DIM ** -0.5)
    group = (jnp.arange(n_rows) // Tq) // B_GROUP
    lane_group = jnp.arange(kvw) // B_HEAD_DIM
    q_wide = jnp.where(group[:, None] == lane_group[None, :], jnp.tile(qh, (1, 1, B_KV_HEADS)), 0.0).astype(BF16)
    pps = DECODE_PAGES_PER_STEP if n_pages % DECODE_PAGES_PER_STEP == 0 else 1

    def pad_new(a, width):
        return jnp.pad(jnp.transpose(a.reshape(Bd, Tq, width), (0, 2, 1)), ((0, 0), (0, 0), (0, page - Tq)))

    nk, nv, nki = pad_new(b_k, kvw), pad_new(b_v, kvw), pad_new(b_ik, IDX_DIM)
    n_keys = past + page
    seq = lambda b, p, pt: (b, 0, 0)
    pages = lambda width: [pl.BlockSpec((1, 1, width, page), lambda b, p, pt, j=j: (layer, pt[b, p * pps + j], 0, 0))
                           for j in range(pps)]
    out = pl.pallas_call(
        functools.partial(_dsa_decode_kernel, n_keep=n_keep, n_q=Tq, pps=pps),
        grid_spec=pltpu.PrefetchScalarGridSpec(
            num_scalar_prefetch=1,
            grid=(Bd, n_pages // pps),
            in_specs=[pl.BlockSpec((1, n_rows, IDX_DIM), seq),
                      pl.BlockSpec((1, n_rows, 1), seq),
                      pl.BlockSpec((1, n_rows, kvw), seq)]
            + pages(kvw) + pages(kvw) + pages(IDX_DIM)
            + [pl.BlockSpec((1, kvw, page), seq),
               pl.BlockSpec((1, kvw, page), seq),
               pl.BlockSpec((1, IDX_DIM, page), seq)],
            out_specs=pl.BlockSpec((1, n_rows, kvw), seq),
            scratch_shapes=[pltpu.VMEM((Tq, n_keys), I32),
                            pltpu.VMEM((kvw, n_keys), BF16),
                            pltpu.VMEM((kvw, n_keys), BF16)]),
        out_shape=jax.ShapeDtypeStruct((Bd, n_rows, kvw), F32),
        compiler_params=_cparams(("parallel", "arbitrary")),
    )(page_table, qi, w, q_wide, *([cache_kt] * pps), *([cache_vt] * pps), *([cache_kit] * pps), nk, nv, nki)
    out = out.reshape(Bd, B_HEADS, Tq, B_KV_HEADS, B_HEAD_DIM)
    out = jnp.concatenate([out[:, g * B_GROUP:(g + 1) * B_GROUP, :, g] for g in range(B_KV_HEADS)], axis=1)
    return jnp.transpose(out, (0, 2, 1, 3)).reshape(Bd * Tq, B_WIDTH)


def _pool_kernel(u_ref, prev_ref, halo0_ref, cmap_ref, cscale_ref, o_ref, ext, *, Tb, pos0):
    j = pl.program_id(1)
    H = POOL_HALO
    ext[0:H, :] = jnp.where(j == 0, halo0_ref[0], prev_ref[0])
    ext[H:H + Tb, :] = u_ref[0]
    pos = pos0 + j * Tb + lax.broadcasted_iota(I32, (Tb, 1), 0)
    for g, wdw in enumerate(C_WINDOWS):
        ls = slice(g * C_GROUP_WIDTH, (g + 1) * C_GROUP_WIDTH)
        win = ext[H:H + Tb, ls]
        for d in range(1, wdw):
            win = win + ext[H - d:H - d + Tb, ls]
        cnt = jnp.minimum(wdw, pos + 1).astype(F32)
        pooled = win / cnt - ext[H:H + Tb, ls]
        z = _dot(pooled.astype(BF16), cmap_ref[g])
        o_ref[0, :, ls] = z * cscale_ref[:, ls]


def _pool(u, prev_src, halo0, c_map, c_scale, *, Tb, pos0):
    Bn, R, C = u.shape
    H = POOL_HALO
    assert R % Tb == 0 and (Tb % H == 0 or R == Tb)
    prev_idx = (lambda b, j: (b, jnp.maximum(j * (Tb // H) - 1, 0), 0)) if R > Tb else (lambda b, j: (b, 0, 0))
    return pl.pallas_call(
        functools.partial(_pool_kernel, Tb=Tb, pos0=pos0),
        grid=(Bn, R // Tb),
        in_specs=[pl.BlockSpec((1, Tb, C), lambda b, j: (b, j, 0)),
                  pl.BlockSpec((1, H, C), prev_idx),
                  pl.BlockSpec((1, H, C), lambda b, j: (b, 0, 0)),
                  pl.BlockSpec(c_map.shape, lambda b, j: (0, 0, 0)),
                  pl.BlockSpec((1, C), lambda b, j: (0, 0))],
        out_specs=pl.BlockSpec((1, Tb, C), lambda b, j: (b, j, 0)),
        out_shape=jax.ShapeDtypeStruct((Bn, R, C), F32),
        scratch_shapes=[pltpu.VMEM((H + Tb, C), F32)],
        compiler_params=_cparams(("parallel", "parallel")),
    )(u, prev_src, halo0, c_map.astype(BF16), c_scale.reshape(1, C))


def _merge_kernel(x_ref, ha_ref, hb_ref, hc_ref, ga_ref, gb_ref, gc_ref, pa_ref, pb_ref, pc_ref, wo_ref,
                  g1_ref, b1_ref, o_ref, *, alpha):
    merged = (_sigmoid(ga_ref[...]) * _dot(ha_ref[...].astype(BF16), pa_ref[...])
              + _sigmoid(gb_ref[...]) * _dot(hb_ref[...].astype(BF16), pb_ref[...])
              + _sigmoid(gc_ref[...]) * _dot(hc_ref[...].astype(BF16), pc_ref[...]))
    y = alpha * x_ref[...] + _dot(merged.astype(BF16), wo_ref[...])
    o_ref[...] = _layer_norm(y, g1_ref[...], b1_ref[...])


def _merge(x, h_a, h_b, h_c, g_a, g_b, g_c, p_a, p_b, p_c, w_out, ln_g, ln_b, *, alpha):
    n, d = x.shape
    tm = MERGE_TM
    assert n % tm == 0
    tok = lambda w: pl.BlockSpec((tm, w), lambda i: (i, 0))
    full = lambda a: pl.BlockSpec(a.shape, lambda i: (0, 0))
    ws = [p_a.astype(BF16), p_b.astype(BF16), p_c.astype(BF16), w_out.astype(BF16), ln_g.reshape(1, d), ln_b.reshape(1, d)]
    return pl.pallas_call(
        functools.partial(_merge_kernel, alpha=alpha),
        grid=(n // tm,),
        in_specs=[tok(d), tok(h_a.shape[1]), tok(h_b.shape[1]), tok(h_c.shape[1]), tok(d), tok(d), tok(d)]
        + [full(a) for a in ws],
        out_specs=tok(d),
        out_shape=jax.ShapeDtypeStruct((n, d), F32),
        compiler_params=_cparams(("parallel",)),
    )(x, h_a, h_b, h_c, g_a, g_b, g_c, *ws)


def _router_kernel(x_ref, w_ref, rb_ref, o_ref):
    tm = x_ref.shape[0]
    E = LANES
    per_group = N_EXPERTS // N_EXPERT_GROUPS
    scores = _sigmoid(_dot(x_ref[...].astype(BF16), w_ref[...]))
    biased = scores + rb_ref[...]
    lane_i = lax.broadcasted_iota(I32, (tm, E), 1)
    lane = lane_i.astype(F32)
    grp = lane_i // per_group
    neg = -jnp.inf

    def first_max(a):
        m = jnp.max(a, axis=-1, keepdims=True)
        idx = jnp.min(jnp.where(a == m, lane, float(E)), axis=-1, keepdims=True)
        return m, idx

    gscore = []
    for g in range(N_EXPERT_GROUPS):
        bg = jnp.where(grp == g, biased, neg)
        m1, i1 = first_max(bg)
        m2, _ = first_max(jnp.where(lane == i1, neg, bg))
        gscore.append(m1 + m2)
    expert_ok = jnp.zeros((tm, E), jnp.bool_)
    for g in range(N_EXPERT_GROUPS):
        beaten = jnp.zeros((tm, 1), F32)
        for o in range(N_EXPERT_GROUPS):
            if o == g:
                continue
            wins = (gscore[o] > gscore[g]) | ((gscore[o] == gscore[g]) & (o < g))
            beaten = beaten + jnp.where(wins, 1.0, 0.0)
        expert_ok = expert_ok | ((grp == g) & (beaten < float(TOPK_GROUPS)))
    rem = jnp.where(expert_ok, biased, neg)
    chosen = jnp.zeros((tm, E), jnp.bool_)
    for _ in range(TOP_K):
        _, idx = first_max(rem)
        hit = lane == idx
        chosen = chosen | hit
        rem = jnp.where(hit, neg, rem)
    s_sel = jnp.where(chosen, scores, 0.0)
    gates = ROUTED_SCALE * s_sel / jnp.sum(s_sel, axis=-1, keepdims=True)
    o_ref[...] = gates


def _router(x, router_w, router_b):
    n, d = x.shape
    tm = ROUTER_TM
    assert n % tm == 0
    pad = LANES - N_EXPERTS
    w = jnp.pad(router_w, ((0, 0), (0, pad))).astype(BF16)
    full = lambda a: pl.BlockSpec(a.shape, lambda i: (0, 0))
    rb = jnp.pad(router_b, (0, pad), constant_values=-jnp.inf).reshape(1, LANES)
    return pl.pallas_call(
        _router_kernel, grid=(n // tm,),
        in_specs=[pl.BlockSpec((tm, d), lambda i: (i, 0)), full(w), full(rb)],
        out_specs=pl.BlockSpec((tm, LANES), lambda i: (i, 0)),
        out_shape=jax.ShapeDtypeStruct((n, LANES), F32),
        compiler_params=_cparams(("parallel",)),
    )(x, w, rb)


def _moe_kernel(x_ref, g_ref, wgu_ref, wd_ref, sgu_ref, sd_ref, g2_ref, b2_ref, o_ref, xb_s, *, alpha):
    e = pl.program_id(1)
    n_e = pl.num_programs(1)

    def swiglu(w_gu, w_down):
        gu = _dot(xb_s[...], w_gu)
        gg = gu[:, :EXPERT_DIM]
        act = gg * _sigmoid(gg) * gu[:, EXPERT_DIM:]
        return _dot(act.astype(BF16), w_down)

    @pl.when(e == 0)
    def _():
        xb_s[...] = x_ref[...].astype(BF16)
        o_ref[...] = swiglu(sgu_ref[...], sd_ref[...])

    gates = g_ref[...]
    lane = lax.broadcasted_iota(I32, gates.shape, 1)
    gate = jnp.sum(jnp.where(lane == e, gates, 0.0), axis=-1, keepdims=True)
    y = swiglu(wgu_ref[0, 0], wd_ref[0, 0])
    o_ref[...] += jnp.where(gate != 0.0, y * gate, 0.0)

    @pl.when(e == n_e - 1)
    def _():
        o_ref[...] = _layer_norm(alpha * x_ref[...] + o_ref[...], g2_ref[...], b2_ref[...])


def _moe(x, gates, wgu, wd, sgu, sd, ln_g, ln_b, *, layer, alpha, tm):
    n, d = x.shape
    n_e = wgu.shape[1]
    assert n % tm == 0
    const = lambda i, e: (0, 0)
    return pl.pallas_call(
        functools.partial(_moe_kernel, alpha=alpha),
        grid=(n // tm, n_e),
        in_specs=[pl.BlockSpec((tm, d), lambda i, e: (i, 0)),
                  pl.BlockSpec((tm, LANES), lambda i, e: (i, 0)),
                  pl.BlockSpec((1, 1) + wgu.shape[2:], lambda i, e: (layer, e, 0, 0)),
                  pl.BlockSpec((1, 1) + wd.shape[2:], lambda i, e: (layer, e, 0, 0)),
                  pl.BlockSpec(sgu.shape, const),
                  pl.BlockSpec(sd.shape, const),
                  pl.BlockSpec((1, d), const),
                  pl.BlockSpec((1, d), const)],
        out_specs=pl.BlockSpec((tm, d), lambda i, e: (i, 0)),
        out_shape=jax.ShapeDtypeStruct((n, d), F32),
        scratch_shapes=[pltpu.VMEM((tm, d), BF16)],
        compiler_params=_cparams(("parallel", "arbitrary")),
    )(x, gates, wgu, wd, sgu, sd, ln_g.reshape(1, d), ln_b.reshape(1, d))


def _moe_tile(n):
    for cand in (1280, 1024, 512, 256, 128):
        if n % cand == 0:
            return cand
    return n


def kernel(x_prompt, x_sample, cache_k, cache_v, cache_kidx, state_C, state_n, state_m, state_pool, page_table, w_in, b_in, b_fgate, a_norm_g, c_map, c_scale, p_a, p_b, p_c, w_out, ln1_g, ln1_b, router_w, router_b, exp_gu, exp_down, sh_gu, sh_down, ln2_g, ln2_b):
    B, T, D = x_prompt.shape
    Bd, Td, _ = x_sample.shape
    depth = w_in.shape[0]
    alpha = (2 * depth) ** 0.25
    n_p, n_d = B * T, Bd * Td
    n = n_p + n_d
    L = MLSTM_CHUNK
    nc = T // L
    past_len = page_table.shape[1] * cache_k.shape[2]
    H, dh = A_HEADS, A_HEAD_DIM

    n_pool, page = cache_k.shape[1:3]
    cache_kt = jnp.transpose(cache_k, (0, 1, 3, 4, 2)).reshape(depth, n_pool, B_KV_WIDTH, page)
    cache_vt = jnp.transpose(cache_v, (0, 1, 3, 4, 2)).reshape(depth, n_pool, B_KV_WIDTH, page)
    cache_kit = jnp.transpose(cache_kidx, (0, 1, 3, 2))
    exp_gu_b = exp_gu.astype(BF16)
    exp_down_b = exp_down.astype(BF16)

    x = jnp.concatenate([x_prompt.reshape(n_p, D), x_sample.reshape(n_d, D)], axis=0)
    new_p, new_s = [], []
    for l in range(depth):
        pr = _project(x, w_in[l], b_in[l])
        gate_rows = pr['gate_rows']
        gr_p = jnp.transpose(gate_rows[:2 * H, :n_p].reshape(2 * H, B * nc, L), (1, 0, 2))
        gr_d = jnp.transpose(gate_rows[:2 * H, n_p:].reshape(2 * H, Bd, Td), (1, 0, 2))
        iw_p = gate_rows[2 * H:, :n_p]
        iw_d = gate_rows[2 * H:, n_p:]

        zc = jnp.zeros((B, H, dh, dh), F32)
        zn = jnp.zeros((B, H, dh), F32)
        zm = jnp.full((B, H), -jnp.inf, F32)
        ml = functools.partial(_mlstm, pr['a_q'], pr['a_k'], pr['a_v'], pr['a_o'])
        ha_p, c_p, nn_p, m_p = ml(gr_p, pr['gate_cols'], b_fgate[l], a_norm_g[l], zc, zn, zm,
                                  row0=0, n_seq=B, n_chunks=nc, L=L)
        ha_d, c_d, nn_d, m_d = ml(gr_d, pr['gate_cols'], b_fgate[l], a_norm_g[l], state_C[l], state_n[l], state_m[l],
                                  row0=n_p, n_seq=Bd, n_chunks=1, L=Td)

        bk, bv, bik = pr['b_k'], pr['b_v'], pr['b_ik']
        hb_p = _dsa_prompt(pr['b_q'][:n_p], pr['b_iq'][:n_p], iw_p, bk[:n_p], bv[:n_p], bik[:n_p], B=B, T=T)
        hb_d = _dsa_decode(pr['b_q'][n_p:], pr['b_iq'][n_p:], iw_d, bk[n_p:], bv[n_p:], bik[n_p:],
                           cache_kt, cache_vt, cache_kit, page_table, layer=l, Bd=Bd, Tq=Td)

        cu = pr['c_u']
        cu_p = cu[:n_p].reshape(B, T, C_WIDTH)
        cu_d = cu[n_p:].reshape(Bd, Td, C_WIDTH)
        zero_halo = jnp.zeros((B, POOL_HALO, C_WIDTH), F32)
        hc_p = _pool(cu_p, cu_p, zero_halo, c_map[l], c_scale[l], Tb=min(T, 512), pos0=0)
        halo_d = jnp.pad(state_pool[l], ((0, 0), (POOL_HALO - POOL_BUF, 0), (0, 0)))
        hc_d = _pool(cu_d, halo_d, halo_d, c_map[l], c_scale[l], Tb=Td, pos0=past_len)

        h_a = jnp.concatenate([ha_p, ha_d], axis=0)
        h_b = jnp.concatenate([hb_p, hb_d], axis=0)
        h_c = jnp.concatenate([hc_p.reshape(n_p, C_WIDTH), hc_d.reshape(n_d, C_WIDTH)], axis=0)
        x1 = _merge(x, h_a, h_b, h_c, pr['g_a'], pr['g_b'], pr['g_c'], p_a[l], p_b[l], p_c[l], w_out[l],
                    ln1_g[l], ln1_b[l], alpha=alpha)
        gates = _router(x1, router_w[l], router_b[l])
        x = _moe(x1, gates, exp_gu_b, exp_down_b, sh_gu[l].astype(BF16), sh_down[l].astype(BF16),
                 ln2_g[l], ln2_b[l], layer=l, alpha=alpha, tm=_moe_tile(n))

        kvs = (B_KV_HEADS, B_HEAD_DIM)
        pool_d = jnp.concatenate([state_pool[l], cu_d], axis=1)[:, -POOL_BUF:]
        new_p.append((bk[:n_p].reshape(B, T, *kvs), bv[:n_p].reshape(B, T, *kvs), bik[:n_p].reshape(B, T, IDX_DIM),
                      c_p, nn_p, m_p, cu_p[:, T - POOL_BUF:]))
        new_s.append((bk[n_p:].reshape(Bd, Td, *kvs), bv[n_p:].reshape(Bd, Td, *kvs),
                      bik[n_p:].reshape(Bd, Td, IDX_DIM), c_d, nn_d, m_d, pool_d))

    def stacked(states, i):
        return jnp.stack([s[i] for s in states])

    outs = [x[:n_p].reshape(B, T, D), x[n_p:].reshape(Bd, Td, D)]
    for i in range(7):
        outs += [stacked(new_p, i), stacked(new_s, i)]
    return tuple(outs)
```

```python
import functools
import math

import jax
import jax.numpy as jnp
from jax import lax
from jax.experimental import pallas as pl
from jax.experimental.pallas import tpu as pltpu

F32 = jnp.float32
BF16 = jnp.bfloat16
I32 = jnp.int32

A_HEADS = 4
A_HEAD_DIM = 128
A_WIDTH = A_HEADS * A_HEAD_DIM
MLSTM_CHUNK = 128
B_HEADS = 8
B_KV_HEADS = 2
B_HEAD_DIM = 64
B_WIDTH = B_HEADS * B_HEAD_DIM
B_KV_WIDTH = B_KV_HEADS * B_HEAD_DIM
B_GROUP = B_HEADS // B_KV_HEADS
IDX_HEADS = 8
IDX_DIM = 64
MAX_KEEP = 256
QUERY_BLOCK = 128
C_WINDOWS = (2, 4, 8, 16)
C_GROUPS = len(C_WINDOWS)
C_GROUP_WIDTH = 128
C_WIDTH = C_GROUPS * C_GROUP_WIDTH
POOL_BUF = max(C_WINDOWS) - 1
N_EXPERTS = 64
TOP_K = 8
N_EXPERT_GROUPS = 8
TOPK_GROUPS = 4
EXPERT_DIM = 256
ROUTED_SCALE = 2.5
LN_EPS = 1e-5

LANES = 128
SUBLANES = 8
VMEM_LIMIT = 56 * 1024 * 1024

INT_MIN = -2 ** 31
NEG_BIG = -1e30

MLSTM_SEQS_PER_STEP = 1
DSA_UNROLL = 4
DECODE_PAGES_PER_STEP = 8
PROJ_TM = 256
MERGE_TM = 256
ROUTER_TM = 256
POOL_HALO = 16


def _cparams(sem):
    return pltpu.CompilerParams(dimension_semantics=sem, vmem_limit_bytes=VMEM_LIMIT)


def _dot(a, b):
    return jnp.dot(a, b, preferred_element_type=F32)


def _dot_nt(a, b):
    return lax.dot_general(a, b, (((1,), (1,)), ((), ())), preferred_element_type=F32)


def _dot_tn(a, b):
    return lax.dot_general(a, b, (((0,), (0,)), ((), ())), preferred_element_type=F32)


def _sigmoid(x):
    return 1.0 / (1.0 + jnp.exp(-x))


def _log_sigmoid(x):
    return jnp.minimum(x, 0.0) - jnp.log1p(jnp.exp(-jnp.abs(x)))


def _layer_norm(x, g, b):
    mu = jnp.mean(x, axis=-1, keepdims=True)
    xc = x - mu
    var = jnp.mean(xc * xc, axis=-1, keepdims=True)
    return xc * lax.rsqrt(var + LN_EPS) * g + b


def _bf16_round(x):
    return x.astype(BF16).astype(F32)


def _exact_pow2(scale):
    m = float(scale)
    assert math.frexp(m)[0] == 0.5, m
    return jnp.asarray(m, BF16)


def _sortable(s):
    bits = lax.bitcast_convert_type(s, I32)
    return jnp.where(bits < 0, bits ^ jnp.int32(0x7FFFFFFF), bits)


_PROJ_PLAIN = ('a_q', 'a_k', 'a_v', 'a_o', 'b_q', 'b_k', 'b_v', 'b_iq', 'b_ik', 'c_u', 'g_a', 'g_b', 'g_c')


def _proj_kernel(*refs):
    n_plain = len(_PROJ_PLAIN)
    x_ref = refs[0]
    w_refs = refs[1:1 + n_plain]
    b_refs = refs[1 + n_plain:1 + 2 * n_plain]
    wt, bt, wc, bc = refs[1 + 2 * n_plain:5 + 2 * n_plain]
    outs = refs[5 + 2 * n_plain:]
    xb = x_ref[...].astype(BF16)
    for w_ref, b_ref, o_ref in zip(w_refs, b_refs, outs[:n_plain]):
        o_ref[...] = _dot(xb, w_ref[...]) + b_ref[...]
    outs[n_plain][...] = _dot_nt(wt[...], xb) + bt[...]
    outs[n_plain + 1][...] = _dot(xb, wc[...]) + bc[...]


def _project(x, w_in, b_in):
    n, d = x.shape
    widths = (A_WIDTH, A_WIDTH, A_WIDTH, A_HEADS, A_HEADS, A_WIDTH,
              B_WIDTH, B_KV_WIDTH, B_KV_WIDTH, IDX_HEADS * IDX_DIM, IDX_DIM, IDX_HEADS,
              C_WIDTH, d, d, d)
    names = ('a_q', 'a_k', 'a_v', 'a_i', 'a_f', 'a_o', 'b_q', 'b_k', 'b_v', 'b_iq', 'b_ik', 'b_iw',
             'c_u', 'g_a', 'g_b', 'g_c')
    off, acc = {}, 0
    for nm, w in zip(names, widths):
        off[nm] = (acc, w)
        acc += w

    def cols(nm):
        s, w = off[nm]
        return w_in[:, s:s + w], b_in[s:s + w]

    ws, bs = [], []
    for nm in _PROJ_PLAIN:
        w, b = cols(nm)
        ws.append(w.astype(BF16))
        bs.append(b.reshape(1, -1))
    wi, bi = cols('a_i')
    wf, bf = cols('a_f')
    ww, bw = cols('b_iw')
    n_rows = 2 * A_HEADS + IDX_HEADS
    w_t = jnp.concatenate([wi, wf, ww], axis=1).T.astype(BF16)
    b_t = jnp.concatenate([bi, bf, bw]).reshape(n_rows, 1)
    w_c = jnp.pad(jnp.concatenate([wi, wf], axis=1), ((0, 0), (0, LANES - 2 * A_HEADS))).astype(BF16)
    b_c = jnp.pad(jnp.concatenate([bi, bf]), (0, LANES - 2 * A_HEADS)).reshape(1, LANES)

    tm = PROJ_TM
    assert n % tm == 0
    const = lambda i: (0, 0)
    in_specs = [pl.BlockSpec((tm, d), lambda i: (i, 0))]
    in_specs += [pl.BlockSpec(w.shape, const) for w in ws]
    in_specs += [pl.BlockSpec(b.shape, const) for b in bs]
    in_specs += [pl.BlockSpec(a.shape, const) for a in (w_t, b_t, w_c, b_c)]
    out_shape = [jax.ShapeDtypeStruct((n, w.shape[1]), F32) for w in ws]
    out_specs = [pl.BlockSpec((tm, w.shape[1]), lambda i: (i, 0)) for w in ws]
    out_shape += [jax.ShapeDtypeStruct((n_rows, n), F32), jax.ShapeDtypeStruct((n, LANES), F32)]
    out_specs += [pl.BlockSpec((n_rows, tm), lambda i: (0, i)), pl.BlockSpec((tm, LANES), lambda i: (i, 0))]
    res = pl.pallas_call(
        _proj_kernel, grid=(n // tm,), in_specs=in_specs, out_specs=out_specs, out_shape=out_shape,
        compiler_params=_cparams(("parallel",)),
    )(x, *ws, *bs, w_t, b_t, w_c, b_c)
    out = dict(zip(_PROJ_PLAIN, res[:len(_PROJ_PLAIN)]))
    out['gate_rows'] = res[-2]
    out['gate_cols'] = res[-1]
    return out


def _mlstm_kernel(*refs, L, S):
    per_seq = [refs[6 * s:6 * s + 6] for s in range(S)]
    br_ref, bc_ref, ng_ref, c0_ref, n0_ref, m0_ref, h_ref, c_out, n_out, m_out, c_s, n_s, m_s = refs[6 * S:]
    c = pl.program_id(1)
    nc = pl.num_programs(1)

    @pl.when(c == 0)
    def _():
        c_s[...] = c0_ref[...]
        n_s[...] = n0_ref[...]
        m_s[...] = m0_ref[...]

    row = lax.broadcasted_iota(I32, (L, L), 0)
    col = lax.broadcasted_iota(I32, (L, L), 1)
    tril = row >= col
    triu = row <= col
    k_scale = A_HEAD_DIM ** -0.5
    for s, h in [(s, h) for s in range(S) for h in range(A_HEADS)]:
        q_ref, k_ref, v_ref, gr_ref, gc_ref, ao_ref = per_seq[s]
        gr = gr_ref[0]
        gc = gc_ref[...]
        lf_rows = _log_sigmoid(gr + br_ref[...])
        lf_cols = _log_sigmoid(gc + bc_ref[...])
        hs = slice(h * A_HEAD_DIM, (h + 1) * A_HEAD_DIM)
        q = q_ref[:, hs]
        k = k_ref[:, hs] * k_scale
        v = v_ref[:, hs]
        li_r = gr[h:h + 1, :]
        lf_r = lf_rows[A_HEADS + h:A_HEADS + h + 1, :]
        li_c = gc[:, h:h + 1]
        lf_c = lf_cols[:, A_HEADS + h:A_HEADS + h + 1]
        b_c = jnp.sum(jnp.where(tril, lf_r, 0.0), axis=-1, keepdims=True)
        b_r = jnp.sum(jnp.where(triu, lf_c, 0.0), axis=0, keepdims=True)
        m_prev = m_s[s, h]
        c_prev = c_s[s, h]
        n_prev = n_s[s, h]
        d_log = jnp.where(tril, b_c - b_r + li_r, -jnp.inf)
        inter = b_c + m_prev
        m_t = jnp.maximum(inter, jnp.max(d_log, axis=-1, keepdims=True))
        w_inter = jnp.exp(inter - m_t)
        qb = q.astype(BF16)
        kb = k.astype(BF16)
        vb = v.astype(BF16)
        a = jnp.exp(d_log - m_t) * _dot_nt(qb, kb)
        num = w_inter * _dot(qb, c_prev.astype(BF16)) + _dot(a.astype(BF16), vb)
        qn = jnp.sum(qb.astype(F32) * _bf16_round(n_prev), axis=-1, keepdims=True)
        den = w_inter * qn + jnp.sum(a, axis=-1, keepdims=True)
        hh = num / jnp.maximum(jnp.abs(den), jnp.exp(-m_t))
        m_new = m_t[L - 1:L, :]
        b_last = b_c[L - 1:L, :]
        w_c = jnp.exp(b_last - b_c + li_c - m_new)
        decay = jnp.exp(b_last + m_prev - m_new)
        kw = k * w_c
        c_s[s, h] = decay * c_prev + _dot_tn(kw.astype(BF16), vb)
        n_s[s, h] = decay * n_prev + jnp.sum(kb.astype(F32) * _bf16_round(w_c), axis=0, keepdims=True)
        m_s[s, h] = m_new
        mu = jnp.mean(hh, axis=-1, keepdims=True)
        hc = hh - mu
        var = jnp.mean(hc * hc, axis=-1, keepdims=True)
        hn = hc * lax.rsqrt(var + LN_EPS)
        h_ref[s, :, hs] = _sigmoid(ao_ref[:, hs]) * hn * ng_ref[:, hs]

    @pl.when(c == nc - 1)
    def _():
        c_out[...] = c_s[...]
        n_out[...] = n_s[...]
        m_out[...] = m_s[...]


def _mlstm(a_q, a_k, a_v, a_o, gate_rows, gate_cols, b_fgate, a_norm_g, c0, n0, m0, *, row0, n_seq, n_chunks, L):
    assert row0 % L == 0
    rb0 = row0 // L
    n_rows = n_seq * n_chunks * L
    H, dh = A_HEADS, A_HEAD_DIM
    S = MLSTM_SEQS_PER_STEP if n_seq % MLSTM_SEQS_PER_STEP == 0 else 1
    st = lambda b, c: (b, 0, 0, 0)
    const = lambda b, c: (0, 0)
    bias_r = jnp.pad(b_fgate, (A_HEADS, 0)).reshape(2 * A_HEADS, 1)
    bias_c = jnp.pad(b_fgate, (A_HEADS, LANES - 2 * A_HEADS)).reshape(1, LANES)
    seq_specs, seq_args = [], []
    for s in range(S):
        tok = lambda b, c, s=s: (rb0 + (b * S + s) * n_chunks + c, 0)
        seq_specs += [pl.BlockSpec((L, A_WIDTH), tok)] * 3
        seq_specs += [pl.BlockSpec((1, 2 * A_HEADS, L), lambda b, c, s=s: ((b * S + s) * n_chunks + c, 0, 0)),
                      pl.BlockSpec((L, LANES), tok),
                      pl.BlockSpec((L, A_WIDTH), tok)]
        seq_args += [a_q, a_k, a_v, gate_rows, gate_cols, a_o]
    outs = pl.pallas_call(
        functools.partial(_mlstm_kernel, L=L, S=S),
        grid=(n_seq // S, n_chunks),
        in_specs=seq_specs
        + [pl.BlockSpec((2 * A_HEADS, 1), const),
           pl.BlockSpec((1, LANES), const),
           pl.BlockSpec((1, A_WIDTH), const),
           pl.BlockSpec((S, H, dh, dh), st),
           pl.BlockSpec((S, H, 1, dh), st),
           pl.BlockSpec((S, H, 1, 1), st)],
        out_specs=[pl.BlockSpec((S, L, A_WIDTH), lambda b, c: (b, c, 0)),
                   pl.BlockSpec((S, H, dh, dh), st),
                   pl.BlockSpec((S, H, 1, dh), st),
                   pl.BlockSpec((S, H, 1, 1), st)],
        out_shape=[jax.ShapeDtypeStruct((n_seq, n_chunks * L, A_WIDTH), F32),
                   jax.ShapeDtypeStruct((n_seq, H, dh, dh), F32),
                   jax.ShapeDtypeStruct((n_seq, H, 1, dh), F32),
                   jax.ShapeDtypeStruct((n_seq, H, 1, 1), F32)],
        scratch_shapes=[pltpu.VMEM((S, H, dh, dh), F32), pltpu.VMEM((S, H, 1, dh), F32), pltpu.VMEM((S, H, 1, 1), F32)],
        compiler_params=_cparams(("parallel", "arbitrary")),
    )(*seq_args, bias_r, bias_c, a_norm_g.reshape(1, A_WIDTH),
      c0, n0.reshape(n_seq, H, 1, dh), m0.reshape(n_seq, H, 1, 1))
    h, c_new, n_new, m_new = outs
    return h.reshape(n_rows, A_WIDTH), c_new, n_new.reshape(n_seq, H, dh), m_new.reshape(n_seq, H)


def _threshold_search(count_ge, n_keep, shape, n_total):
    def body(j, carry):
        t, cnt_t = carry
        cand = t ^ (jnp.int32(1) << (31 - j))
        cnt = count_ge(cand)
        ok = cnt >= n_keep
        return jnp.where(ok, cand, t), jnp.where(ok, cnt, cnt_t)
    return lax.fori_loop(0, 32, body, (jnp.full(shape, INT_MIN, I32), jnp.full(shape, n_total, F32)))


def _tree_sum(parts):
    parts = list(parts)
    while len(parts) > 1:
        parts = [parts[i] + parts[i + 1] for i in range(0, len(parts) - 1, 2)] + parts[len(parts) & ~1:]
    return parts[0]


def _dsa_prompt_kernel(qi_ref, w_ref, ki_ref, q_ref, k_ref, vt_ref, o_ref, s_ref, acc_ref, m_ref, l_ref,
                       bias_s, lg_s, p_s, *, n_keep):
    QB = QUERY_BLOCK
    U = DSA_UNROLL
    SK = U * QB
    PW = 2 * QB
    i = pl.program_id(1)
    n_trips = (i + U) // U
    key_minus_qry = lax.broadcasted_iota(I32, (QB, QB), 0) - lax.broadcasted_iota(I32, (QB, QB), 1)
    qi = qi_ref[0, 0]
    w = _bf16_round(w_ref[0, 0] * (IDX_HEADS ** -0.5))

    def score_body(t, carry):
        for u in range(U):
            c = t * U + u
            ki = ki_ref[0, pl.ds(pl.multiple_of(c * QB, QB), QB), :]
            s = jnp.zeros((QB, QB), F32)
            for hp in range(IDX_HEADS // 2):
                d = _dot(ki, qi[:, hp * PW:(hp + 1) * PW])
                r = jnp.maximum(d.astype(BF16), 0).astype(F32)
                for h in (2 * hp, 2 * hp + 1):
                    s = s + r[:, (h % 2) * QB:(h % 2 + 1) * QB] * w[h:h + 1, :]
            visible = key_minus_qry <= (i - c) * QB
            s_ref[c] = jnp.where(visible, _sortable(s), INT_MIN)
        return carry

    lax.fori_loop(0, n_trips, score_body, 0)

    def count(pred):
        def body(t, acc):
            parts = []
            for u in range(U):
                hit = jnp.where(pred(s_ref[t * U + u]), 1.0, 0.0)
                parts += [hit[r * SUBLANES:(r + 1) * SUBLANES, :] for r in range(QB // SUBLANES)]
            return acc + _tree_sum(parts)
        acc = lax.fori_loop(0, n_trips, body, jnp.zeros((SUBLANES, QB), F32))
        return jnp.sum(acc, axis=0, keepdims=True)

    keep = float(n_keep)
    n_total = (n_trips * SK).astype(F32)
    thr, cnt_thr = _threshold_search(lambda cand: count(lambda blk: blk >= cand), keep, (1, QB), n_total)
    has_ties = jnp.max(jnp.where((thr != INT_MIN) & (cnt_thr > keep), 1.0, 0.0)) > 0.5

    m_ref[...] = jnp.full(m_ref.shape, NEG_BIG, F32)
    l_ref[...] = jnp.zeros(l_ref.shape, F32)
    acc_ref[...] = jnp.zeros(acc_ref.shape, F32)
    q = q_ref[0, 0]
    pairs_per_group = B_GROUP // 2

    def fold(x, op):
        parts = [x[r * SUBLANES:(r + 1) * SUBLANES, :] for r in range(QB // SUBLANES)]
        while len(parts) > 1:
            parts = [op(parts[a], parts[a + 1]) for a in range(0, len(parts), 2)]
        return parts[0]

    n_pairs = B_HEADS // 2
    halves = [slice(e * QB, (e + 1) * QB) for e in range(2)]

    def logits_phase(t, slot):
        off = pl.multiple_of(t * SK, SK)
        mx = [[None, None] for _ in range(n_pairs)]
        for u in range(U):
            bias = bias_s[u]
            for g in range(B_KV_HEADS):
                kc = k_ref[0, g, pl.ds(off + u * QB, QB), :]
                for hp in range(g * pairs_per_group, (g + 1) * pairs_per_group):
                    lg = _dot(kc, q[:, hp * PW:(hp + 1) * PW])
                    for e, cs in enumerate(halves):
                        piece = lg[:, cs] + bias
                        lg_s[slot, hp, u, :, cs] = piece
                        pm = fold(piece, jnp.maximum)
                        mx[hp][e] = pm if mx[hp][e] is None else jnp.maximum(mx[hp][e], pm)
        return tuple(jnp.concatenate([jnp.max(mx[hp][e], axis=0, keepdims=True) for e in range(2)], axis=1)
                     for hp in range(n_pairs))

    def values_phase(t, slot, mx):
        off = pl.multiple_of(t * SK, SK)
        alpha = []
        for hp in range(n_pairs):
            m_old = m_ref[hp]
            m_new = jnp.maximum(m_old, mx[hp])
            alpha.append(jnp.exp(m_old - m_new))
            m_ref[hp] = m_new
            ls = [None, None]
            for u in range(U):
                for e, cs in enumerate(halves):
                    p = jnp.exp(lg_s[slot, hp, u, :, cs] - m_new[:, cs])
                    p_s[hp, u * QB:(u + 1) * QB, cs] = p.astype(BF16)
                    ps = fold(p, jnp.add)
                    ls[e] = ps if ls[e] is None else ls[e] + ps
            l_new = jnp.concatenate([jnp.sum(ls[e], axis=0, keepdims=True) for e in range(2)], axis=1)
            l_ref[hp] = alpha[hp] * l_ref[hp] + l_new
        for g in range(B_KV_HEADS):
            vt = vt_ref[0, g, :, pl.ds(off, SK)]
            for hp in range(g * pairs_per_group, (g + 1) * pairs_per_group):
                acc_ref[hp] = alpha[hp] * acc_ref[hp] + _dot(vt, p_s[hp])

    @pl.when(jnp.logical_not(has_ties))
    def _():
        thr_eff = jnp.maximum(thr, INT_MIN + 1)

        def select(t):
            for u in range(U):
                bias_s[u] = jnp.where(s_ref[t * U + u] >= thr_eff, 0.0, NEG_BIG)

        select(0)
        mx0 = logits_phase(0, 0)

        def body(t, mx):
            values_phase(t, 0, mx)
            nxt = jnp.minimum(t + 1, n_trips - 1)
            select(nxt)
            return logits_phase(nxt, 0)

        lax.fori_loop(0, n_trips, body, mx0)

    @pl.when(has_ties)
    def _():
        n_ties = keep - count(lambda blk: blk > thr)
        tri = (lax.broadcasted_iota(I32, (SK, SK), 0) >= lax.broadcasted_iota(I32, (SK, SK), 1)).astype(BF16)

        def body(t, run):
            blk = s_ref[pl.ds(t * U, U)].reshape(SK, QB)
            tie = (blk == thr) & (blk != INT_MIN)
            prefix = _dot(tri, jnp.where(tie, 1.0, 0.0).astype(BF16))
            sel = (blk > thr) | (tie & (run + prefix <= n_ties))
            bias_s[...] = jnp.where(sel, 0.0, NEG_BIG).reshape(U, QB, QB)
            values_phase(t, 0, logits_phase(t, 0))
            return run + prefix[SK - 1:SK, :]

        lax.fori_loop(0, n_trips, body, jnp.zeros((1, QB), F32))

    for hp in range(B_HEADS // 2):
        g, pr = divmod(hp, pairs_per_group)
        o_ref[0, 0, g, :, pr * PW:(pr + 1) * PW] = acc_ref[hp] / l_ref[hp]


def _dsa_prompt(b_q, b_iq, iw_rows, b_k, b_v, b_ik, *, B, T):
    QB = QUERY_BLOCK
    nb = T // QB
    n_keep = min(MAX_KEEP, T // 4)

    assert nb % DSA_UNROLL == 0

    def head_t(a, heads, dim):
        a = a.reshape(B, nb, QB, heads, dim).astype(BF16) * _exact_pow2(dim ** -0.5)
        return jnp.transpose(a, (0, 1, 4, 3, 2)).reshape(B, nb, dim, heads * QB)

    qi_t = head_t(b_iq, IDX_HEADS, IDX_DIM)
    q_t = head_t(b_q, B_HEADS, B_HEAD_DIM)
    w_t = jnp.transpose(iw_rows.reshape(IDX_HEADS, B, nb, QB), (1, 2, 0, 3))
    ki = b_ik.reshape(B, T, IDX_DIM).astype(BF16)
    kh = jnp.transpose(b_k.reshape(B, T, B_KV_HEADS, B_HEAD_DIM), (0, 2, 1, 3)).astype(BF16)
    vt = jnp.transpose(b_v.reshape(B, T, B_KV_HEADS, B_HEAD_DIM), (0, 2, 3, 1)).astype(BF16)
    GW = B_GROUP * QB
    n_pairs = B_HEADS // 2
    blk = lambda b, i: (b, i, 0, 0)
    out = pl.pallas_call(
        functools.partial(_dsa_prompt_kernel, n_keep=n_keep),
        grid=(B, nb),
        in_specs=[pl.BlockSpec((1, 1, IDX_DIM, IDX_HEADS * QB), blk),
                  pl.BlockSpec((1, 1, IDX_HEADS, QB), blk),
                  pl.BlockSpec((1, T, IDX_DIM), lambda b, i: (b, 0, 0)),
                  pl.BlockSpec((1, 1, B_HEAD_DIM, B_HEADS * QB), blk),
                  pl.BlockSpec((1, B_KV_HEADS, T, B_HEAD_DIM), lambda b, i: (b, 0, 0, 0)),
                  pl.BlockSpec((1, B_KV_HEADS, B_HEAD_DIM, T), lambda b, i: (b, 0, 0, 0))],
        out_specs=pl.BlockSpec((1, 1, B_KV_HEADS, B_HEAD_DIM, GW), lambda b, i: (b, i, 0, 0, 0)),
        out_shape=jax.ShapeDtypeStruct((B, nb, B_KV_HEADS, B_HEAD_DIM, GW), F32),
        scratch_shapes=[pltpu.VMEM((nb, QB, QB), I32),
                        pltpu.VMEM((n_pairs, B_HEAD_DIM, 2 * QB), F32),
                        pltpu.VMEM((n_pairs, 1, 2 * QB), F32),
                        pltpu.VMEM((n_pairs, 1, 2 * QB), F32),
                        pltpu.VMEM((DSA_UNROLL, QB, QB), F32),
                        pltpu.VMEM((1, n_pairs, DSA_UNROLL, QB, 2 * QB), F32),
                        pltpu.VMEM((n_pairs, DSA_UNROLL * QB, 2 * QB), BF16)],
        compiler_params=_cparams(("parallel", "arbitrary")),
    )(qi_t, w_t, ki, q_t, kh, vt)
    out = out.reshape(B, nb, B_KV_HEADS, B_HEAD_DIM, B_GROUP, QB)
    return jnp.transpose(out, (0, 1, 5, 2, 4, 3)).reshape(B * T, B_WIDTH)


def _dsa_decode_kernel(pt_ref, qi_ref, w_ref, q_ref, *rest, n_keep, n_q, pps):
    ck_refs, cv_refs, cki_refs = rest[0:pps], rest[pps:2 * pps], rest[2 * pps:3 * pps]
    nk_ref, nv_ref, nki_ref, o_ref, s_ref, k_s, v_s = rest[3 * pps:]
    P = LANES
    p = pl.program_id(1)
    n_steps = pl.num_programs(1)
    n_rows = IDX_HEADS * n_q
    qi = qi_ref[0]
    w = _bf16_round(w_ref[0] * (IDX_HEADS ** -0.5))

    def chunk_scores(ki_t):
        t = _bf16_round(jnp.maximum(_dot(qi, ki_t), 0.0)) * w
        s = jnp.zeros((n_q, P), F32)
        for h in range(IDX_HEADS):
            s = s + t[h * n_q:(h + 1) * n_q, :]
        return s

    for j in range(pps):
        off = pl.multiple_of((p * pps + j) * P, P)
        s_ref[:, pl.ds(off, P)] = _sortable(chunk_scores(cki_refs[j][0, 0].astype(BF16)))
        k_s[:, pl.ds(off, P)] = ck_refs[j][0, 0].astype(BF16)
        v_s[:, pl.ds(off, P)] = cv_refs[j][0, 0].astype(BF16)

    @pl.when(p == n_steps - 1)
    def _():
        n_keys = s_ref.shape[1]
        past = n_keys - P
        n_chunks = n_keys // P
        key_j = lax.broadcasted_iota(I32, (n_q, P), 1)
        qry_t = lax.broadcasted_iota(I32, (n_q, P), 0)
        visible = key_j <= qry_t
        s_ref[:, pl.ds(past, P)] = jnp.where(visible, _sortable(chunk_scores(nki_ref[0].astype(BF16))), INT_MIN)
        k_s[:, pl.ds(past, P)] = nk_ref[0].astype(BF16)
        v_s[:, pl.ds(past, P)] = nv_ref[0].astype(BF16)

        def count(pred):
            hit = jnp.where(pred(s_ref[...]), 1.0, 0.0)
            return jnp.sum(_tree_sum([hit[:, c * P:(c + 1) * P] for c in range(n_chunks)]), axis=-1, keepdims=True)

        keep = float(n_keep)
        thr, cnt_thr = _threshold_search(lambda cand: count(lambda blk: blk >= cand), keep, (n_q, 1), float(n_keys))
        has_ties = jnp.max(jnp.where((thr != INT_MIN) & (cnt_thr > keep), 1.0, 0.0)) > 0.5

        @pl.when(jnp.logical_not(has_ties))
        def _():
            thr_eff = jnp.maximum(thr, INT_MIN + 1)
            s_ref[...] = jnp.where(s_ref[...] >= thr_eff, 1, 0).astype(I32)

        @pl.when(has_ties)
        def _():
            n_ties = keep - count(lambda blk: blk > thr)
            tri = (lax.broadcasted_iota(I32, (P, P), 0) <= lax.broadcasted_iota(I32, (P, P), 1)).astype(BF16)

            def sel_body(c, run):
                o = pl.multiple_of(c * P, P)
                blk = s_ref[:, pl.ds(o, P)]
                tie = (blk == thr) & (blk != INT_MIN)
                prefix = _dot(jnp.where(tie, 1.0, 0.0).astype(BF16), tri)
                sel = (blk > thr) | (tie & (run + prefix <= n_ties))
                s_ref[:, pl.ds(o, P)] = jnp.where(sel, 1, 0).astype(I32)
                return run + prefix[:, P - 1:P]

            lax.fori_loop(0, n_chunks, sel_body, jnp.zeros((n_q, 1), F32))

        sel = s_ref[...] > 0
        q = q_ref[0]
        lg = _dot(q, k_s[...])
        L = lg.shape[1]
        lg = jnp.where(sel[None], lg.reshape(B_HEADS, n_q, L), NEG_BIG)
        m = jnp.max(lg, axis=-1, keepdims=True)
        pr = jnp.exp(lg - m)
        pr = pr / jnp.sum(pr, axis=-1, keepdims=True)
        o_ref[0] = _dot_nt(pr.reshape(n_rows, L).astype(BF16), v_s[...])


def _dsa_decode(b_q, b_iq, iw_rows, b_k, b_v, b_ik, cache_kt, cache_vt, cache_kit, page_table, *, layer, Bd, Tq):
    page = cache_kt.shape[-1]
    assert page == LANES
    n_pages = page_table.shape[1]
    past = n_pages * page
    n_keep = min(MAX_KEEP, (past + Tq) // 4)
    n_rows = B_HEADS * Tq
    kvw = B_KV_HEADS * B_HEAD_DIM

    def rows_hq(a, heads, dim):
        return jnp.transpose(a.reshape(Bd, Tq, heads, dim), (0, 2, 1, 3)).reshape(Bd, heads * Tq, dim)

    qi = rows_hq(b_iq, IDX_HEADS, IDX_DIM).astype(BF16) * _exact_pow2(IDX_DIM ** -0.5)
    w = jnp.transpose(iw_rows.reshape(IDX_HEADS, Bd, Tq), (1, 0, 2)).reshape(Bd, IDX_HEADS * Tq, 1)
    qh = rows_hq(b_q, B_HEADS, B_HEAD_DIM).astype(BF16) * _exact_pow2(B_HEAD_DIM ** -0.5)
    group = (jnp.arange(n_rows) // Tq) // B_GROUP
    lane_group = jnp.arange(kvw) // B_HEAD_DIM
    q_wide = jnp.where(group[:, None] == lane_group[None, :], jnp.tile(qh, (1, 1, B_KV_HEADS)), 0.0).astype(BF16)
    pps = DECODE_PAGES_PER_STEP if n_pages % DECODE_PAGES_PER_STEP == 0 else 1

    def pad_new(a, width):
        return jnp.pad(jnp.transpose(a.reshape(Bd, Tq, width), (0, 2, 1)), ((0, 0), (0, 0), (0, page - Tq)))

    nk, nv, nki = pad_new(b_k, kvw), pad_new(b_v, kvw), pad_new(b_ik, IDX_DIM)
    n_keys = past + page
    seq = lambda b, p, pt: (b, 0, 0)
    pages = lambda width: [pl.BlockSpec((1, 1, width, page), lambda b, p, pt, j=j: (layer, pt[b, p * pps + j], 0, 0))
                           for j in range(pps)]
    out = pl.pallas_call(
        functools.partial(_dsa_decode_kernel, n_keep=n_keep, n_q=Tq, pps=pps),
        grid_spec=pltpu.PrefetchScalarGridSpec(
            num_scalar_prefetch=1,
            grid=(Bd, n_pages // pps),
            in_specs=[pl.BlockSpec((1, n_rows, IDX_DIM), seq),
                      pl.BlockSpec((1, n_rows, 1), seq),
                      pl.BlockSpec((1, n_rows, kvw), seq)]
            + pages(kvw) + pages(kvw) + pages(IDX_DIM)
            + [pl.BlockSpec((1, kvw, page), seq),
               pl.BlockSpec((1, kvw, page), seq),
               pl.BlockSpec((1, IDX_DIM, page), seq)],
            out_specs=pl.BlockSpec((1, n_rows, kvw), seq),
            scratch_shapes=[pltpu.VMEM((Tq, n_keys), I32),
                            pltpu.VMEM((kvw, n_keys), BF16),
                            pltpu.VMEM((kvw, n_keys), BF16)]),
        out_shape=jax.ShapeDtypeStruct((Bd, n_rows, kvw), F32),
        compiler_params=_cparams(("parallel", "arbitrary")),
    )(page_table, qi, w, q_wide, *([cache_kt] * pps), *([cache_vt] * pps), *([cache_kit] * pps), nk, nv, nki)
    out = out.reshape(Bd, B_HEADS, Tq, B_KV_HEADS, B_HEAD_DIM)
    out = jnp.concatenate([out[:, g * B_GROUP:(g + 1) * B_GROUP, :, g] for g in range(B_KV_HEADS)], axis=1)
    return jnp.transpose(out, (0, 2, 1, 3)).reshape(Bd * Tq, B_WIDTH)


def _pool_kernel(u_ref, prev_ref, halo0_ref, cmap_ref, cscale_ref, o_ref, ext, *, Tb, pos0):
    j = pl.program_id(1)
    H = POOL_HALO
    ext[0:H, :] = jnp.where(j == 0, halo0_ref[0], prev_ref[0])
    ext[H:H + Tb, :] = u_ref[0]
    pos = pos0 + j * Tb + lax.broadcasted_iota(I32, (Tb, 1), 0)
    for g, wdw in enumerate(C_WINDOWS):
        ls = slice(g * C_GROUP_WIDTH, (g + 1) * C_GROUP_WIDTH)
        win = ext[H:H + Tb, ls]
        for d in range(1, wdw):
            win = win + ext[H - d:H - d + Tb, ls]
        cnt = jnp.minimum(wdw, pos + 1).astype(F32)
        pooled = win / cnt - ext[H:H + Tb, ls]
        z = _dot(pooled.astype(BF16), cmap_ref[g])
        o_ref[0, :, ls] = z * cscale_ref[:, ls]


def _pool(u, prev_src, halo0, c_map, c_scale, *, Tb, pos0):
    Bn, R, C = u.shape
    H = POOL_HALO
    assert R % Tb == 0 and (Tb % H == 0 or R == Tb)
    prev_idx = (lambda b, j: (b, jnp.maximum(j * (Tb // H) - 1, 0), 0)) if R > Tb else (lambda b, j: (b, 0, 0))
    return pl.pallas_call(
        functools.partial(_pool_kernel, Tb=Tb, pos0=pos0),
        grid=(Bn, R // Tb),
        in_specs=[pl.BlockSpec((1, Tb, C), lambda b, j: (b, j, 0)),
                  pl.BlockSpec((1, H, C), prev_idx),
                  pl.BlockSpec((1, H, C), lambda b, j: (b, 0, 0)),
                  pl.BlockSpec(c_map.shape, lambda b, j: (0, 0, 0)),
                  pl.BlockSpec((1, C), lambda b, j: (0, 0))],
        out_specs=pl.BlockSpec((1, Tb, C), lambda b, j: (b, j, 0)),
        out_shape=jax.ShapeDtypeStruct((Bn, R, C), F32),
        scratch_shapes=[pltpu.VMEM((H + Tb, C), F32)],
        compiler_params=_cparams(("parallel", "parallel")),
    )(u, prev_src, halo0, c_map.astype(BF16), c_scale.reshape(1, C))


def _merge_kernel(x_ref, ha_ref, hb_ref, hc_ref, ga_ref, gb_ref, gc_ref, pa_ref, pb_ref, pc_ref, wo_ref,
                  g1_ref, b1_ref, o_ref, *, alpha):
    merged = (_sigmoid(ga_ref[...]) * _dot(ha_ref[...].astype(BF16), pa_ref[...])
              + _sigmoid(gb_ref[...]) * _dot(hb_ref[...].astype(BF16), pb_ref[...])
              + _sigmoid(gc_ref[...]) * _dot(hc_ref[...].astype(BF16), pc_ref[...]))
    y = alpha * x_ref[...] + _dot(merged.astype(BF16), wo_ref[...])
    o_ref[...] = _layer_norm(y, g1_ref[...], b1_ref[...])


def _merge(x, h_a, h_b, h_c, g_a, g_b, g_c, p_a, p_b, p_c, w_out, ln_g, ln_b, *, alpha):
    n, d = x.shape
    tm = MERGE_TM
    assert n % tm == 0
    tok = lambda w: pl.BlockSpec((tm, w), lambda i: (i, 0))
    full = lambda a: pl.BlockSpec(a.shape, lambda i: (0, 0))
    ws = [p_a.astype(BF16), p_b.astype(BF16), p_c.astype(BF16), w_out.astype(BF16), ln_g.reshape(1, d), ln_b.reshape(1, d)]
    return pl.pallas_call(
        functools.partial(_merge_kernel, alpha=alpha),
        grid=(n // tm,),
        in_specs=[tok(d), tok(h_a.shape[1]), tok(h_b.shape[1]), tok(h_c.shape[1]), tok(d), tok(d), tok(d)]
        + [full(a) for a in ws],
        out_specs=tok(d),
        out_shape=jax.ShapeDtypeStruct((n, d), F32),
        compiler_params=_cparams(("parallel",)),
    )(x, h_a, h_b, h_c, g_a, g_b, g_c, *ws)


def _router_kernel(x_ref, w_ref, rb_ref, o_ref):
    tm = x_ref.shape[0]
    G = N_EXPERT_GROUPS
    per_group = N_EXPERTS // G
    assert per_group == SUBLANES
    neg = -jnp.inf
    scores = _sigmoid(_dot_nt(w_ref[...], x_ref[...].astype(BF16)))
    biased = scores + rb_ref[...]
    sc = [scores[g * per_group:(g + 1) * per_group, :] for g in range(G)]
    bg = [biased[g * per_group:(g + 1) * per_group, :] for g in range(G)]
    sub = lax.broadcasted_iota(I32, (per_group, tm), 0).astype(F32)

    def colmax(a):
        return jnp.max(a, axis=0, keepdims=True)

    def first_in_group(a, m):
        return jnp.min(jnp.where(a == m, sub, float(per_group)), axis=0, keepdims=True)

    gscore = []
    for g in range(G):
        m1 = colmax(bg[g])
        rest = jnp.where(sub == first_in_group(bg[g], m1), neg, bg[g])
        gscore.append(m1 + colmax(rest))
    rem = []
    for g in range(G):
        beaten = jnp.zeros((1, tm), F32)
        for o in range(G):
            if o != g:
                wins = (gscore[o] > gscore[g]) | ((gscore[o] == gscore[g]) & (o < g))
                beaten = beaten + jnp.where(wins, 1.0, 0.0)
        rem.append(jnp.where(beaten < float(TOPK_GROUPS), bg[g], neg))
    chosen = [jnp.zeros((per_group, tm), jnp.bool_) for _ in range(G)]
    for _ in range(TOP_K):
        best = colmax(functools.reduce(jnp.maximum, rem))
        idx = functools.reduce(jnp.minimum, [
            jnp.min(jnp.where(rem[g] == best, sub + float(g * per_group), float(N_EXPERTS)), axis=0, keepdims=True)
            for g in range(G)])
        for g in range(G):
            hit = (sub + float(g * per_group)) == idx
            chosen[g] = chosen[g] | hit
            rem[g] = jnp.where(hit, neg, rem[g])
    s_sel = [jnp.where(chosen[g], sc[g], 0.0) for g in range(G)]
    total = jnp.sum(functools.reduce(jnp.add, s_sel), axis=0, keepdims=True)
    gates = jnp.concatenate([ROUTED_SCALE * s / total for s in s_sel]
                            + [jnp.zeros((LANES - N_EXPERTS, tm), F32)], axis=0)
    o_ref[...] = gates.T


def _router(x, router_w, router_b):
    n, d = x.shape
    tm = ROUTER_TM
    assert n % tm == 0
    w = router_w.T.astype(BF16)
    full = lambda a: pl.BlockSpec(a.shape, lambda i: (0, 0))
    rb = router_b.reshape(N_EXPERTS, 1)
    return pl.pallas_call(
        _router_kernel, grid=(n // tm,),
        in_specs=[pl.BlockSpec((tm, d), lambda i: (i, 0)), full(w), full(rb)],
        out_specs=pl.BlockSpec((tm, LANES), lambda i: (i, 0)),
        out_shape=jax.ShapeDtypeStruct((n, LANES), F32),
        compiler_params=_cparams(("parallel",)),
    )(x, w, rb)


def _moe_kernel(x_ref, g_ref, wgu_ref, wd_ref, sgu_ref, sd_ref, g2_ref, b2_ref, o_ref, xb_s, *, alpha):
    e = pl.program_id(1)
    n_e = pl.num_programs(1)

    def swiglu(w_gu, w_down):
        gu = _dot(xb_s[...], w_gu)
        gg = gu[:, :EXPERT_DIM]
        act = gg * _sigmoid(gg) * gu[:, EXPERT_DIM:]
        return _dot(act.astype(BF16), w_down)

    @pl.when(e == 0)
    def _():
        xb_s[...] = x_ref[...].astype(BF16)
        o_ref[...] = swiglu(sgu_ref[...], sd_ref[...])

    gates = g_ref[...]
    lane = lax.broadcasted_iota(I32, gates.shape, 1)
    gate = jnp.sum(jnp.where(lane == e, gates, 0.0), axis=-1, keepdims=True)
    y = swiglu(wgu_ref[0, 0], wd_ref[0, 0])
    o_ref[...] += jnp.where(gate != 0.0, y * gate, 0.0)

    @pl.when(e == n_e - 1)
    def _():
        o_ref[...] = _layer_norm(alpha * x_ref[...] + o_ref[...], g2_ref[...], b2_ref[...])


def _moe(x, gates, wgu, wd, sgu, sd, ln_g, ln_b, *, layer, alpha, tm):
    n, d = x.shape
    n_e = wgu.shape[1]
    assert n % tm == 0
    const = lambda i, e: (0, 0)
    return pl.pallas_call(
        functools.partial(_moe_kernel, alpha=alpha),
        grid=(n // tm, n_e),
        in_specs=[pl.BlockSpec((tm, d), lambda i, e: (i, 0)),
                  pl.BlockSpec((tm, LANES), lambda i, e: (i, 0)),
                  pl.BlockSpec((1, 1) + wgu.shape[2:], lambda i, e: (layer, e, 0, 0)),
                  pl.BlockSpec((1, 1) + wd.shape[2:], lambda i, e: (layer, e, 0, 0)),
                  pl.BlockSpec(sgu.shape, const),
                  pl.BlockSpec(sd.shape, const),
                  pl.BlockSpec((1, d), const),
                  pl.BlockSpec((1, d), const)],
        out_specs=pl.BlockSpec((tm, d), lambda i, e: (i, 0)),
        out_shape=jax.ShapeDtypeStruct((n, d), F32),
        scratch_shapes=[pltpu.VMEM((tm, d), BF16)],
        compiler_params=_cparams(("parallel", "arbitrary")),
    )(x, gates, wgu, wd, sgu, sd, ln_g.reshape(1, d), ln_b.reshape(1, d))


def _moe_tile(n):
    for cand in (1280, 1024, 512, 256, 128):
        if n % cand == 0:
            return cand
    return n


def kernel(x_prompt, x_sample, cache_k, cache_v, cache_kidx, state_C, state_n, state_m, state_pool, page_table, w_in, b_in, b_fgate, a_norm_g, c_map, c_scale, p_a, p_b, p_c, w_out, ln1_g, ln1_b, router_w, router_b, exp_gu, exp_down, sh_gu, sh_down, ln2_g, ln2_b):
    B, T, D = x_prompt.shape
    Bd, Td, _ = x_sample.shape
    depth = w_in.shape[0]
    alpha = (2 * depth) ** 0.25
    n_p, n_d = B * T, Bd * Td
    n = n_p + n_d
    L = MLSTM_CHUNK
    nc = T // L
    past_len = page_table.shape[1] * cache_k.shape[2]
    H, dh = A_HEADS, A_HEAD_DIM

    n_pool, page = cache_k.shape[1:3]
    cache_kt = jnp.transpose(cache_k, (0, 1, 3, 4, 2)).reshape(depth, n_pool, B_KV_WIDTH, page)
    cache_vt = jnp.transpose(cache_v, (0, 1, 3, 4, 2)).reshape(depth, n_pool, B_KV_WIDTH, page)
    cache_kit = jnp.transpose(cache_kidx, (0, 1, 3, 2))
    exp_gu_b = exp_gu.astype(BF16)
    exp_down_b = exp_down.astype(BF16)

    x = jnp.concatenate([x_prompt.reshape(n_p, D), x_sample.reshape(n_d, D)], axis=0)
    new_p, new_s = [], []
    for l in range(depth):
        pr = _project(x, w_in[l], b_in[l])
        gate_rows = pr['gate_rows']
        gr_p = jnp.transpose(gate_rows[:2 * H, :n_p].reshape(2 * H, B * nc, L), (1, 0, 2))
        gr_d = jnp.transpose(gate_rows[:2 * H, n_p:].reshape(2 * H, Bd, Td), (1, 0, 2))
        iw_p = gate_rows[2 * H:, :n_p]
        iw_d = gate_rows[2 * H:, n_p:]

        zc = jnp.zeros((B, H, dh, dh), F32)
        zn = jnp.zeros((B, H, dh), F32)
        zm = jnp.full((B, H), -jnp.inf, F32)
        ml = functools.partial(_mlstm, pr['a_q'], pr['a_k'], pr['a_v'], pr['a_o'])
        ha_p, c_p, nn_p, m_p = ml(gr_p, pr['gate_cols'], b_fgate[l], a_norm_g[l], zc, zn, zm,
                                  row0=0, n_seq=B, n_chunks=nc, L=L)
        ha_d, c_d, nn_d, m_d = ml(gr_d, pr['gate_cols'], b_fgate[l], a_norm_g[l], state_C[l], state_n[l], state_m[l],
                                  row0=n_p, n_seq=Bd, n_chunks=1, L=Td)

        bk, bv, bik = pr['b_k'], pr['b_v'], pr['b_ik']
        hb_p = _dsa_prompt(pr['b_q'][:n_p], pr['b_iq'][:n_p], iw_p, bk[:n_p], bv[:n_p], bik[:n_p], B=B, T=T)
        hb_d = _dsa_decode(pr['b_q'][n_p:], pr['b_iq'][n_p:], iw_d, bk[n_p:], bv[n_p:], bik[n_p:],
                           cache_kt, cache_vt, cache_kit, page_table, layer=l, Bd=Bd, Tq=Td)

        cu = pr['c_u']
        cu_p = cu[:n_p].reshape(B, T, C_WIDTH)
        cu_d = cu[n_p:].reshape(Bd, Td, C_WIDTH)
        zero_halo = jnp.zeros((B, POOL_HALO, C_WIDTH), F32)
        hc_p = _pool(cu_p, cu_p, zero_halo, c_map[l], c_scale[l], Tb=min(T, 512), pos0=0)
        halo_d = jnp.pad(state_pool[l], ((0, 0), (POOL_HALO - POOL_BUF, 0), (0, 0)))
        hc_d = _pool(cu_d, halo_d, halo_d, c_map[l], c_scale[l], Tb=Td, pos0=past_len)

        h_a = jnp.concatenate([ha_p, ha_d], axis=0)
        h_b = jnp.concatenate([hb_p, hb_d], axis=0)
        h_c = jnp.concatenate([hc_p.reshape(n_p, C_WIDTH), hc_d.reshape(n_d, C_WIDTH)], axis=0)
        x1 = _merge(x, h_a, h_b, h_c, pr['g_a'], pr['g_b'], pr['g_c'], p_a[l], p_b[l], p_c[l], w_out[l],
                    ln1_g[l], ln1_b[l], alpha=alpha)
        gates = _router(x1, router_w[l], router_b[l])
        x = _moe(x1, gates, exp_gu_b, exp_down_b, sh_gu[l].astype(BF16), sh_down[l].astype(BF16),
                 ln2_g[l], ln2_b[l], layer=l, alpha=alpha, tm=_moe_tile(n))

        kvs = (B_KV_HEADS, B_HEAD_DIM)
        pool_d = jnp.concatenate([state_pool[l], cu_d], axis=1)[:, -POOL_BUF:]
        new_p.append((bk[:n_p].reshape(B, T, *kvs), bv[:n_p].reshape(B, T, *kvs), bik[:n_p].reshape(B, T, IDX_DIM),
                      c_p, nn_p, m_p, cu_p[:, T - POOL_BUF:]))
        new_s.append((bk[n_p:].reshape(Bd, Td, *kvs), bv[n_p:].reshape(Bd, Td, *kvs),
                      bik[n_p:].reshape(Bd, Td, IDX_DIM), c_d, nn_d, m_d, pool_d))

    def stacked(states, i):
        return jnp.stack([s[i] for s in states])

    outs = [x[:n_p].reshape(B, T, D), x[n_p:].reshape(Bd, Td, D)]
    for i in range(7):
        outs += [stacked(new_p, i), stacked(new_s, i)]
    return tuple(outs)
```

```python
import functools
import math

import jax
import jax.numpy as jnp
from jax import lax
from jax.experimental import pallas as pl
from jax.experimental.pallas import tpu as pltpu

F32 = jnp.float32
BF16 = jnp.bfloat16
I32 = jnp.int32

A_HEADS = 4
A_HEAD_DIM = 128
A_WIDTH = A_HEADS * A_HEAD_DIM
MLSTM_CHUNK = 128
B_HEADS = 8
B_KV_HEADS = 2
B_HEAD_DIM = 64
B_WIDTH = B_HEADS * B_HEAD_DIM
B_KV_WIDTH = B_KV_HEADS * B_HEAD_DIM
B_GROUP = B_HEADS // B_KV_HEADS
IDX_HEADS = 8
IDX_DIM = 64
MAX_KEEP = 256
QUERY_BLOCK = 128
C_WINDOWS = (2, 4, 8, 16)
C_GROUPS = len(C_WINDOWS)
C_GROUP_WIDTH = 128
C_WIDTH = C_GROUPS * C_GROUP_WIDTH
POOL_BUF = max(C_WINDOWS) - 1
N_EXPERTS = 64
TOP_K = 8
N_EXPERT_GROUPS = 8
TOPK_GROUPS = 4
EXPERT_DIM = 256
ROUTED_SCALE = 2.5
LN_EPS = 1e-5

LANES = 128
SUBLANES = 8
VMEM_LIMIT = 56 * 1024 * 1024

INT_MIN = -2 ** 31
NEG_BIG = -1e30

MLSTM_SEQS_PER_STEP = 1
DSA_UNROLL = 4
DECODE_PAGES_PER_STEP = 16
PROJ_TM = 256
MERGE_TM = 256
ROUTER_TM = 256
POOL_HALO = 16


def _cparams(sem):
    return pltpu.CompilerParams(dimension_semantics=sem, vmem_limit_bytes=VMEM_LIMIT)


def _dot(a, b):
    return jnp.dot(a, b, preferred_element_type=F32)


def _dot_nt(a, b):
    return lax.dot_general(a, b, (((1,), (1,)), ((), ())), preferred_element_type=F32)


def _dot_tn(a, b):
    return lax.dot_general(a, b, (((0,), (0,)), ((), ())), preferred_element_type=F32)


def _sigmoid(x):
    return 1.0 / (1.0 + jnp.exp(-x))


def _log_sigmoid(x):
    return jnp.minimum(x, 0.0) - jnp.log1p(jnp.exp(-jnp.abs(x)))


def _layer_norm(x, g, b):
    mu = jnp.mean(x, axis=-1, keepdims=True)
    xc = x - mu
    var = jnp.mean(xc * xc, axis=-1, keepdims=True)
    return xc * lax.rsqrt(var + LN_EPS) * g + b


def _bf16_round(x):
    return x.astype(BF16).astype(F32)


def _exact_pow2(scale):
    m = float(scale)
    assert math.frexp(m)[0] == 0.5, m
    return m


def _sortable(s):
    bits = lax.bitcast_convert_type(s, I32)
    return jnp.where(bits < 0, bits ^ jnp.int32(0x7FFFFFFF), bits)


_PROJ_PLAIN = ('a_q', 'a_k', 'a_v', 'a_o', 'b_q', 'b_k', 'b_v', 'b_iq', 'b_ik', 'c_u', 'g_a', 'g_b', 'g_c')


def _proj_kernel(*refs):
    n_plain = len(_PROJ_PLAIN)
    x_ref = refs[0]
    w_refs = refs[1:1 + n_plain]
    b_refs = refs[1 + n_plain:1 + 2 * n_plain]
    wt, bt, wc, bc = refs[1 + 2 * n_plain:5 + 2 * n_plain]
    outs = refs[5 + 2 * n_plain:]
    xb = x_ref[...].astype(BF16)
    for w_ref, b_ref, o_ref in zip(w_refs, b_refs, outs[:n_plain]):
        o_ref[...] = _dot(xb, w_ref[...]) + b_ref[...]
    outs[n_plain][...] = _dot_nt(wt[...], xb) + bt[...]
    outs[n_plain + 1][...] = _dot(xb, wc[...]) + bc[...]


def _project(x, w_in, b_in):
    n, d = x.shape
    widths = (A_WIDTH, A_WIDTH, A_WIDTH, A_HEADS, A_HEADS, A_WIDTH,
              B_WIDTH, B_KV_WIDTH, B_KV_WIDTH, IDX_HEADS * IDX_DIM, IDX_DIM, IDX_HEADS,
              C_WIDTH, d, d, d)
    names = ('a_q', 'a_k', 'a_v', 'a_i', 'a_f', 'a_o', 'b_q', 'b_k', 'b_v', 'b_iq', 'b_ik', 'b_iw',
             'c_u', 'g_a', 'g_b', 'g_c')
    off, acc = {}, 0
    for nm, w in zip(names, widths):
        off[nm] = (acc, w)
        acc += w

    def cols(nm):
        s, w = off[nm]
        return w_in[:, s:s + w], b_in[s:s + w]

    ws, bs = [], []
    for nm in _PROJ_PLAIN:
        w, b = cols(nm)
        ws.append(w.astype(BF16))
        bs.append(b.reshape(1, -1))
    wi, bi = cols('a_i')
    wf, bf = cols('a_f')
    ww, bw = cols('b_iw')
    n_rows = 2 * A_HEADS + IDX_HEADS
    w_t = jnp.concatenate([wi, wf, ww], axis=1).T.astype(BF16)
    b_t = jnp.concatenate([bi, bf, bw]).reshape(n_rows, 1)
    w_c = jnp.pad(jnp.concatenate([wi, wf], axis=1), ((0, 0), (0, LANES - 2 * A_HEADS))).astype(BF16)
    b_c = jnp.pad(jnp.concatenate([bi, bf]), (0, LANES - 2 * A_HEADS)).reshape(1, LANES)

    tm = PROJ_TM
    assert n % tm == 0
    const = lambda i: (0, 0)
    in_specs = [pl.BlockSpec((tm, d), lambda i: (i, 0))]
    in_specs += [pl.BlockSpec(w.shape, const) for w in ws]
    in_specs += [pl.BlockSpec(b.shape, const) for b in bs]
    in_specs += [pl.BlockSpec(a.shape, const) for a in (w_t, b_t, w_c, b_c)]
    out_shape = [jax.ShapeDtypeStruct((n, w.shape[1]), F32) for w in ws]
    out_specs = [pl.BlockSpec((tm, w.shape[1]), lambda i: (i, 0)) for w in ws]
    out_shape += [jax.ShapeDtypeStruct((n_rows, n), F32), jax.ShapeDtypeStruct((n, LANES), F32)]
    out_specs += [pl.BlockSpec((n_rows, tm), lambda i: (0, i)), pl.BlockSpec((tm, LANES), lambda i: (i, 0))]
    res = pl.pallas_call(
        _proj_kernel, grid=(n // tm,), in_specs=in_specs, out_specs=out_specs, out_shape=out_shape,
        compiler_params=_cparams(("parallel",)),
    )(x, *ws, *bs, w_t, b_t, w_c, b_c)
    out = dict(zip(_PROJ_PLAIN, res[:len(_PROJ_PLAIN)]))
    out['gate_rows'] = res[-2]
    out['gate_cols'] = res[-1]
    return out


def _mlstm_kernel(*refs, L, S):
    per_seq = [refs[6 * s:6 * s + 6] for s in range(S)]
    br_ref, bc_ref, ng_ref, c0_ref, n0_ref, m0_ref, h_ref, c_out, n_out, m_out, c_s, n_s, m_s = refs[6 * S:]
    c = pl.program_id(1)
    nc = pl.num_programs(1)

    @pl.when(c == 0)
    def _():
        c_s[...] = c0_ref[...]
        n_s[...] = n0_ref[...]
        m_s[...] = m0_ref[...]

    row = lax.broadcasted_iota(I32, (L, L), 0)
    col = lax.broadcasted_iota(I32, (L, L), 1)
    tril = row >= col
    triu = row <= col
    k_scale = A_HEAD_DIM ** -0.5
    gates = []
    for s in range(S):
        gr = per_seq[s][3][0]
        gc = per_seq[s][4][...]
        gates.append((gr, gc, _log_sigmoid(gr + br_ref[...]), _log_sigmoid(gc + bc_ref[...])))
    for s, h in [(s, h) for s in range(S) for h in range(A_HEADS)]:
        q_ref, k_ref, v_ref, _, _, ao_ref = per_seq[s]
        gr, gc, lf_rows, lf_cols = gates[s]
        hs = slice(h * A_HEAD_DIM, (h + 1) * A_HEAD_DIM)
        q = q_ref[:, hs]
        k = k_ref[:, hs] * k_scale
        v = v_ref[:, hs]
        li_r = gr[h:h + 1, :]
        lf_r = lf_rows[A_HEADS + h:A_HEADS + h + 1, :]
        li_c = gc[:, h:h + 1]
        lf_c = lf_cols[:, A_HEADS + h:A_HEADS + h + 1]
        b_c = jnp.sum(jnp.where(tril, lf_r, 0.0), axis=-1, keepdims=True)
        b_r = jnp.sum(jnp.where(triu, lf_c, 0.0), axis=0, keepdims=True)
        m_prev = m_s[s, h]
        c_prev = c_s[s, h]
        n_prev = n_s[s, h]
        d_log = jnp.where(tril, b_c - b_r + li_r, -jnp.inf)
        inter = b_c + m_prev
        m_t = jnp.maximum(inter, jnp.max(d_log, axis=-1, keepdims=True))
        w_inter = jnp.exp(inter - m_t)
        qb = q.astype(BF16)
        kb = k.astype(BF16)
        vb = v.astype(BF16)
        a = jnp.exp(d_log - m_t) * _dot_nt(qb, kb)
        num = w_inter * _dot(qb, c_prev.astype(BF16)) + _dot(a.astype(BF16), vb)
        qn = jnp.sum(qb.astype(F32) * _bf16_round(n_prev), axis=-1, keepdims=True)
        den = w_inter * qn + jnp.sum(a, axis=-1, keepdims=True)
        hh = num / jnp.maximum(jnp.abs(den), jnp.exp(-m_t))
        m_new = m_t[L - 1:L, :]
        b_last = b_c[L - 1:L, :]
        w_c = jnp.exp(b_last - b_c + li_c - m_new)
        decay = jnp.exp(b_last + m_prev - m_new)
        kw = k * w_c
        c_s[s, h] = decay * c_prev + _dot_tn(kw.astype(BF16), vb)
        n_s[s, h] = decay * n_prev + jnp.sum(kb.astype(F32) * _bf16_round(w_c), axis=0, keepdims=True)
        m_s[s, h] = m_new
        mu = jnp.mean(hh, axis=-1, keepdims=True)
        hc = hh - mu
        var = jnp.mean(hc * hc, axis=-1, keepdims=True)
        hn = hc * lax.rsqrt(var + LN_EPS)
        h_ref[s, :, hs] = _sigmoid(ao_ref[:, hs]) * hn * ng_ref[:, hs]

    @pl.when(c == nc - 1)
    def _():
        c_out[...] = c_s[...]
        n_out[...] = n_s[...]
        m_out[...] = m_s[...]


def _mlstm(a_q, a_k, a_v, a_o, gate_rows, gate_cols, b_fgate, a_norm_g, c0, n0, m0, *, row0, n_seq, n_chunks, L):
    assert row0 % L == 0
    rb0 = row0 // L
    n_rows = n_seq * n_chunks * L
    H, dh = A_HEADS, A_HEAD_DIM
    S = MLSTM_SEQS_PER_STEP if n_seq % MLSTM_SEQS_PER_STEP == 0 else 1
    st = lambda b, c: (b, 0, 0, 0)
    const = lambda b, c: (0, 0)
    bias_r = jnp.pad(b_fgate, (A_HEADS, 0)).reshape(2 * A_HEADS, 1)
    bias_c = jnp.pad(b_fgate, (A_HEADS, LANES - 2 * A_HEADS)).reshape(1, LANES)
    seq_specs, seq_args = [], []
    for s in range(S):
        tok = lambda b, c, s=s: (rb0 + (b * S + s) * n_chunks + c, 0)
        seq_specs += [pl.BlockSpec((L, A_WIDTH), tok)] * 3
        seq_specs += [pl.BlockSpec((1, 2 * A_HEADS, L), lambda b, c, s=s: ((b * S + s) * n_chunks + c, 0, 0)),
                      pl.BlockSpec((L, LANES), tok),
                      pl.BlockSpec((L, A_WIDTH), tok)]
        seq_args += [a_q, a_k, a_v, gate_rows, gate_cols, a_o]
    outs = pl.pallas_call(
        functools.partial(_mlstm_kernel, L=L, S=S),
        grid=(n_seq // S, n_chunks),
        in_specs=seq_specs
        + [pl.BlockSpec((2 * A_HEADS, 1), const),
           pl.BlockSpec((1, LANES), const),
           pl.BlockSpec((1, A_WIDTH), const),
           pl.BlockSpec((S, H, dh, dh), st),
           pl.BlockSpec((S, H, 1, dh), st),
           pl.BlockSpec((S, H, 1, 1), st)],
        out_specs=[pl.BlockSpec((S, L, A_WIDTH), lambda b, c: (b, c, 0)),
                   pl.BlockSpec((S, H, dh, dh), st),
                   pl.BlockSpec((S, H, 1, dh), st),
                   pl.BlockSpec((S, H, 1, 1), st)],
        out_shape=[jax.ShapeDtypeStruct((n_seq, n_chunks * L, A_WIDTH), F32),
                   jax.ShapeDtypeStruct((n_seq, H, dh, dh), F32),
                   jax.ShapeDtypeStruct((n_seq, H, 1, dh), F32),
                   jax.ShapeDtypeStruct((n_seq, H, 1, 1), F32)],
        scratch_shapes=[pltpu.VMEM((S, H, dh, dh), F32), pltpu.VMEM((S, H, 1, dh), F32), pltpu.VMEM((S, H, 1, 1), F32)],
        compiler_params=_cparams(("parallel", "arbitrary")),
    )(*seq_args, bias_r, bias_c, a_norm_g.reshape(1, A_WIDTH),
      c0, n0.reshape(n_seq, H, 1, dh), m0.reshape(n_seq, H, 1, 1))
    h, c_new, n_new, m_new = outs
    return h.reshape(n_rows, A_WIDTH), c_new, n_new.reshape(n_seq, H, dh), m_new.reshape(n_seq, H)


def _threshold_search(count_ge, n_keep, shape, n_total):
    def body(j, carry):
        t, cnt_t = carry
        cand = t ^ (jnp.int32(1) << (31 - j))
        cnt = count_ge(cand)
        ok = cnt >= n_keep
        return jnp.where(ok, cand, t), jnp.where(ok, cnt, cnt_t)
    return lax.fori_loop(0, 32, body, (jnp.full(shape, INT_MIN, I32), jnp.full(shape, n_total, F32)))


def _threshold_search_radix4(count_ge, n_keep, shape, n_total):
    def body(j, carry):
        t, cnt_t = carry
        hi = jnp.int32(1) << (31 - 2 * j)
        lo = jnp.int32(1) << (30 - 2 * j)
        cands = (t ^ lo, t ^ hi, t ^ hi ^ lo)
        cnts = [count_ge(c) for c in cands]
        for cand, cnt in zip(cands, cnts):
            ok = cnt >= n_keep
            t, cnt_t = jnp.where(ok, cand, t), jnp.where(ok, cnt, cnt_t)
        return t, cnt_t
    return lax.fori_loop(0, 16, body, (jnp.full(shape, INT_MIN, I32), jnp.full(shape, n_total, F32)))


def _tree_sum(parts):
    parts = list(parts)
    while len(parts) > 1:
        parts = [parts[i] + parts[i + 1] for i in range(0, len(parts) - 1, 2)] + parts[len(parts) & ~1:]
    return parts[0]


def _dsa_prompt_kernel(qi_ref, w_ref, ki_ref, q_ref, k_ref, vt_ref, o_ref, s_ref, acc_ref, m_ref, l_ref,
                       bias_s, lg_s, p_s, *, n_keep):
    QB = QUERY_BLOCK
    U = DSA_UNROLL
    SK = U * QB
    PW = 2 * QB
    i = pl.program_id(1)
    n_trips = (i + U) // U
    key_minus_qry = lax.broadcasted_iota(I32, (QB, QB), 0) - lax.broadcasted_iota(I32, (QB, QB), 1)
    def heads_to_lanes(x_ref, dim):
        per_tile = LANES // dim
        slabs = []
        for tile in range(x_ref.shape[1] // LANES):
            xt = x_ref[:, tile * LANES:(tile + 1) * LANES].T
            slabs += [xt[e * dim:(e + 1) * dim, :] for e in range(per_tile)]
        return (jnp.concatenate(slabs, axis=1) * _exact_pow2(dim ** -0.5)).astype(BF16)

    qi = heads_to_lanes(qi_ref, IDX_DIM)
    w = _bf16_round(w_ref[0, 0] * (IDX_HEADS ** -0.5))

    def score_body(t, carry):
        for u in range(U):
            c = t * U + u
            ki = ki_ref[0, pl.ds(pl.multiple_of(c * QB, QB), QB), :]
            s = jnp.zeros((QB, QB), F32)
            for hp in range(IDX_HEADS // 2):
                d = _dot(ki, qi[:, hp * PW:(hp + 1) * PW])
                r = jnp.maximum(d.astype(BF16), 0).astype(F32)
                for h in (2 * hp, 2 * hp + 1):
                    s = s + r[:, (h % 2) * QB:(h % 2 + 1) * QB] * w[h:h + 1, :]
            visible = key_minus_qry <= (i - c) * QB
            s_ref[c] = jnp.where(visible, _sortable(s), INT_MIN)
        return carry

    lax.fori_loop(0, n_trips, score_body, 0)

    def count(pred):
        def body(t, acc):
            parts = []
            for u in range(U):
                hit = jnp.where(pred(s_ref[t * U + u]), 1.0, 0.0)
                parts += [hit[r * SUBLANES:(r + 1) * SUBLANES, :] for r in range(QB // SUBLANES)]
            return acc + _tree_sum(parts)
        acc = lax.fori_loop(0, n_trips, body, jnp.zeros((SUBLANES, QB), F32))
        return jnp.sum(acc, axis=0, keepdims=True)

    keep = float(n_keep)
    n_total = (n_trips * SK).astype(F32)
    thr, cnt_thr = _threshold_search(lambda cand: count(lambda blk: blk >= cand), keep, (1, QB), n_total)
    has_ties = jnp.max(jnp.where((thr != INT_MIN) & (cnt_thr > keep), 1.0, 0.0)) > 0.5

    m_ref[...] = jnp.full(m_ref.shape, NEG_BIG, F32)
    l_ref[...] = jnp.zeros(l_ref.shape, F32)
    acc_ref[...] = jnp.zeros(acc_ref.shape, F32)
    q = heads_to_lanes(q_ref, B_HEAD_DIM)
    pairs_per_group = B_GROUP // 2

    def fold(x, op):
        parts = [x[r * SUBLANES:(r + 1) * SUBLANES, :] for r in range(QB // SUBLANES)]
        while len(parts) > 1:
            parts = [op(parts[a], parts[a + 1]) for a in range(0, len(parts), 2)]
        return parts[0]

    n_pairs = B_HEADS // 2
    halves = [slice(e * QB, (e + 1) * QB) for e in range(2)]

    def logits_phase(t, slot):
        off = pl.multiple_of(t * SK, SK)
        mx = [[None, None] for _ in range(n_pairs)]
        for u in range(U):
            bias = bias_s[u]
            for g in range(B_KV_HEADS):
                kc = k_ref[0, g, pl.ds(off + u * QB, QB), :]
                for hp in range(g * pairs_per_group, (g + 1) * pairs_per_group):
                    lg = _dot(kc, q[:, hp * PW:(hp + 1) * PW])
                    for e, cs in enumerate(halves):
                        piece = lg[:, cs] + bias
                        lg_s[slot, hp, u, :, cs] = piece
                        pm = fold(piece, jnp.maximum)
                        mx[hp][e] = pm if mx[hp][e] is None else jnp.maximum(mx[hp][e], pm)
        return tuple(jnp.concatenate([jnp.max(mx[hp][e], axis=0, keepdims=True) for e in range(2)], axis=1)
                     for hp in range(n_pairs))

    def values_phase(t, slot, mx):
        off = pl.multiple_of(t * SK, SK)
        alpha = []
        for hp in range(n_pairs):
            m_old = m_ref[hp]
            m_new = jnp.maximum(m_old, mx[hp])
            alpha.append(jnp.exp(m_old - m_new))
            m_ref[hp] = m_new
            ls = [None, None]
            for u in range(U):
                for e, cs in enumerate(halves):
                    p = jnp.exp(lg_s[slot, hp, u, :, cs] - m_new[:, cs])
                    p_s[hp, u * QB:(u + 1) * QB, cs] = p.astype(BF16)
                    ps = fold(p, jnp.add)
                    ls[e] = ps if ls[e] is None else ls[e] + ps
            l_new = jnp.concatenate([jnp.sum(ls[e], axis=0, keepdims=True) for e in range(2)], axis=1)
            l_ref[hp] = alpha[hp] * l_ref[hp] + l_new
        for g in range(B_KV_HEADS):
            vt = vt_ref[0, g, :, pl.ds(off, SK)]
            for hp in range(g * pairs_per_group, (g + 1) * pairs_per_group):
                acc_ref[hp] = alpha[hp] * acc_ref[hp] + _dot(vt, p_s[hp])

    @pl.when(jnp.logical_not(has_ties))
    def _():
        thr_eff = jnp.maximum(thr, INT_MIN + 1)

        def select(t):
            for u in range(U):
                bias_s[u] = jnp.where(s_ref[t * U + u] >= thr_eff, 0.0, NEG_BIG)

        select(0)
        mx0 = logits_phase(0, 0)

        def body(t, mx):
            values_phase(t, 0, mx)
            select(t + 1)
            return logits_phase(t + 1, 0)

        values_phase(n_trips - 1, 0, lax.fori_loop(0, n_trips - 1, body, mx0))

    @pl.when(has_ties)
    def _():
        n_ties = keep - count(lambda blk: blk > thr)
        tri = (lax.broadcasted_iota(I32, (SK, SK), 0) >= lax.broadcasted_iota(I32, (SK, SK), 1)).astype(BF16)

        def body(t, run):
            blk = s_ref[pl.ds(t * U, U)].reshape(SK, QB)
            tie = (blk == thr) & (blk != INT_MIN)
            prefix = _dot(tri, jnp.where(tie, 1.0, 0.0).astype(BF16))
            sel = (blk > thr) | (tie & (run + prefix <= n_ties))
            bias_s[...] = jnp.where(sel, 0.0, NEG_BIG).reshape(U, QB, QB)
            values_phase(t, 0, logits_phase(t, 0))
            return run + prefix[SK - 1:SK, :]

        lax.fori_loop(0, n_trips, body, jnp.zeros((1, QB), F32))

    for hp in range(B_HEADS // 2):
        o = acc_ref[hp] / l_ref[hp]
        o_ref[:, hp * LANES:(hp + 1) * LANES] = jnp.concatenate([o[:, :QB], o[:, QB:]], axis=0).T


def _dsa_prompt(b_q, b_iq, iw_rows, b_k, b_v, b_ik, *, B, T):
    QB = QUERY_BLOCK
    nb = T // QB
    n_keep = min(MAX_KEEP, T // 4)

    assert nb % DSA_UNROLL == 0
    w_t = jnp.transpose(iw_rows.reshape(IDX_HEADS, B, nb, QB), (1, 2, 0, 3))
    ki = b_ik.reshape(B, T, IDX_DIM).astype(BF16)
    kh = jnp.transpose(b_k.reshape(B, T, B_KV_HEADS, B_HEAD_DIM), (0, 2, 1, 3)).astype(BF16)
    vt = jnp.transpose(b_v.reshape(B, T, B_KV_HEADS, B_HEAD_DIM), (0, 2, 3, 1)).astype(BF16)
    n_pairs = B_HEADS // 2
    tok = lambda b, i: (b * nb + i, 0)
    return pl.pallas_call(
        functools.partial(_dsa_prompt_kernel, n_keep=n_keep),
        grid=(B, nb),
        in_specs=[pl.BlockSpec((QB, IDX_HEADS * IDX_DIM), tok),
                  pl.BlockSpec((1, 1, IDX_HEADS, QB), lambda b, i: (b, i, 0, 0)),
                  pl.BlockSpec((1, T, IDX_DIM), lambda b, i: (b, 0, 0)),
                  pl.BlockSpec((QB, B_WIDTH), tok),
                  pl.BlockSpec((1, B_KV_HEADS, T, B_HEAD_DIM), lambda b, i: (b, 0, 0, 0)),
                  pl.BlockSpec((1, B_KV_HEADS, B_HEAD_DIM, T), lambda b, i: (b, 0, 0, 0))],
        out_specs=pl.BlockSpec((QB, B_WIDTH), tok),
        out_shape=jax.ShapeDtypeStruct((B * T, B_WIDTH), F32),
        scratch_shapes=[pltpu.VMEM((nb, QB, QB), I32),
                        pltpu.VMEM((n_pairs, B_HEAD_DIM, 2 * QB), F32),
                        pltpu.VMEM((n_pairs, 1, 2 * QB), F32),
                        pltpu.VMEM((n_pairs, 1, 2 * QB), F32),
                        pltpu.VMEM((DSA_UNROLL, QB, QB), F32),
                        pltpu.VMEM((1, n_pairs, DSA_UNROLL, QB, 2 * QB), F32),
                        pltpu.VMEM((n_pairs, DSA_UNROLL * QB, 2 * QB), BF16)],
        compiler_params=_cparams(("parallel", "arbitrary")),
    )(b_iq, w_t, ki, b_q, kh, vt)


def _dsa_decode_kernel(pt_ref, qi_ref, w_ref, q_ref, *rest, n_keep, n_q, pps):
    ck_refs, cv_refs, cki_refs = rest[0:pps], rest[pps:2 * pps], rest[2 * pps:3 * pps]
    nk_ref, nv_ref, nki_ref, o_ref, s_ref, k_s, v_s = rest[3 * pps:]
    P = LANES
    p = pl.program_id(1)
    n_steps = pl.num_programs(1)
    n_rows = IDX_HEADS * n_q
    qi = qi_ref[0]
    w = _bf16_round(w_ref[0] * (IDX_HEADS ** -0.5))

    def chunk_scores(ki_t):
        t = _bf16_round(jnp.maximum(_dot(qi, ki_t), 0.0)) * w
        s = jnp.zeros((n_q, ki_t.shape[1]), F32)
        for h in range(IDX_HEADS):
            s = s + t[h * n_q:(h + 1) * n_q, :]
        return s

    off = pl.multiple_of(p * (pps * P), pps * P)
    pages = lambda refs: jnp.concatenate([r[0, 0].astype(BF16) for r in refs], axis=1)
    s_ref[:, pl.ds(off, pps * P)] = _sortable(chunk_scores(pages(cki_refs)))
    k_s[:, pl.ds(off, pps * P)] = pages(ck_refs)
    v_s[:, pl.ds(off, pps * P)] = pages(cv_refs)

    @pl.when(p == n_steps - 1)
    def _():
        n_keys = s_ref.shape[1]
        past = n_keys - P
        n_chunks = n_keys // P
        key_j = lax.broadcasted_iota(I32, (n_q, P), 1)
        qry_t = lax.broadcasted_iota(I32, (n_q, P), 0)
        visible = key_j <= qry_t
        s_ref[:, pl.ds(past, P)] = jnp.where(visible, _sortable(chunk_scores(nki_ref[0].astype(BF16))), INT_MIN)
        k_s[:, pl.ds(past, P)] = nk_ref[0].astype(BF16)
        v_s[:, pl.ds(past, P)] = nv_ref[0].astype(BF16)

        def count(pred):
            hit = jnp.where(pred(s_ref[...]), 1.0, 0.0)
            return jnp.sum(_tree_sum([hit[:, c * P:(c + 1) * P] for c in range(n_chunks)]), axis=-1, keepdims=True)

        keep = float(n_keep)
        thr, cnt_thr = _threshold_search_radix4(lambda cand: count(lambda blk: blk >= cand), keep, (n_q, 1),
                                                float(n_keys))
        has_ties = jnp.max(jnp.where((thr != INT_MIN) & (cnt_thr > keep), 1.0, 0.0)) > 0.5

        @pl.when(jnp.logical_not(has_ties))
        def _():
            thr_eff = jnp.maximum(thr, INT_MIN + 1)
            s_ref[...] = jnp.where(s_ref[...] >= thr_eff, 1, 0).astype(I32)

        @pl.when(has_ties)
        def _():
            n_ties = keep - count(lambda blk: blk > thr)
            tri = (lax.broadcasted_iota(I32, (P, P), 0) <= lax.broadcasted_iota(I32, (P, P), 1)).astype(BF16)

            def sel_body(c, run):
                o = pl.multiple_of(c * P, P)
                blk = s_ref[:, pl.ds(o, P)]
                tie = (blk == thr) & (blk != INT_MIN)
                prefix = _dot(jnp.where(tie, 1.0, 0.0).astype(BF16), tri)
                sel = (blk > thr) | (tie & (run + prefix <= n_ties))
                s_ref[:, pl.ds(o, P)] = jnp.where(sel, 1, 0).astype(I32)
                return run + prefix[:, P - 1:P]

            lax.fori_loop(0, n_chunks, sel_body, jnp.zeros((n_q, 1), F32))

        sel = s_ref[...] > 0
        q = q_ref[0]
        lg = _dot(q, k_s[...])
        L = lg.shape[1]
        lg = jnp.where(sel[None], lg.reshape(B_HEADS, n_q, L), NEG_BIG)
        m = jnp.max(lg, axis=-1, keepdims=True)
        pr = jnp.exp(lg - m)
        pr = pr / jnp.sum(pr, axis=-1, keepdims=True)
        o_ref[0] = _dot_nt(pr.reshape(n_rows, L).astype(BF16), v_s[...])


def _dsa_decode(b_q, b_iq, iw_rows, b_k, b_v, b_ik, cache_kt, cache_vt, cache_kit, page_table, *, layer, Bd, Tq):
    page = cache_kt.shape[-1]
    assert page == LANES
    n_pages = page_table.shape[1]
    past = n_pages * page
    n_keep = min(MAX_KEEP, (past + Tq) // 4)
    n_rows = B_HEADS * Tq
    kvw = B_KV_HEADS * B_HEAD_DIM

    def rows_hq(a, heads, dim):
        return jnp.transpose(a.reshape(Bd, Tq, heads, dim), (0, 2, 1, 3)).reshape(Bd, heads * Tq, dim)

    qi = rows_hq(b_iq, IDX_HEADS, IDX_DIM).astype(BF16) * _exact_pow2(IDX_DIM ** -0.5)
    w = jnp.transpose(iw_rows.reshape(IDX_HEADS, Bd, Tq), (1, 0, 2)).reshape(Bd, IDX_HEADS * Tq, 1)
    qh = rows_hq(b_q, B_HEADS, B_HEAD_DIM).astype(BF16) * _exact_pow2(B_HEAD_DIM ** -0.5)
    group = (jnp.arange(n_rows) // Tq) // B_GROUP
    lane_group = jnp.arange(kvw) // B_HEAD_DIM
    q_wide = jnp.where(group[:, None] == lane_group[None, :], jnp.tile(qh, (1, 1, B_KV_HEADS)), 0.0).astype(BF16)
    pps = DECODE_PAGES_PER_STEP if n_pages % DECODE_PAGES_PER_STEP == 0 else 1

    def pad_new(a, width):
        return jnp.pad(jnp.transpose(a.reshape(Bd, Tq, width), (0, 2, 1)), ((0, 0), (0, 0), (0, page - Tq)))

    nk, nv, nki = pad_new(b_k, kvw), pad_new(b_v, kvw), pad_new(b_ik, IDX_DIM)
    n_keys = past + page
    seq = lambda b, p, pt: (b, 0, 0)
    pages = lambda width: [pl.BlockSpec((1, 1, width, page), lambda b, p, pt, j=j: (layer, pt[b, p * pps + j], 0, 0))
                           for j in range(pps)]
    out = pl.pallas_call(
        functools.partial(_dsa_decode_kernel, n_keep=n_keep, n_q=Tq, pps=pps),
        grid_spec=pltpu.PrefetchScalarGridSpec(
            num_scalar_prefetch=1,
            grid=(Bd, n_pages // pps),
            in_specs=[pl.BlockSpec((1, n_rows, IDX_DIM), seq),
                      pl.BlockSpec((1, n_rows, 1), seq),
                      pl.BlockSpec((1, n_rows, kvw), seq)]
            + pages(kvw) + pages(kvw) + pages(IDX_DIM)
            + [pl.BlockSpec((1, kvw, page), seq),
               pl.BlockSpec((1, kvw, page), seq),
               pl.BlockSpec((1, IDX_DIM, page), seq)],
            out_specs=pl.BlockSpec((1, n_rows, kvw), seq),
            scratch_shapes=[pltpu.VMEM((Tq, n_keys), I32),
                            pltpu.VMEM((kvw, n_keys), BF16),
                            pltpu.VMEM((kvw, n_keys), BF16)]),
        out_shape=jax.ShapeDtypeStruct((Bd, n_rows, kvw), F32),
        compiler_params=_cparams(("parallel", "arbitrary")),
    )(page_table, qi, w, q_wide, *([cache_kt] * pps), *([cache_vt] * pps), *([cache_kit] * pps), nk, nv, nki)
    out = out.reshape(Bd, B_HEADS, Tq, B_KV_HEADS, B_HEAD_DIM)
    out = jnp.concatenate([out[:, g * B_GROUP:(g + 1) * B_GROUP, :, g] for g in range(B_KV_HEADS)], axis=1)
    return jnp.transpose(out, (0, 2, 1, 3)).reshape(Bd * Tq, B_WIDTH)


def _pool_kernel(u_ref, prev_ref, halo0_ref, cmap_ref, cscale_ref, o_ref, ext, *, Tb, pos0):
    j = pl.program_id(1)
    H = POOL_HALO
    ext[0:H, :] = jnp.where(j == 0, halo0_ref[0], prev_ref[0])
    ext[H:H + Tb, :] = u_ref[0]
    pos = pos0 + j * Tb + lax.broadcasted_iota(I32, (Tb, 1), 0)
    for g, wdw in enumerate(C_WINDOWS):
        ls = slice(g * C_GROUP_WIDTH, (g + 1) * C_GROUP_WIDTH)
        win = ext[H:H + Tb, ls]
        for d in range(1, wdw):
            win = win + ext[H - d:H - d + Tb, ls]
        cnt = jnp.minimum(wdw, pos + 1).astype(F32)
        pooled = win / cnt - ext[H:H + Tb, ls]
        z = _dot(pooled.astype(BF16), cmap_ref[g])
        o_ref[0, :, ls] = z * cscale_ref[:, ls]


def _pool(u, prev_src, halo0, c_map, c_scale, *, Tb, pos0):
    Bn, R, C = u.shape
    H = POOL_HALO
    assert R % Tb == 0 and (Tb % H == 0 or R == Tb)
    prev_idx = (lambda b, j: (b, jnp.maximum(j * (Tb // H) - 1, 0), 0)) if R > Tb else (lambda b, j: (b, 0, 0))
    return pl.pallas_call(
        functools.partial(_pool_kernel, Tb=Tb, pos0=pos0),
        grid=(Bn, R // Tb),
        in_specs=[pl.BlockSpec((1, Tb, C), lambda b, j: (b, j, 0)),
                  pl.BlockSpec((1, H, C), prev_idx),
                  pl.BlockSpec((1, H, C), lambda b, j: (b, 0, 0)),
                  pl.BlockSpec(c_map.shape, lambda b, j: (0, 0, 0)),
                  pl.BlockSpec((1, C), lambda b, j: (0, 0))],
        out_specs=pl.BlockSpec((1, Tb, C), lambda b, j: (b, j, 0)),
        out_shape=jax.ShapeDtypeStruct((Bn, R, C), F32),
        scratch_shapes=[pltpu.VMEM((H + Tb, C), F32)],
        compiler_params=_cparams(("parallel", "parallel")),
    )(u, prev_src, halo0, c_map.astype(BF16), c_scale.reshape(1, C))


def _merge_kernel(x_ref, hap_ref, had_ref, hbp_ref, hbd_ref, hcp_ref, hcd_ref, ga_ref, gb_ref, gc_ref,
                  pa_ref, pb_ref, pc_ref, wo_ref, g1_ref, b1_ref, o_ref, *, alpha, n_prompt_tiles):
    is_prompt = pl.program_id(0) < n_prompt_tiles
    h_a, h_b, h_c = [jnp.where(is_prompt, p_ref[...], d_ref[...]).astype(BF16)
                     for p_ref, d_ref in ((hap_ref, had_ref), (hbp_ref, hbd_ref), (hcp_ref, hcd_ref))]
    merged = (_sigmoid(ga_ref[...]) * _dot(h_a, pa_ref[...])
              + _sigmoid(gb_ref[...]) * _dot(h_b, pb_ref[...])
              + _sigmoid(gc_ref[...]) * _dot(h_c, pc_ref[...]))
    y = alpha * x_ref[...] + _dot(merged.astype(BF16), wo_ref[...])
    o_ref[...] = _layer_norm(y, g1_ref[...], b1_ref[...])


def _merge(x, h_prompt, h_decode, g_a, g_b, g_c, p_a, p_b, p_c, w_out, ln_g, ln_b, *, alpha):
    n, d = x.shape
    tm = MERGE_TM
    n_p, n_d = h_prompt[0].shape[0], h_decode[0].shape[0]
    assert n_p % tm == 0 and n_d == tm and n == n_p + n_d
    n_prompt_tiles = n_p // tm
    tok = lambda w: pl.BlockSpec((tm, w), lambda i: (i, 0))
    prompt = lambda w: pl.BlockSpec((tm, w), lambda i: (jnp.minimum(i, n_prompt_tiles - 1), 0))
    full = lambda a: pl.BlockSpec(a.shape, lambda i: (0, 0))
    ws = [p_a.astype(BF16), p_b.astype(BF16), p_c.astype(BF16), w_out.astype(BF16), ln_g.reshape(1, d), ln_b.reshape(1, d)]
    mixers = [a for pair in zip(h_prompt, h_decode) for a in pair]
    mixer_specs = [spec for hp, hd in zip(h_prompt, h_decode) for spec in (prompt(hp.shape[1]), full(hd))]
    return pl.pallas_call(
        functools.partial(_merge_kernel, alpha=alpha, n_prompt_tiles=n_prompt_tiles),
        grid=(n // tm,),
        in_specs=[tok(d)] + mixer_specs + [tok(d), tok(d), tok(d)] + [full(a) for a in ws],
        out_specs=tok(d),
        out_shape=jax.ShapeDtypeStruct((n, d), F32),
        compiler_params=_cparams(("parallel",)),
    )(x, *mixers, g_a, g_b, g_c, *ws)


def _router_kernel(x_ref, w_ref, rb_ref, o_ref):
    tm = x_ref.shape[0]
    G = N_EXPERT_GROUPS
    per_group = N_EXPERTS // G
    assert per_group == SUBLANES
    neg = -jnp.inf
    scores = _sigmoid(_dot_nt(w_ref[...], x_ref[...].astype(BF16)))
    biased = scores + rb_ref[...]
    sc = [scores[g * per_group:(g + 1) * per_group, :] for g in range(G)]
    bg = [biased[g * per_group:(g + 1) * per_group, :] for g in range(G)]
    sub = lax.broadcasted_iota(I32, (per_group, tm), 0).astype(F32)

    def colmax(a):
        return jnp.max(a, axis=0, keepdims=True)

    def first_in_group(a, m):
        return jnp.min(jnp.where(a == m, sub, float(per_group)), axis=0, keepdims=True)

    gscore = []
    for g in range(G):
        m1 = colmax(bg[g])
        rest = jnp.where(sub == first_in_group(bg[g], m1), neg, bg[g])
        gscore.append(m1 + colmax(rest))
    rem = []
    for g in range(G):
        beaten = jnp.zeros((1, tm), F32)
        for o in range(G):
            if o != g:
                wins = (gscore[o] > gscore[g]) | ((gscore[o] == gscore[g]) & (o < g))
                beaten = beaten + jnp.where(wins, 1.0, 0.0)
        rem.append(jnp.where(beaten < float(TOPK_GROUPS), bg[g], neg))
    chosen = [jnp.zeros((per_group, tm), jnp.bool_) for _ in range(G)]
    for _ in range(TOP_K):
        best = colmax(functools.reduce(jnp.maximum, rem))
        idx = functools.reduce(jnp.minimum, [
            jnp.min(jnp.where(rem[g] == best, sub + float(g * per_group), float(N_EXPERTS)), axis=0, keepdims=True)
            for g in range(G)])
        for g in range(G):
            hit = (sub + float(g * per_group)) == idx
            chosen[g] = chosen[g] | hit
            rem[g] = jnp.where(hit, neg, rem[g])
    s_sel = [jnp.where(chosen[g], sc[g], 0.0) for g in range(G)]
    total = jnp.sum(functools.reduce(jnp.add, s_sel), axis=0, keepdims=True)
    gates = jnp.concatenate([ROUTED_SCALE * s / total for s in s_sel]
                            + [jnp.zeros((LANES - N_EXPERTS, tm), F32)], axis=0)
    o_ref[...] = gates.T


def _router(x, router_w, router_b):
    n, d = x.shape
    tm = ROUTER_TM
    assert n % tm == 0
    w = router_w.T.astype(BF16)
    full = lambda a: pl.BlockSpec(a.shape, lambda i: (0, 0))
    rb = router_b.reshape(N_EXPERTS, 1)
    return pl.pallas_call(
        _router_kernel, grid=(n // tm,),
        in_specs=[pl.BlockSpec((tm, d), lambda i: (i, 0)), full(w), full(rb)],
        out_specs=pl.BlockSpec((tm, LANES), lambda i: (i, 0)),
        out_shape=jax.ShapeDtypeStruct((n, LANES), F32),
        compiler_params=_cparams(("parallel",)),
    )(x, w, rb)


def _moe_kernel(x_ref, g_ref, wgu_ref, wd_ref, sgu_ref, sd_ref, g2_ref, b2_ref, o_ref, xb_s, *, alpha):
    e = pl.program_id(1)
    n_e = pl.num_programs(1)

    def swiglu(w_gu, w_down):
        gu = _dot(xb_s[...], w_gu)
        gg = gu[:, :EXPERT_DIM]
        act = gg * _sigmoid(gg) * gu[:, EXPERT_DIM:]
        return _dot(act.astype(BF16), w_down)

    @pl.when(e == 0)
    def _():
        xb_s[...] = x_ref[...].astype(BF16)
        o_ref[...] = swiglu(sgu_ref[...], sd_ref[...])

    gates = g_ref[...]
    lane = lax.broadcasted_iota(I32, gates.shape, 1)
    gate = jnp.sum(jnp.where(lane == e, gates, 0.0), axis=-1, keepdims=True)
    y = swiglu(wgu_ref[0, 0], wd_ref[0, 0])
    o_ref[...] += jnp.where(gate != 0.0, y * gate, 0.0)

    @pl.when(e == n_e - 1)
    def _():
        o_ref[...] = _layer_norm(alpha * x_ref[...] + o_ref[...], g2_ref[...], b2_ref[...])


def _moe(x, gates, wgu, wd, sgu, sd, ln_g, ln_b, *, layer, alpha, tm):
    n, d = x.shape
    n_e = wgu.shape[1]
    assert n % tm == 0
    const = lambda i, e: (0, 0)
    return pl.pallas_call(
        functools.partial(_moe_kernel, alpha=alpha),
        grid=(n // tm, n_e),
        in_specs=[pl.BlockSpec((tm, d), lambda i, e: (i, 0)),
                  pl.BlockSpec((tm, LANES), lambda i, e: (i, 0)),
                  pl.BlockSpec((1, 1) + wgu.shape[2:], lambda i, e: (layer, e, 0, 0)),
                  pl.BlockSpec((1, 1) + wd.shape[2:], lambda i, e: (layer, e, 0, 0)),
                  pl.BlockSpec(sgu.shape, const),
                  pl.BlockSpec(sd.shape, const),
                  pl.BlockSpec((1, d), const),
                  pl.BlockSpec((1, d), const)],
        out_specs=pl.BlockSpec((tm, d), lambda i, e: (i, 0)),
        out_shape=jax.ShapeDtypeStruct((n, d), F32),
        scratch_shapes=[pltpu.VMEM((tm, d), BF16)],
        compiler_params=_cparams(("parallel", "arbitrary")),
    )(x, gates, wgu, wd, sgu, sd, ln_g.reshape(1, d), ln_b.reshape(1, d))


def _moe_tile(n):
    for cand in (1280, 1024, 512, 256, 128):
        if n % cand == 0:
            return cand
    return n


def kernel(x_prompt, x_sample, cache_k, cache_v, cache_kidx, state_C, state_n, state_m, state_pool, page_table, w_in, b_in, b_fgate, a_norm_g, c_map, c_scale, p_a, p_b, p_c, w_out, ln1_g, ln1_b, router_w, router_b, exp_gu, exp_down, sh_gu, sh_down, ln2_g, ln2_b):
    B, T, D = x_prompt.shape
    Bd, Td, _ = x_sample.shape
    depth = w_in.shape[0]
    alpha = (2 * depth) ** 0.25
    n_p, n_d = B * T, Bd * Td
    n = n_p + n_d
    L = MLSTM_CHUNK
    nc = T // L
    past_len = page_table.shape[1] * cache_k.shape[2]
    H, dh = A_HEADS, A_HEAD_DIM

    n_pool, page = cache_k.shape[1:3]
    cache_kt = jnp.transpose(cache_k, (0, 1, 3, 4, 2)).reshape(depth, n_pool, B_KV_WIDTH, page)
    cache_vt = jnp.transpose(cache_v, (0, 1, 3, 4, 2)).reshape(depth, n_pool, B_KV_WIDTH, page)
    cache_kit = jnp.transpose(cache_kidx, (0, 1, 3, 2))
    exp_gu_b = exp_gu.astype(BF16)
    exp_down_b = exp_down.astype(BF16)

    x = jnp.concatenate([x_prompt.reshape(n_p, D), x_sample.reshape(n_d, D)], axis=0)
    new_p, new_s = [], []
    for l in range(depth):
        pr = _project(x, w_in[l], b_in[l])
        gate_rows = pr['gate_rows']
        gr_p = jnp.transpose(gate_rows[:2 * H, :n_p].reshape(2 * H, B * nc, L), (1, 0, 2))
        gr_d = jnp.transpose(gate_rows[:2 * H, n_p:].reshape(2 * H, Bd, Td), (1, 0, 2))
        iw_p = gate_rows[2 * H:, :n_p]
        iw_d = gate_rows[2 * H:, n_p:]

        zc = jnp.zeros((B, H, dh, dh), F32)
        zn = jnp.zeros((B, H, dh), F32)
        zm = jnp.full((B, H), -jnp.inf, F32)
        ml = functools.partial(_mlstm, pr['a_q'], pr['a_k'], pr['a_v'], pr['a_o'])
        ha_p, c_p, nn_p, m_p = ml(gr_p, pr['gate_cols'], b_fgate[l], a_norm_g[l], zc, zn, zm,
                                  row0=0, n_seq=B, n_chunks=nc, L=L)
        ha_d, c_d, nn_d, m_d = ml(gr_d, pr['gate_cols'], b_fgate[l], a_norm_g[l], state_C[l], state_n[l], state_m[l],
                                  row0=n_p, n_seq=Bd, n_chunks=1, L=Td)

        bk, bv, bik = pr['b_k'], pr['b_v'], pr['b_ik']
        hb_p = _dsa_prompt(pr['b_q'], pr['b_iq'], iw_p, bk[:n_p], bv[:n_p], bik[:n_p], B=B, T=T)
        hb_d = _dsa_decode(pr['b_q'][n_p:], pr['b_iq'][n_p:], iw_d, bk[n_p:], bv[n_p:], bik[n_p:],
                           cache_kt, cache_vt, cache_kit, page_table, layer=l, Bd=Bd, Tq=Td)

        cu = pr['c_u']
        cu_p = cu[:n_p].reshape(B, T, C_WIDTH)
        cu_d = cu[n_p:].reshape(Bd, Td, C_WIDTH)
        zero_halo = jnp.zeros((B, POOL_HALO, C_WIDTH), F32)
        hc_p = _pool(cu_p, cu_p, zero_halo, c_map[l], c_scale[l], Tb=min(T, 512), pos0=0)
        halo_d = jnp.pad(state_pool[l], ((0, 0), (POOL_HALO - POOL_BUF, 0), (0, 0)))
        hc_d = _pool(cu_d, halo_d, halo_d, c_map[l], c_scale[l], Tb=Td, pos0=past_len)

        x1 = _merge(x, (ha_p, hb_p, hc_p.reshape(n_p, C_WIDTH)), (ha_d, hb_d, hc_d.reshape(n_d, C_WIDTH)),
                    pr['g_a'], pr['g_b'], pr['g_c'], p_a[l], p_b[l], p_c[l], w_out[l], ln1_g[l], ln1_b[l], alpha=alpha)
        gates = _router(x1, router_w[l], router_b[l])
        x = _moe(x1, gates, exp_gu_b, exp_down_b, sh_gu[l].astype(BF16), sh_down[l].astype(BF16),
                 ln2_g[l], ln2_b[l], layer=l, alpha=alpha, tm=_moe_tile(n))

        kvs = (B_KV_HEADS, B_HEAD_DIM)
        pool_d = jnp.concatenate([state_pool[l], cu_d], axis=1)[:, -POOL_BUF:]
        new_p.append((bk[:n_p].reshape(B, T, *kvs), bv[:n_p].reshape(B, T, *kvs), bik[:n_p].reshape(B, T, IDX_DIM),
                      c_p, nn_p, m_p, cu_p[:, T - POOL_BUF:]))
        new_s.append((bk[n_p:].reshape(Bd, Td, *kvs), bv[n_p:].reshape(Bd, Td, *kvs),
                      bik[n_p:].reshape(Bd, Td, IDX_DIM), c_d, nn_d, m_d, pool_d))

    def stacked(states, i):
        return jnp.stack([s[i] for s in states])

    outs = [x[:n_p].reshape(B, T, D), x[n_p:].reshape(Bd, Td, D)]
    for i in range(7):
        outs += [stacked(new_p, i), stacked(new_s, i)]
    return tuple(outs)
```

```python
import functools
import math

import jax
import jax.numpy as jnp
from jax import lax
from jax.experimental import pallas as pl
from jax.experimental.pallas import tpu as pltpu

F32 = jnp.float32
BF16 = jnp.bfloat16
I32 = jnp.int32

A_HEADS = 4
A_HEAD_DIM = 128
A_WIDTH = A_HEADS * A_HEAD_DIM
MLSTM_CHUNK = 128
B_HEADS = 8
B_KV_HEADS = 2
B_HEAD_DIM = 64
B_WIDTH = B_HEADS * B_HEAD_DIM
B_KV_WIDTH = B_KV_HEADS * B_HEAD_DIM
B_GROUP = B_HEADS // B_KV_HEADS
IDX_HEADS = 8
IDX_DIM = 64
MAX_KEEP = 256
QUERY_BLOCK = 128
C_WINDOWS = (2, 4, 8, 16)
C_GROUPS = len(C_WINDOWS)
C_GROUP_WIDTH = 128
C_WIDTH = C_GROUPS * C_GROUP_WIDTH
POOL_BUF = max(C_WINDOWS) - 1
N_EXPERTS = 64
TOP_K = 8
N_EXPERT_GROUPS = 8
TOPK_GROUPS = 4
EXPERT_DIM = 256
ROUTED_SCALE = 2.5
LN_EPS = 1e-5

LANES = 128
SUBLANES = 8
VMEM_LIMIT = 56 * 1024 * 1024

INT_MIN = -2 ** 31
NEG_BIG = -1e30

MLSTM_SEQS_PER_STEP = 1
DSA_UNROLL = 4
DECODE_PAGES_PER_STEP = 32
PROJ_TM = 256
MERGE_TM = 256
ROUTER_TM = 256
POOL_HALO = 16


def _cparams(sem):
    return pltpu.CompilerParams(dimension_semantics=sem, vmem_limit_bytes=VMEM_LIMIT)


def _dot(a, b):
    return jnp.dot(a, b, preferred_element_type=F32)


def _dot_nt(a, b):
    return lax.dot_general(a, b, (((1,), (1,)), ((), ())), preferred_element_type=F32)


def _dot_tn(a, b):
    return lax.dot_general(a, b, (((0,), (0,)), ((), ())), preferred_element_type=F32)


def _sigmoid(x):
    return 1.0 / (1.0 + jnp.exp(-x))


def _log_sigmoid(x):
    return jnp.minimum(x, 0.0) - jnp.log1p(jnp.exp(-jnp.abs(x)))


def _layer_norm(x, g, b):
    mu = jnp.mean(x, axis=-1, keepdims=True)
    xc = x - mu
    var = jnp.mean(xc * xc, axis=-1, keepdims=True)
    return xc * lax.rsqrt(var + LN_EPS) * g + b


def _bf16_round(x):
    return x.astype(BF16).astype(F32)


def _exact_pow2(scale):
    m = float(scale)
    assert math.frexp(m)[0] == 0.5, m
    return m


def _sortable(s):
    bits = lax.bitcast_convert_type(s, I32)
    return jnp.where(bits < 0, bits ^ jnp.int32(0x7FFFFFFF), bits)


_PROJ_PLAIN = ('a_q', 'a_k', 'a_v', 'a_o', 'b_q', 'b_k', 'b_v', 'b_iq', 'b_ik', 'c_u', 'g_a', 'g_b', 'g_c')


def _proj_kernel(*refs):
    n_plain = len(_PROJ_PLAIN)
    x_ref = refs[0]
    w_refs = refs[1:1 + n_plain]
    b_refs = refs[1 + n_plain:1 + 2 * n_plain]
    wt, bt, wc, bc = refs[1 + 2 * n_plain:5 + 2 * n_plain]
    outs = refs[5 + 2 * n_plain:]
    xb = x_ref[...].astype(BF16)
    for w_ref, b_ref, o_ref in zip(w_refs, b_refs, outs[:n_plain]):
        o_ref[...] = _dot(xb, w_ref[...]) + b_ref[...]
    outs[n_plain][...] = _dot_nt(wt[...], xb) + bt[...]
    outs[n_plain + 1][...] = _dot(xb, wc[...]) + bc[...]


def _project(x, w_in, b_in):
    n, d = x.shape
    widths = (A_WIDTH, A_WIDTH, A_WIDTH, A_HEADS, A_HEADS, A_WIDTH,
              B_WIDTH, B_KV_WIDTH, B_KV_WIDTH, IDX_HEADS * IDX_DIM, IDX_DIM, IDX_HEADS,
              C_WIDTH, d, d, d)
    names = ('a_q', 'a_k', 'a_v', 'a_i', 'a_f', 'a_o', 'b_q', 'b_k', 'b_v', 'b_iq', 'b_ik', 'b_iw',
             'c_u', 'g_a', 'g_b', 'g_c')
    off, acc = {}, 0
    for nm, w in zip(names, widths):
        off[nm] = (acc, w)
        acc += w

    def cols(nm):
        s, w = off[nm]
        return w_in[:, s:s + w], b_in[s:s + w]

    ws, bs = [], []
    for nm in _PROJ_PLAIN:
        w, b = cols(nm)
        ws.append(w.astype(BF16))
        bs.append(b.reshape(1, -1))
    wi, bi = cols('a_i')
    wf, bf = cols('a_f')
    ww, bw = cols('b_iw')
    n_rows = 2 * A_HEADS + IDX_HEADS
    w_t = jnp.concatenate([wi, wf, ww], axis=1).T.astype(BF16)
    b_t = jnp.concatenate([bi, bf, bw]).reshape(n_rows, 1)
    w_c = jnp.pad(jnp.concatenate([wi, wf], axis=1), ((0, 0), (0, LANES - 2 * A_HEADS))).astype(BF16)
    b_c = jnp.pad(jnp.concatenate([bi, bf]), (0, LANES - 2 * A_HEADS)).reshape(1, LANES)

    tm = PROJ_TM
    assert n % tm == 0
    const = lambda i: (0, 0)
    in_specs = [pl.BlockSpec((tm, d), lambda i: (i, 0))]
    in_specs += [pl.BlockSpec(w.shape, const) for w in ws]
    in_specs += [pl.BlockSpec(b.shape, const) for b in bs]
    in_specs += [pl.BlockSpec(a.shape, const) for a in (w_t, b_t, w_c, b_c)]
    out_shape = [jax.ShapeDtypeStruct((n, w.shape[1]), F32) for w in ws]
    out_specs = [pl.BlockSpec((tm, w.shape[1]), lambda i: (i, 0)) for w in ws]
    out_shape += [jax.ShapeDtypeStruct((n_rows, n), F32), jax.ShapeDtypeStruct((n, LANES), F32)]
    out_specs += [pl.BlockSpec((n_rows, tm), lambda i: (0, i)), pl.BlockSpec((tm, LANES), lambda i: (i, 0))]
    res = pl.pallas_call(
        _proj_kernel, grid=(n // tm,), in_specs=in_specs, out_specs=out_specs, out_shape=out_shape,
        compiler_params=_cparams(("parallel",)),
    )(x, *ws, *bs, w_t, b_t, w_c, b_c)
    out = dict(zip(_PROJ_PLAIN, res[:len(_PROJ_PLAIN)]))
    out['gate_rows'] = res[-2]
    out['gate_cols'] = res[-1]
    return out


def _mlstm_kernel(*refs, L, S):
    per_seq = [refs[6 * s:6 * s + 6] for s in range(S)]
    br_ref, bc_ref, ng_ref, c0_ref, n0_ref, m0_ref, h_ref, c_out, n_out, m_out, c_s, n_s, m_s = refs[6 * S:]
    c = pl.program_id(1)
    nc = pl.num_programs(1)

    @pl.when(c == 0)
    def _():
        c_s[...] = c0_ref[...]
        n_s[...] = n0_ref[...]
        m_s[...] = m0_ref[...]

    row = lax.broadcasted_iota(I32, (L, L), 0)
    col = lax.broadcasted_iota(I32, (L, L), 1)
    tril = row >= col
    triu = row <= col
    k_scale = A_HEAD_DIM ** -0.5
    gates = []
    for s in range(S):
        gr = per_seq[s][3][0]
        gc = per_seq[s][4][...]
        gates.append((gr, gc, _log_sigmoid(gr + br_ref[...]), _log_sigmoid(gc + bc_ref[...])))
    for s, h in [(s, h) for s in range(S) for h in range(A_HEADS)]:
        q_ref, k_ref, v_ref, _, _, ao_ref = per_seq[s]
        gr, gc, lf_rows, lf_cols = gates[s]
        hs = slice(h * A_HEAD_DIM, (h + 1) * A_HEAD_DIM)
        q = q_ref[:, hs]
        k = k_ref[:, hs] * k_scale
        v = v_ref[:, hs]
        li_r = gr[h:h + 1, :]
        lf_r = lf_rows[A_HEADS + h:A_HEADS + h + 1, :]
        li_c = gc[:, h:h + 1]
        lf_c = lf_cols[:, A_HEADS + h:A_HEADS + h + 1]
        b_c = jnp.sum(jnp.where(tril, lf_r, 0.0), axis=-1, keepdims=True)
        b_r = jnp.sum(jnp.where(triu, lf_c, 0.0), axis=0, keepdims=True)
        m_prev = m_s[s, h]
        c_prev = c_s[s, h]
        n_prev = n_s[s, h]
        d_log = jnp.where(tril, b_c - b_r + li_r, -jnp.inf)
        inter = b_c + m_prev
        m_t = jnp.maximum(inter, jnp.max(d_log, axis=-1, keepdims=True))
        w_inter = jnp.exp(inter - m_t)
        qb = q.astype(BF16)
        kb = k.astype(BF16)
        vb = v.astype(BF16)
        a = jnp.exp(d_log - m_t) * _dot_nt(qb, kb)
        num = w_inter * _dot(qb, c_prev.astype(BF16)) + _dot(a.astype(BF16), vb)
        qn = jnp.sum(qb.astype(F32) * _bf16_round(n_prev), axis=-1, keepdims=True)
        den = w_inter * qn + jnp.sum(a, axis=-1, keepdims=True)
        hh = num / jnp.maximum(jnp.abs(den), jnp.exp(-m_t))
        m_new = m_t[L - 1:L, :]
        b_last = b_c[L - 1:L, :]
        w_c = jnp.exp(b_last - b_c + li_c - m_new)
        decay = jnp.exp(b_last + m_prev - m_new)
        kw = k * w_c
        c_s[s, h] = decay * c_prev + _dot_tn(kw.astype(BF16), vb)
        n_s[s, h] = decay * n_prev + jnp.sum(kb.astype(F32) * _bf16_round(w_c), axis=0, keepdims=True)
        m_s[s, h] = m_new
        mu = jnp.mean(hh, axis=-1, keepdims=True)
        hc = hh - mu
        var = jnp.mean(hc * hc, axis=-1, keepdims=True)
        hn = hc * lax.rsqrt(var + LN_EPS)
        h_ref[s, :, hs] = _sigmoid(ao_ref[:, hs]) * hn * ng_ref[:, hs]

    @pl.when(c == nc - 1)
    def _():
        c_out[...] = c_s[...]
        n_out[...] = n_s[...]
        m_out[...] = m_s[...]


def _mlstm(a_q, a_k, a_v, a_o, gate_rows, gate_cols, b_fgate, a_norm_g, c0, n0, m0, *, row0, n_seq, n_chunks, L):
    assert row0 % L == 0
    rb0 = row0 // L
    n_rows = n_seq * n_chunks * L
    H, dh = A_HEADS, A_HEAD_DIM
    S = MLSTM_SEQS_PER_STEP if n_seq % MLSTM_SEQS_PER_STEP == 0 else 1
    st = lambda b, c: (b, 0, 0, 0)
    const = lambda b, c: (0, 0)
    bias_r = jnp.pad(b_fgate, (A_HEADS, 0)).reshape(2 * A_HEADS, 1)
    bias_c = jnp.pad(b_fgate, (A_HEADS, LANES - 2 * A_HEADS)).reshape(1, LANES)
    seq_specs, seq_args = [], []
    for s in range(S):
        tok = lambda b, c, s=s: (rb0 + (b * S + s) * n_chunks + c, 0)
        seq_specs += [pl.BlockSpec((L, A_WIDTH), tok)] * 3
        seq_specs += [pl.BlockSpec((1, 2 * A_HEADS, L), lambda b, c, s=s: ((b * S + s) * n_chunks + c, 0, 0)),
                      pl.BlockSpec((L, LANES), tok),
                      pl.BlockSpec((L, A_WIDTH), tok)]
        seq_args += [a_q, a_k, a_v, gate_rows, gate_cols, a_o]
    outs = pl.pallas_call(
        functools.partial(_mlstm_kernel, L=L, S=S),
        grid=(n_seq // S, n_chunks),
        in_specs=seq_specs
        + [pl.BlockSpec((2 * A_HEADS, 1), const),
           pl.BlockSpec((1, LANES), const),
           pl.BlockSpec((1, A_WIDTH), const),
           pl.BlockSpec((S, H, dh, dh), st),
           pl.BlockSpec((S, H, 1, dh), st),
           pl.BlockSpec((S, H, 1, 1), st)],
        out_specs=[pl.BlockSpec((S, L, A_WIDTH), lambda b, c: (b, c, 0)),
                   pl.BlockSpec((S, H, dh, dh), st),
                   pl.BlockSpec((S, H, 1, dh), st),
                   pl.BlockSpec((S, H, 1, 1), st)],
        out_shape=[jax.ShapeDtypeStruct((n_seq, n_chunks * L, A_WIDTH), F32),
                   jax.ShapeDtypeStruct((n_seq, H, dh, dh), F32),
                   jax.ShapeDtypeStruct((n_seq, H, 1, dh), F32),
                   jax.ShapeDtypeStruct((n_seq, H, 1, 1), F32)],
        scratch_shapes=[pltpu.VMEM((S, H, dh, dh), F32), pltpu.VMEM((S, H, 1, dh), F32), pltpu.VMEM((S, H, 1, 1), F32)],
        compiler_params=_cparams(("parallel", "arbitrary")),
    )(*seq_args, bias_r, bias_c, a_norm_g.reshape(1, A_WIDTH),
      c0, n0.reshape(n_seq, H, 1, dh), m0.reshape(n_seq, H, 1, 1))
    h, c_new, n_new, m_new = outs
    return h.reshape(n_rows, A_WIDTH), c_new, n_new.reshape(n_seq, H, dh), m_new.reshape(n_seq, H)


def _threshold_search(count_ge, n_keep, shape, n_total):
    def body(j, carry):
        t, cnt_t = carry
        cand = t ^ (jnp.int32(1) << (31 - j))
        cnt = count_ge(cand)
        ok = cnt >= n_keep
        return jnp.where(ok, cand, t), jnp.where(ok, cnt, cnt_t)
    return lax.fori_loop(0, 32, body, (jnp.full(shape, INT_MIN, I32), jnp.full(shape, n_total, F32)))


def _threshold_search_radix4(count_ge, n_keep, shape, n_total):
    def body(j, carry):
        t, cnt_t = carry
        hi = jnp.int32(1) << (31 - 2 * j)
        lo = jnp.int32(1) << (30 - 2 * j)
        cands = (t ^ lo, t ^ hi, t ^ hi ^ lo)
        cnts = [count_ge(c) for c in cands]
        for cand, cnt in zip(cands, cnts):
            ok = cnt >= n_keep
            t, cnt_t = jnp.where(ok, cand, t), jnp.where(ok, cnt, cnt_t)
        return t, cnt_t
    return lax.fori_loop(0, 16, body, (jnp.full(shape, INT_MIN, I32), jnp.full(shape, n_total, F32)))


def _tree_sum(parts):
    parts = list(parts)
    while len(parts) > 1:
        parts = [parts[i] + parts[i + 1] for i in range(0, len(parts) - 1, 2)] + parts[len(parts) & ~1:]
    return parts[0]


def _dsa_prompt_kernel(qi_ref, w_ref, ki_ref, q_ref, k_ref, vt_ref, o_ref, s_ref, acc_ref, m_ref, l_ref,
                       bias_s, lg_s, p_s, *, n_keep):
    QB = QUERY_BLOCK
    U = DSA_UNROLL
    SK = U * QB
    PW = 2 * QB
    i = pl.program_id(1)
    n_trips = (i + U) // U
    key_minus_qry = lax.broadcasted_iota(I32, (QB, QB), 0) - lax.broadcasted_iota(I32, (QB, QB), 1)
    def heads_to_lanes(x_ref, dim):
        per_tile = LANES // dim
        slabs = []
        for tile in range(x_ref.shape[1] // LANES):
            xt = x_ref[:, tile * LANES:(tile + 1) * LANES].T
            slabs += [xt[e * dim:(e + 1) * dim, :] for e in range(per_tile)]
        return (jnp.concatenate(slabs, axis=1) * _exact_pow2(dim ** -0.5)).astype(BF16)

    qi = heads_to_lanes(qi_ref, IDX_DIM)
    w = _bf16_round(w_ref[0, 0] * (IDX_HEADS ** -0.5))

    def score_body(t, carry):
        for u in range(U):
            c = t * U + u
            ki = ki_ref[0, pl.ds(pl.multiple_of(c * QB, QB), QB), :]
            s = jnp.zeros((QB, QB), F32)
            for hp in range(IDX_HEADS // 2):
                d = _dot(ki, qi[:, hp * PW:(hp + 1) * PW])
                r = jnp.maximum(d.astype(BF16), 0).astype(F32)
                for h in (2 * hp, 2 * hp + 1):
                    s = s + r[:, (h % 2) * QB:(h % 2 + 1) * QB] * w[h:h + 1, :]
            visible = key_minus_qry <= (i - c) * QB
            s_ref[c] = jnp.where(visible, _sortable(s), INT_MIN)
        return carry

    lax.fori_loop(0, n_trips, score_body, 0)

    def count(pred):
        def body(t, acc):
            parts = []
            for u in range(U):
                hit = jnp.where(pred(s_ref[t * U + u]), 1.0, 0.0)
                parts += [hit[r * SUBLANES:(r + 1) * SUBLANES, :] for r in range(QB // SUBLANES)]
            return acc + _tree_sum(parts)
        acc = lax.fori_loop(0, n_trips, body, jnp.zeros((SUBLANES, QB), F32))
        return jnp.sum(acc, axis=0, keepdims=True)

    keep = float(n_keep)
    n_total = (n_trips * SK).astype(F32)
    thr, cnt_thr = _threshold_search(lambda cand: count(lambda blk: blk >= cand), keep, (1, QB), n_total)
    has_ties = jnp.max(jnp.where((thr != INT_MIN) & (cnt_thr > keep), 1.0, 0.0)) > 0.5

    m_ref[...] = jnp.full(m_ref.shape, NEG_BIG, F32)
    l_ref[...] = jnp.zeros(l_ref.shape, F32)
    acc_ref[...] = jnp.zeros(acc_ref.shape, F32)
    q = heads_to_lanes(q_ref, B_HEAD_DIM)
    pairs_per_group = B_GROUP // 2

    def fold(x, op):
        parts = [x[r * SUBLANES:(r + 1) * SUBLANES, :] for r in range(QB // SUBLANES)]
        while len(parts) > 1:
            parts = [op(parts[a], parts[a + 1]) for a in range(0, len(parts), 2)]
        return parts[0]

    n_pairs = B_HEADS // 2
    halves = [slice(e * QB, (e + 1) * QB) for e in range(2)]

    def logits_phase(t, slot):
        off = pl.multiple_of(t * SK, SK)
        mx = [[None, None] for _ in range(n_pairs)]
        for u in range(U):
            bias = bias_s[u]
            for g in range(B_KV_HEADS):
                kc = k_ref[0, g, pl.ds(off + u * QB, QB), :]
                for hp in range(g * pairs_per_group, (g + 1) * pairs_per_group):
                    lg = _dot(kc, q[:, hp * PW:(hp + 1) * PW])
                    for e, cs in enumerate(halves):
                        piece = lg[:, cs] + bias
                        lg_s[slot, hp, u, :, cs] = piece
                        pm = fold(piece, jnp.maximum)
                        mx[hp][e] = pm if mx[hp][e] is None else jnp.maximum(mx[hp][e], pm)
        return tuple(jnp.concatenate([jnp.max(mx[hp][e], axis=0, keepdims=True) for e in range(2)], axis=1)
                     for hp in range(n_pairs))

    def values_phase(t, slot, mx):
        off = pl.multiple_of(t * SK, SK)
        alpha = []
        for hp in range(n_pairs):
            m_old = m_ref[hp]
            m_new = jnp.maximum(m_old, mx[hp])
            alpha.append(jnp.exp(m_old - m_new))
            m_ref[hp] = m_new
            ls = [None, None]
            for u in range(U):
                for e, cs in enumerate(halves):
                    p = jnp.exp(lg_s[slot, hp, u, :, cs] - m_new[:, cs])
                    p_s[hp, u * QB:(u + 1) * QB, cs] = p.astype(BF16)
                    ps = fold(p, jnp.add)
                    ls[e] = ps if ls[e] is None else ls[e] + ps
            l_new = jnp.concatenate([jnp.sum(ls[e], axis=0, keepdims=True) for e in range(2)], axis=1)
            l_ref[hp] = alpha[hp] * l_ref[hp] + l_new
        for g in range(B_KV_HEADS):
            vt = vt_ref[0, g, :, pl.ds(off, SK)]
            for hp in range(g * pairs_per_group, (g + 1) * pairs_per_group):
                acc_ref[hp] = alpha[hp] * acc_ref[hp] + _dot(vt, p_s[hp])

    @pl.when(jnp.logical_not(has_ties))
    def _():
        thr_eff = jnp.maximum(thr, INT_MIN + 1)

        def select(t):
            for u in range(U):
                bias_s[u] = jnp.where(s_ref[t * U + u] >= thr_eff, 0.0, NEG_BIG)

        select(0)
        mx0 = logits_phase(0, 0)

        def body(t, mx):
            values_phase(t, 0, mx)
            select(t + 1)
            return logits_phase(t + 1, 0)

        values_phase(n_trips - 1, 0, lax.fori_loop(0, n_trips - 1, body, mx0))

    @pl.when(has_ties)
    def _():
        n_ties = keep - count(lambda blk: blk > thr)
        tri = (lax.broadcasted_iota(I32, (SK, SK), 0) >= lax.broadcasted_iota(I32, (SK, SK), 1)).astype(BF16)

        def body(t, run):
            blk = s_ref[pl.ds(t * U, U)].reshape(SK, QB)
            tie = (blk == thr) & (blk != INT_MIN)
            prefix = _dot(tri, jnp.where(tie, 1.0, 0.0).astype(BF16))
            sel = (blk > thr) | (tie & (run + prefix <= n_ties))
            bias_s[...] = jnp.where(sel, 0.0, NEG_BIG).reshape(U, QB, QB)
            values_phase(t, 0, logits_phase(t, 0))
            return run + prefix[SK - 1:SK, :]

        lax.fori_loop(0, n_trips, body, jnp.zeros((1, QB), F32))

    for hp in range(B_HEADS // 2):
        o = acc_ref[hp] / l_ref[hp]
        o_ref[:, hp * LANES:(hp + 1) * LANES] = jnp.concatenate([o[:, :QB], o[:, QB:]], axis=0).T


def _dsa_prompt(b_q, b_iq, iw_rows, b_k, b_v, b_ik, *, B, T):
    QB = QUERY_BLOCK
    nb = T // QB
    n_keep = min(MAX_KEEP, T // 4)

    assert nb % DSA_UNROLL == 0
    w_t = jnp.transpose(iw_rows.reshape(IDX_HEADS, B, nb, QB), (1, 2, 0, 3))
    ki = b_ik.reshape(B, T, IDX_DIM).astype(BF16)
    kh = jnp.transpose(b_k.reshape(B, T, B_KV_HEADS, B_HEAD_DIM), (0, 2, 1, 3)).astype(BF16)
    vt = jnp.transpose(b_v.reshape(B, T, B_KV_HEADS, B_HEAD_DIM), (0, 2, 3, 1)).astype(BF16)
    n_pairs = B_HEADS // 2
    tok = lambda b, i: (b * nb + i, 0)
    return pl.pallas_call(
        functools.partial(_dsa_prompt_kernel, n_keep=n_keep),
        grid=(B, nb),
        in_specs=[pl.BlockSpec((QB, IDX_HEADS * IDX_DIM), tok),
                  pl.BlockSpec((1, 1, IDX_HEADS, QB), lambda b, i: (b, i, 0, 0)),
                  pl.BlockSpec((1, T, IDX_DIM), lambda b, i: (b, 0, 0)),
                  pl.BlockSpec((QB, B_WIDTH), tok),
                  pl.BlockSpec((1, B_KV_HEADS, T, B_HEAD_DIM), lambda b, i: (b, 0, 0, 0)),
                  pl.BlockSpec((1, B_KV_HEADS, B_HEAD_DIM, T), lambda b, i: (b, 0, 0, 0))],
        out_specs=pl.BlockSpec((QB, B_WIDTH), tok),
        out_shape=jax.ShapeDtypeStruct((B * T, B_WIDTH), F32),
        scratch_shapes=[pltpu.VMEM((nb, QB, QB), I32),
                        pltpu.VMEM((n_pairs, B_HEAD_DIM, 2 * QB), F32),
                        pltpu.VMEM((n_pairs, 1, 2 * QB), F32),
                        pltpu.VMEM((n_pairs, 1, 2 * QB), F32),
                        pltpu.VMEM((DSA_UNROLL, QB, QB), F32),
                        pltpu.VMEM((1, n_pairs, DSA_UNROLL, QB, 2 * QB), F32),
                        pltpu.VMEM((n_pairs, DSA_UNROLL * QB, 2 * QB), BF16)],
        compiler_params=_cparams(("parallel", "arbitrary")),
    )(b_iq, w_t, ki, b_q, kh, vt)


def _dsa_decode_kernel(pt_ref, qi_ref, w_ref, q_ref, *rest, n_keep, n_q, pps):
    ck_refs, cv_refs, cki_refs = rest[0:pps], rest[pps:2 * pps], rest[2 * pps:3 * pps]
    nk_ref, nv_ref, nki_ref, o_ref, s_ref, k_s, v_s = rest[3 * pps:]
    P = LANES
    p = pl.program_id(1)
    n_steps = pl.num_programs(1)
    n_rows = IDX_HEADS * n_q
    qi = qi_ref[0]
    w = _bf16_round(w_ref[0] * (IDX_HEADS ** -0.5))

    def chunk_scores(ki_t):
        t = _bf16_round(jnp.maximum(_dot(qi, ki_t), 0.0)) * w
        s = jnp.zeros((n_q, ki_t.shape[1]), F32)
        for h in range(IDX_HEADS):
            s = s + t[h * n_q:(h + 1) * n_q, :]
        return s

    off = pl.multiple_of(p * (pps * P), pps * P)
    pages = lambda refs: jnp.concatenate([r[0, 0].astype(BF16) for r in refs], axis=1)
    s_ref[:, pl.ds(off, pps * P)] = _sortable(chunk_scores(pages(cki_refs)))
    k_s[:, pl.ds(off, pps * P)] = pages(ck_refs)
    v_s[:, pl.ds(off, pps * P)] = pages(cv_refs)

    @pl.when(p == n_steps - 1)
    def _():
        n_keys = s_ref.shape[1]
        past = n_keys - P
        n_chunks = n_keys // P
        key_j = lax.broadcasted_iota(I32, (n_q, P), 1)
        qry_t = lax.broadcasted_iota(I32, (n_q, P), 0)
        visible = key_j <= qry_t
        s_ref[:, pl.ds(past, P)] = jnp.where(visible, _sortable(chunk_scores(nki_ref[0].astype(BF16))), INT_MIN)
        k_s[:, pl.ds(past, P)] = nk_ref[0].astype(BF16)
        v_s[:, pl.ds(past, P)] = nv_ref[0].astype(BF16)

        def count(pred):
            hit = jnp.where(pred(s_ref[...]), 1.0, 0.0)
            return jnp.sum(_tree_sum([hit[:, c * P:(c + 1) * P] for c in range(n_chunks)]), axis=-1, keepdims=True)

        keep = float(n_keep)
        thr, cnt_thr = _threshold_search_radix4(lambda cand: count(lambda blk: blk >= cand), keep, (n_q, 1),
                                                float(n_keys))
        has_ties = jnp.max(jnp.where((thr != INT_MIN) & (cnt_thr > keep), 1.0, 0.0)) > 0.5

        @pl.when(jnp.logical_not(has_ties))
        def _():
            thr_eff = jnp.maximum(thr, INT_MIN + 1)
            s_ref[...] = jnp.where(s_ref[...] >= thr_eff, 1, 0).astype(I32)

        @pl.when(has_ties)
        def _():
            n_ties = keep - count(lambda blk: blk > thr)
            tri = (lax.broadcasted_iota(I32, (P, P), 0) <= lax.broadcasted_iota(I32, (P, P), 1)).astype(BF16)

            def sel_body(c, run):
                o = pl.multiple_of(c * P, P)
                blk = s_ref[:, pl.ds(o, P)]
                tie = (blk == thr) & (blk != INT_MIN)
                prefix = _dot(jnp.where(tie, 1.0, 0.0).astype(BF16), tri)
                sel = (blk > thr) | (tie & (run + prefix <= n_ties))
                s_ref[:, pl.ds(o, P)] = jnp.where(sel, 1, 0).astype(I32)
                return run + prefix[:, P - 1:P]

            lax.fori_loop(0, n_chunks, sel_body, jnp.zeros((n_q, 1), F32))

        sel = s_ref[...] > 0
        q = q_ref[0]
        lg = _dot(q, k_s[...])
        L = lg.shape[1]
        lg = jnp.where(sel[None], lg.reshape(B_HEADS, n_q, L), NEG_BIG)
        m = jnp.max(lg, axis=-1, keepdims=True)
        pr = jnp.exp(lg - m)
        pr = pr / jnp.sum(pr, axis=-1, keepdims=True)
        o_ref[0] = _dot_nt(pr.reshape(n_rows, L).astype(BF16), v_s[...])


def _dsa_decode(b_q, b_iq, iw_rows, b_k, b_v, b_ik, cache_kt, cache_vt, cache_kit, page_table, *, layer, Bd, Tq):
    page = cache_kt.shape[-1]
    assert page == LANES
    n_pages = page_table.shape[1]
    past = n_pages * page
    n_keep = min(MAX_KEEP, (past + Tq) // 4)
    n_rows = B_HEADS * Tq
    kvw = B_KV_HEADS * B_HEAD_DIM

    def rows_hq(a, heads, dim):
        return jnp.transpose(a.reshape(Bd, Tq, heads, dim), (0, 2, 1, 3)).reshape(Bd, heads * Tq, dim)

    qi = rows_hq(b_iq, IDX_HEADS, IDX_DIM).astype(BF16) * _exact_pow2(IDX_DIM ** -0.5)
    w = jnp.transpose(iw_rows.reshape(IDX_HEADS, Bd, Tq), (1, 0, 2)).reshape(Bd, IDX_HEADS * Tq, 1)
    qh = rows_hq(b_q, B_HEADS, B_HEAD_DIM).astype(BF16) * _exact_pow2(B_HEAD_DIM ** -0.5)
    group = (jnp.arange(n_rows) // Tq) // B_GROUP
    lane_group = jnp.arange(kvw) // B_HEAD_DIM
    q_wide = jnp.where(group[:, None] == lane_group[None, :], jnp.tile(qh, (1, 1, B_KV_HEADS)), 0.0).astype(BF16)
    pps = DECODE_PAGES_PER_STEP if n_pages % DECODE_PAGES_PER_STEP == 0 else 1

    def pad_new(a, width):
        return jnp.pad(jnp.transpose(a.reshape(Bd, Tq, width), (0, 2, 1)), ((0, 0), (0, 0), (0, page - Tq)))

    nk, nv, nki = pad_new(b_k, kvw), pad_new(b_v, kvw), pad_new(b_ik, IDX_DIM)
    n_keys = past + page
    seq = lambda b, p, pt: (b, 0, 0)
    pages = lambda width: [pl.BlockSpec((1, 1, width, page), lambda b, p, pt, j=j: (layer, pt[b, p * pps + j], 0, 0))
                           for j in range(pps)]
    out = pl.pallas_call(
        functools.partial(_dsa_decode_kernel, n_keep=n_keep, n_q=Tq, pps=pps),
        grid_spec=pltpu.PrefetchScalarGridSpec(
            num_scalar_prefetch=1,
            grid=(Bd, n_pages // pps),
            in_specs=[pl.BlockSpec((1, n_rows, IDX_DIM), seq),
                      pl.BlockSpec((1, n_rows, 1), seq),
                      pl.BlockSpec((1, n_rows, kvw), seq)]
            + pages(kvw) + pages(kvw) + pages(IDX_DIM)
            + [pl.BlockSpec((1, kvw, page), seq),
               pl.BlockSpec((1, kvw, page), seq),
               pl.BlockSpec((1, IDX_DIM, page), seq)],
            out_specs=pl.BlockSpec((1, n_rows, kvw), seq),
            scratch_shapes=[pltpu.VMEM((Tq, n_keys), I32),
                            pltpu.VMEM((kvw, n_keys), BF16),
                            pltpu.VMEM((kvw, n_keys), BF16)]),
        out_shape=jax.ShapeDtypeStruct((Bd, n_rows, kvw), F32),
        compiler_params=_cparams(("parallel", "arbitrary")),
    )(page_table, qi, w, q_wide, *([cache_kt] * pps), *([cache_vt] * pps), *([cache_kit] * pps), nk, nv, nki)
    out = out.reshape(Bd, B_HEADS, Tq, B_KV_HEADS, B_HEAD_DIM)
    out = jnp.concatenate([out[:, g * B_GROUP:(g + 1) * B_GROUP, :, g] for g in range(B_KV_HEADS)], axis=1)
    return jnp.transpose(out, (0, 2, 1, 3)).reshape(Bd * Tq, B_WIDTH)


def _pool_kernel(u_ref, prev_ref, halo0_ref, cmap_ref, cscale_ref, o_ref, ext, *, Tb, pos0):
    j = pl.program_id(1)
    H = POOL_HALO
    ext[0:H, :] = jnp.where(j == 0, halo0_ref[0], prev_ref[0])
    ext[H:H + Tb, :] = u_ref[0]
    pos = pos0 + j * Tb + lax.broadcasted_iota(I32, (Tb, 1), 0)
    for g, wdw in enumerate(C_WINDOWS):
        ls = slice(g * C_GROUP_WIDTH, (g + 1) * C_GROUP_WIDTH)
        win = ext[H:H + Tb, ls]
        for d in range(1, wdw):
            win = win + ext[H - d:H - d + Tb, ls]
        cnt = jnp.minimum(wdw, pos + 1).astype(F32)
        pooled = win / cnt - ext[H:H + Tb, ls]
        z = _dot(pooled.astype(BF16), cmap_ref[g])
        o_ref[0, :, ls] = z * cscale_ref[:, ls]


def _pool(u, prev_src, halo0, c_map, c_scale, *, Tb, pos0):
    Bn, R, C = u.shape
    H = POOL_HALO
    assert R % Tb == 0 and (Tb % H == 0 or R == Tb)
    prev_idx = (lambda b, j: (b, jnp.maximum(j * (Tb // H) - 1, 0), 0)) if R > Tb else (lambda b, j: (b, 0, 0))
    return pl.pallas_call(
        functools.partial(_pool_kernel, Tb=Tb, pos0=pos0),
        grid=(Bn, R // Tb),
        in_specs=[pl.BlockSpec((1, Tb, C), lambda b, j: (b, j, 0)),
                  pl.BlockSpec((1, H, C), prev_idx),
                  pl.BlockSpec((1, H, C), lambda b, j: (b, 0, 0)),
                  pl.BlockSpec(c_map.shape, lambda b, j: (0, 0, 0)),
                  pl.BlockSpec((1, C), lambda b, j: (0, 0))],
        out_specs=pl.BlockSpec((1, Tb, C), lambda b, j: (b, j, 0)),
        out_shape=jax.ShapeDtypeStruct((Bn, R, C), F32),
        scratch_shapes=[pltpu.VMEM((H + Tb, C), F32)],
        compiler_params=_cparams(("parallel", "parallel")),
    )(u, prev_src, halo0, c_map.astype(BF16), c_scale.reshape(1, C))


def _merge_kernel(x_ref, hap_ref, had_ref, hbp_ref, hbd_ref, hcp_ref, hcd_ref, ga_ref, gb_ref, gc_ref,
                  pa_ref, pb_ref, pc_ref, wo_ref, g1_ref, b1_ref, o_ref, *, alpha, n_prompt_tiles):
    is_prompt = pl.program_id(0) < n_prompt_tiles
    h_a, h_b, h_c = [jnp.where(is_prompt, p_ref[...], d_ref[...]).astype(BF16)
                     for p_ref, d_ref in ((hap_ref, had_ref), (hbp_ref, hbd_ref), (hcp_ref, hcd_ref))]
    merged = (_sigmoid(ga_ref[...]) * _dot(h_a, pa_ref[...])
              + _sigmoid(gb_ref[...]) * _dot(h_b, pb_ref[...])
              + _sigmoid(gc_ref[...]) * _dot(h_c, pc_ref[...]))
    y = alpha * x_ref[...] + _dot(merged.astype(BF16), wo_ref[...])
    o_ref[...] = _layer_norm(y, g1_ref[...], b1_ref[...])


def _merge(x, h_prompt, h_decode, g_a, g_b, g_c, p_a, p_b, p_c, w_out, ln_g, ln_b, *, alpha):
    n, d = x.shape
    tm = MERGE_TM
    n_p, n_d = h_prompt[0].shape[0], h_decode[0].shape[0]
    assert n_p % tm == 0 and n_d == tm and n == n_p + n_d
    n_prompt_tiles = n_p // tm
    tok = lambda w: pl.BlockSpec((tm, w), lambda i: (i, 0))
    prompt = lambda w: pl.BlockSpec((tm, w), lambda i: (jnp.minimum(i, n_prompt_tiles - 1), 0))
    full = lambda a: pl.BlockSpec(a.shape, lambda i: (0, 0))
    ws = [p_a.astype(BF16), p_b.astype(BF16), p_c.astype(BF16), w_out.astype(BF16), ln_g.reshape(1, d), ln_b.reshape(1, d)]
    mixers = [a for pair in zip(h_prompt, h_decode) for a in pair]
    mixer_specs = [spec for hp, hd in zip(h_prompt, h_decode) for spec in (prompt(hp.shape[1]), full(hd))]
    return pl.pallas_call(
        functools.partial(_merge_kernel, alpha=alpha, n_prompt_tiles=n_prompt_tiles),
        grid=(n // tm,),
        in_specs=[tok(d)] + mixer_specs + [tok(d), tok(d), tok(d)] + [full(a) for a in ws],
        out_specs=tok(d),
        out_shape=jax.ShapeDtypeStruct((n, d), F32),
        compiler_params=_cparams(("parallel",)),
    )(x, *mixers, g_a, g_b, g_c, *ws)


def _router_kernel(x_ref, w_ref, rb_ref, o_ref):
    tm = x_ref.shape[0]
    G = N_EXPERT_GROUPS
    per_group = N_EXPERTS // G
    assert per_group == SUBLANES
    neg = -jnp.inf
    scores = _sigmoid(_dot_nt(w_ref[...], x_ref[...].astype(BF16)))
    biased = scores + rb_ref[...]
    sc = [scores[g * per_group:(g + 1) * per_group, :] for g in range(G)]
    bg = [biased[g * per_group:(g + 1) * per_group, :] for g in range(G)]
    sub = lax.broadcasted_iota(I32, (per_group, tm), 0).astype(F32)

    def colmax(a):
        return jnp.max(a, axis=0, keepdims=True)

    def first_in_group(a, m):
        return jnp.min(jnp.where(a == m, sub, float(per_group)), axis=0, keepdims=True)

    gscore = []
    for g in range(G):
        m1 = colmax(bg[g])
        rest = jnp.where(sub == first_in_group(bg[g], m1), neg, bg[g])
        gscore.append(m1 + colmax(rest))
    rem = []
    for g in range(G):
        beaten = jnp.zeros((1, tm), F32)
        for o in range(G):
            if o != g:
                wins = (gscore[o] > gscore[g]) | ((gscore[o] == gscore[g]) & (o < g))
                beaten = beaten + jnp.where(wins, 1.0, 0.0)
        rem.append(jnp.where(beaten < float(TOPK_GROUPS), bg[g], neg))
    chosen = [jnp.zeros((per_group, tm), jnp.bool_) for _ in range(G)]
    for _ in range(TOP_K):
        best = colmax(functools.reduce(jnp.maximum, rem))
        idx = functools.reduce(jnp.minimum, [
            jnp.min(jnp.where(rem[g] == best, sub + float(g * per_group), float(N_EXPERTS)), axis=0, keepdims=True)
            for g in range(G)])
        for g in range(G):
            hit = (sub + float(g * per_group)) == idx
            chosen[g] = chosen[g] | hit
            rem[g] = jnp.where(hit, neg, rem[g])
    s_sel = [jnp.where(chosen[g], sc[g], 0.0) for g in range(G)]
    total = jnp.sum(functools.reduce(jnp.add, s_sel), axis=0, keepdims=True)
    gates = jnp.concatenate([ROUTED_SCALE * s / total for s in s_sel]
                            + [jnp.zeros((LANES - N_EXPERTS, tm), F32)], axis=0)
    o_ref[...] = gates.T


def _router(x, router_w, router_b):
    n, d = x.shape
    tm = ROUTER_TM
    assert n % tm == 0
    w = router_w.T.astype(BF16)
    full = lambda a: pl.BlockSpec(a.shape, lambda i: (0, 0))
    rb = router_b.reshape(N_EXPERTS, 1)
    return pl.pallas_call(
        _router_kernel, grid=(n // tm,),
        in_specs=[pl.BlockSpec((tm, d), lambda i: (i, 0)), full(w), full(rb)],
        out_specs=pl.BlockSpec((tm, LANES), lambda i: (i, 0)),
        out_shape=jax.ShapeDtypeStruct((n, LANES), F32),
        compiler_params=_cparams(("parallel",)),
    )(x, w, rb)


def _moe_kernel(x_ref, g_ref, wgu_ref, wd_ref, sgu_ref, sd_ref, g2_ref, b2_ref, o_ref, xb_s, *, alpha):
    e = pl.program_id(1)
    n_e = pl.num_programs(1)

    def swiglu(w_gu, w_down):
        gu = _dot(xb_s[...], w_gu)
        gg = gu[:, :EXPERT_DIM]
        act = gg * _sigmoid(gg) * gu[:, EXPERT_DIM:]
        return _dot(act.astype(BF16), w_down)

    @pl.when(e == 0)
    def _():
        xb_s[...] = x_ref[...].astype(BF16)
        o_ref[...] = swiglu(sgu_ref[...], sd_ref[...])

    gates = g_ref[...]
    lane = lax.broadcasted_iota(I32, gates.shape, 1)
    gate = jnp.sum(jnp.where(lane == e, gates, 0.0), axis=-1, keepdims=True)
    y = swiglu(wgu_ref[0, 0], wd_ref[0, 0])
    o_ref[...] += jnp.where(gate != 0.0, y * gate, 0.0)

    @pl.when(e == n_e - 1)
    def _():
        o_ref[...] = _layer_norm(alpha * x_ref[...] + o_ref[...], g2_ref[...], b2_ref[...])


def _moe(x, gates, wgu, wd, sgu, sd, ln_g, ln_b, *, layer, alpha, tm):
    n, d = x.shape
    n_e = wgu.shape[1]
    assert n % tm == 0
    const = lambda i, e: (0, 0)
    return pl.pallas_call(
        functools.partial(_moe_kernel, alpha=alpha),
        grid=(n // tm, n_e),
        in_specs=[pl.BlockSpec((tm, d), lambda i, e: (i, 0)),
                  pl.BlockSpec((tm, LANES), lambda i, e: (i, 0)),
                  pl.BlockSpec((1, 1) + wgu.shape[2:], lambda i, e: (layer, e, 0, 0)),
                  pl.BlockSpec((1, 1) + wd.shape[2:], lambda i, e: (layer, e, 0, 0)),
                  pl.BlockSpec(sgu.shape, const),
                  pl.BlockSpec(sd.shape, const),
                  pl.BlockSpec((1, d), const),
                  pl.BlockSpec((1, d), const)],
        out_specs=pl.BlockSpec((tm, d), lambda i, e: (i, 0)),
        out_shape=jax.ShapeDtypeStruct((n, d), F32),
        scratch_shapes=[pltpu.VMEM((tm, d), BF16)],
        compiler_params=_cparams(("parallel", "arbitrary")),
    )(x, gates, wgu, wd, sgu, sd, ln_g.reshape(1, d), ln_b.reshape(1, d))


def _moe_tile(n):
    for cand in (1280, 1024, 512, 256, 128):
        if n % cand == 0:
            return cand
    return n


def kernel(x_prompt, x_sample, cache_k, cache_v, cache_kidx, state_C, state_n, state_m, state_pool, page_table, w_in, b_in, b_fgate, a_norm_g, c_map, c_scale, p_a, p_b, p_c, w_out, ln1_g, ln1_b, router_w, router_b, exp_gu, exp_down, sh_gu, sh_down, ln2_g, ln2_b):
    B, T, D = x_prompt.shape
    Bd, Td, _ = x_sample.shape
    depth = w_in.shape[0]
    alpha = (2 * depth) ** 0.25
    n_p, n_d = B * T, Bd * Td
    n = n_p + n_d
    L = MLSTM_CHUNK
    nc = T // L
    past_len = page_table.shape[1] * cache_k.shape[2]
    H, dh = A_HEADS, A_HEAD_DIM

    n_pool, page = cache_k.shape[1:3]
    cache_kt = jnp.transpose(cache_k, (0, 1, 3, 4, 2)).reshape(depth, n_pool, B_KV_WIDTH, page)
    cache_vt = jnp.transpose(cache_v, (0, 1, 3, 4, 2)).reshape(depth, n_pool, B_KV_WIDTH, page)
    cache_kit = jnp.transpose(cache_kidx, (0, 1, 3, 2))
    exp_gu_b = exp_gu.astype(BF16)
    exp_down_b = exp_down.astype(BF16)

    x = jnp.concatenate([x_prompt.reshape(n_p, D), x_sample.reshape(n_d, D)], axis=0)
    new_p, new_s = [], []
    for l in range(depth):
        pr = _project(x, w_in[l], b_in[l])
        gate_rows = pr['gate_rows']
        gr_p = jnp.transpose(gate_rows[:2 * H, :n_p].reshape(2 * H, B * nc, L), (1, 0, 2))
        gr_d = jnp.transpose(gate_rows[:2 * H, n_p:].reshape(2 * H, Bd, Td), (1, 0, 2))
        iw_p = gate_rows[2 * H:, :n_p]
        iw_d = gate_rows[2 * H:, n_p:]

        zc = jnp.zeros((B, H, dh, dh), F32)
        zn = jnp.zeros((B, H, dh), F32)
        zm = jnp.full((B, H), -jnp.inf, F32)
        ml = functools.partial(_mlstm, pr['a_q'], pr['a_k'], pr['a_v'], pr['a_o'])
        ha_p, c_p, nn_p, m_p = ml(gr_p, pr['gate_cols'], b_fgate[l], a_norm_g[l], zc, zn, zm,
                                  row0=0, n_seq=B, n_chunks=nc, L=L)
        ha_d, c_d, nn_d, m_d = ml(gr_d, pr['gate_cols'], b_fgate[l], a_norm_g[l], state_C[l], state_n[l], state_m[l],
                                  row0=n_p, n_seq=Bd, n_chunks=1, L=Td)

        bk, bv, bik = pr['b_k'], pr['b_v'], pr['b_ik']
        hb_p = _dsa_prompt(pr['b_q'], pr['b_iq'], iw_p, bk[:n_p], bv[:n_p], bik[:n_p], B=B, T=T)
        hb_d = _dsa_decode(pr['b_q'][n_p:], pr['b_iq'][n_p:], iw_d, bk[n_p:], bv[n_p:], bik[n_p:],
                           cache_kt, cache_vt, cache_kit, page_table, layer=l, Bd=Bd, Tq=Td)

        cu = pr['c_u']
        cu_p = cu[:n_p].reshape(B, T, C_WIDTH)
        cu_d = cu[n_p:].reshape(Bd, Td, C_WIDTH)
        zero_halo = jnp.zeros((B, POOL_HALO, C_WIDTH), F32)
        hc_p = _pool(cu_p, cu_p, zero_halo, c_map[l], c_scale[l], Tb=min(T, 512), pos0=0)
        halo_d = jnp.pad(state_pool[l], ((0, 0), (POOL_HALO - POOL_BUF, 0), (0, 0)))
        hc_d = _pool(cu_d, halo_d, halo_d, c_map[l], c_scale[l], Tb=Td, pos0=past_len)

        x1 = _merge(x, (ha_p, hb_p, hc_p.reshape(n_p, C_WIDTH)), (ha_d, hb_d, hc_d.reshape(n_d, C_WIDTH)),
                    pr['g_a'], pr['g_b'], pr['g_c'], p_a[l], p_b[l], p_c[l], w_out[l], ln1_g[l], ln1_b[l], alpha=alpha)
        gates = _router(x1, router_w[l], router_b[l])
        x = _moe(x1, gates, exp_gu_b, exp_down_b, sh_gu[l].astype(BF16), sh_down[l].astype(BF16),
                 ln2_g[l], ln2_b[l], layer=l, alpha=alpha, tm=_moe_tile(n))

        kvs = (B_KV_HEADS, B_HEAD_DIM)
        pool_d = jnp.concatenate([state_pool[l], cu_d], axis=1)[:, -POOL_BUF:]
        new_p.append((bk[:n_p].reshape(B, T, *kvs), bv[:n_p].reshape(B, T, *kvs), bik[:n_p].reshape(B, T, IDX_DIM),
                      c_p, nn_p, m_p, cu_p[:, T - POOL_BUF:]))
        new_s.append((bk[n_p:].reshape(Bd, Td, *kvs), bv[n_p:].reshape(Bd, Td, *kvs),
                      bik[n_p:].reshape(Bd, Td, IDX_DIM), c_d, nn_d, m_d, pool_d))

    def stacked(states, i):
        return jnp.stack([s[i] for s in states])

    outs = [x[:n_p].reshape(B, T, D), x[n_p:].reshape(Bd, Td, D)]
    for i in range(7):
        outs += [stacked(new_p, i), stacked(new_s, i)]
    return tuple(outs)
```
